```python
import math
import jax, jax.numpy as jnp
from jax import lax
import numpy as np

D_MODEL = 2048
BATCH = 2
SEQ = 4096
DEPTH = 1
DEC_BATCH = 4
DEC_SEQ = 8192
PAST_LEN = 128

MIX_WIDTH = D_MODEL
ATTN_WIDTH = MIX_WIDTH // 2
LRU_WIDTH = MIX_WIDTH - ATTN_WIDTH
HEAD_DIM = 128
N_HEADS = ATTN_WIDTH // HEAD_DIM
N_KV_HEADS = 2
GQA_GROUP = N_HEADS // N_KV_HEADS
WINDOW = 128
BLOCK = 128
N_BUCKETS = 32
MAX_DISTANCE = 128
N_META = 16
LRU_BLOCKS = 8
LRU_BLOCK_W = LRU_WIDTH // LRU_BLOCKS
LRU_C = 8.0
CONV_WIDTH = 4
CONV_LEFT = 2
N_EXPERTS = 32
TOP_K = 4
D_FF = D_MODEL
SWIGLU_LIMIT = 7.0
SWIGLU_ALPHA = 1.702
EXPERT_BLOCK = 256
DN_ALPHA = (2.0 * DEPTH) ** 0.25
DN_BETA = (8.0 * DEPTH) ** -0.25
LN_EPS = 1e-5
NEG_INF = -1e30
Q_COLS = N_HEADS * HEAD_DIM
KV_COLS = N_KV_HEADS * HEAD_DIM
IN_COLS = Q_COLS + 2 * KV_COLS + 2 * LRU_WIDTH

kernel_name = 'hymba_swa_rglru_moe_encoder'


def layer_norm(x, g, b):
    xf = x.astype(jnp.float32)
    mu = jnp.mean(xf, axis=-1, keepdims=True)
    var = jnp.mean(jnp.square(xf - mu), axis=-1, keepdims=True)
    out = (xf - mu) * lax.rsqrt(var + LN_EPS) * g.astype(jnp.float32) + b.astype(jnp.float32)
    return out.astype(x.dtype)


def t5_bucket(rel):
    half = N_BUCKETS // 2
    exact = half // 2
    n = jnp.abs(rel)
    large = exact + (jnp.log(jnp.maximum(n, 1).astype(jnp.float32) / exact)
                     / math.log(MAX_DISTANCE / exact) * (half - exact)).astype(jnp.int32)
    large = jnp.minimum(large, half - 1)
    return jnp.where(rel > 0, half, 0) + jnp.where(n < exact, n, large)


def windowed_attention(q, k, v, rel_bias, sink):
    B, L = q.shape[0], q.shape[1]
    front = BLOCK - N_META
    nblk = -(-(front + L) // BLOCK)
    back = nblk * BLOCK - front - L
    qb = jnp.pad(q, ((0, 0), (front, back), (0, 0), (0, 0))).reshape(
        B, nblk, BLOCK, N_KV_HEADS, GQA_GROUP, HEAD_DIM)

    def band(t):
        tp = jnp.pad(t, ((0, 0), (front + BLOCK, back + BLOCK), (0, 0), (0, 0))).reshape(
            B, nblk + 2, BLOCK, N_KV_HEADS, HEAD_DIM)
        return jnp.concatenate([tp[:, :-2], tp[:, 1:-1], tp[:, 2:]], axis=2)

    kb, vb = band(k), band(v)
    km, vm = k[:, :N_META], v[:, :N_META]

    qi = jnp.arange(BLOCK)
    kj = jnp.arange(3 * BLOCK)
    rel_band = kj[None, :] - BLOCK - qi[:, None]
    kpos = (jnp.arange(nblk)[:, None] - 1) * BLOCK + kj[None, :]
    real = (kpos >= BLOCK) & (kpos < BLOCK + (L - N_META))
    band_mask = (jnp.abs(rel_band) <= WINDOW)[None] & real[:, None, :]
    band_bias = rel_bias[t5_bucket(rel_band)].transpose(2, 0, 1).reshape(
        N_KV_HEADS, GQA_GROUP, BLOCK, 3 * BLOCK)
    qpos = jnp.arange(nblk * BLOCK).reshape(nblk, BLOCK)
    rel_meta = (front + jnp.arange(N_META))[None, None, :] - qpos[:, :, None]
    meta_bias = rel_bias[t5_bucket(rel_meta)].transpose(0, 3, 1, 2).reshape(
        nblk, N_KV_HEADS, GQA_GROUP, BLOCK, N_META)

    scale = HEAD_DIM ** -0.5
    s_band = jnp.einsum('bnqhgd,bnkhd->bnhgqk', qb, kb).astype(jnp.float32) * scale \
        + band_bias.astype(jnp.float32)
    s_band = jnp.where(band_mask[None, :, None, None], s_band, NEG_INF)
    s_meta = jnp.einsum('bnqhgd,bmhd->bnhgqm', qb, km).astype(jnp.float32) * scale \
        + meta_bias.astype(jnp.float32)
    s_sink = jnp.broadcast_to(sink.astype(jnp.float32).reshape(N_KV_HEADS, GQA_GROUP, 1, 1),
                              s_meta.shape[:-1] + (1,))
    p = jax.nn.softmax(jnp.concatenate([s_band, s_meta, s_sink], axis=-1), axis=-1).astype(v.dtype)
    o = jnp.einsum('bnhgqk,bnkhd->bnqhgd', p[..., :3 * BLOCK], vb) \
        + jnp.einsum('bnhgqm,bmhd->bnqhgd', p[..., 3 * BLOCK:3 * BLOCK + N_META], vm)
    return o.reshape(B, nblk * BLOCK, N_HEADS * HEAD_DIM)[:, front:front + L]


def linear_scan(a, u, reverse):
    def combine(left, right):
        a1, b1 = left
        a2, b2 = right
        return a1 * a2, a2 * b1 + b2
    return lax.associative_scan(combine, (a, u), reverse=reverse, axis=1)[1]


def rg_lru_branch(xr, yg, conv_w, conv_b, wa, ba, wi, bi, lam):
    B, L = xr.shape[0], xr.shape[1]
    xp = jnp.pad(xr, ((0, 0), (CONV_LEFT, CONV_WIDTH - 1 - CONV_LEFT), (0, 0)))
    xc = conv_b + conv_w[0] * xp[:, 0:L]
    for t in range(1, CONV_WIDTH):
        xc = xc + conv_w[t] * xp[:, t:t + L]
    xblk = xc.reshape(B, L, LRU_BLOCKS, LRU_BLOCK_W)
    gate_a = jnp.einsum('blnc,zncj->zblnj', xblk, wa).reshape(2, B, L, LRU_WIDTH) + ba[:, None, None, :]
    gate_i = jnp.einsum('blnc,zncj->zblnj', xblk, wi).reshape(2, B, L, LRU_WIDTH) + bi[:, None, None, :]
    r = jax.nn.sigmoid(gate_a.astype(jnp.float32))
    i = jax.nn.sigmoid(gate_i.astype(jnp.float32))
    log_a = -LRU_C * r * jax.nn.softplus(-lam.astype(jnp.float32))[:, None, None, :]
    a = jnp.exp(log_a)
    u = jnp.sqrt(-jnp.expm1(2.0 * log_a)) * (i * xc.astype(jnp.float32)[None])
    h = linear_scan(a[0], u[0], False) + linear_scan(a[1], u[1], True)
    return (h * jax.nn.gelu(yg.astype(jnp.float32))).astype(xr.dtype)


def moe_ffn(h, router_w, router_b, w1, b1, w2, b2):
    T = h.shape[0]
    logits = (h @ router_w).astype(jnp.float32) + router_b.astype(jnp.float32)
    top_v, top_e = lax.top_k(logits, TOP_K)
    gates = jax.nn.softmax(top_v, axis=-1)
    flat_e = top_e.reshape(-1).astype(jnp.int32)
    flat_tok = jnp.repeat(jnp.arange(T, dtype=jnp.int32), TOP_K)
    flat_g = gates.reshape(-1)
    order = jnp.argsort(flat_e)
    se, stok, sg = flat_e[order], flat_tok[order], flat_g[order]
    counts = jnp.bincount(flat_e, length=N_EXPERTS).astype(jnp.int32)
    padded = (counts + EXPERT_BLOCK - 1) // EXPERT_BLOCK * EXPERT_BLOCK
    start = jnp.cumsum(counts) - counts
    pend = jnp.cumsum(padded)
    pstart = pend - padded
    dest = pstart[se] + jnp.arange(T * TOP_K, dtype=jnp.int32) - start[se]
    n_blocks = -(-(T * TOP_K + N_EXPERTS * (EXPERT_BLOCK - 1)) // EXPERT_BLOCK)
    R = n_blocks * EXPERT_BLOCK
    row_tok = jnp.full((R,), T, jnp.int32).at[dest].set(stok)
    row_g = jnp.zeros((R,), jnp.float32).at[dest].set(sg)
    blk_e = jnp.minimum(jnp.searchsorted(pend, jnp.arange(n_blocks, dtype=jnp.int32) * EXPERT_BLOCK,
                                         side='right'), N_EXPERTS - 1)
    h_ext = jnp.concatenate([h, jnp.zeros((1, h.shape[1]), h.dtype)], axis=0)

    def expert_block(args):
        e, tok, g = args
        u = h_ext[tok] @ w1[e] + b1[e]
        glu, lin = jnp.split(u, 2, axis=-1)
        glu = jnp.minimum(glu, SWIGLU_LIMIT)
        lin = jnp.clip(lin, -SWIGLU_LIMIT, SWIGLU_LIMIT)
        act = glu * jax.nn.sigmoid(SWIGLU_ALPHA * glu) * (lin + 1.0)
        return (act @ w2[e] + b2[e]) * g[:, None].astype(h.dtype)

    out = lax.map(expert_block, (blk_e, row_tok.reshape(n_blocks, EXPERT_BLOCK),
                                 row_g.reshape(n_blocks, EXPERT_BLOCK)))
    y = jnp.zeros((T + 1, h.shape[1]), h.dtype).at[row_tok].add(out.reshape(R, h.shape[1]))
    return y[:T]


def encode(x, meta_tokens, ln_in_g, ln_in_b, rel_bias, w_in, conv_w, conv_b, lru_wa, lru_ba,
           lru_wi, lru_bi, lru_lam, attn_sink, w_out, ln1_g, ln1_b, router_w, router_b,
           exp_w1, exp_b1, exp_w2, exp_b2, ln2_g, ln2_b):
    B = x.shape[0]
    h = jnp.concatenate([jnp.broadcast_to(meta_tokens[None], (B, N_META, D_MODEL)).astype(x.dtype), x], axis=1)
    h = layer_norm(h, ln_in_g, ln_in_b)
    L = h.shape[1]
    splits = [Q_COLS, Q_COLS + KV_COLS, Q_COLS + 2 * KV_COLS, Q_COLS + 2 * KV_COLS + LRU_WIDTH]
    for l in range(DEPTH):
        proj = h @ w_in[l]
        q, k, v, xr, yg = jnp.split(proj, splits, axis=-1)
        attn = windowed_attention(q.reshape(B, L, N_HEADS, HEAD_DIM),
                                  k.reshape(B, L, N_KV_HEADS, HEAD_DIM),
                                  v.reshape(B, L, N_KV_HEADS, HEAD_DIM), rel_bias, attn_sink[l])
        rec = rg_lru_branch(xr, yg, conv_w[l], conv_b[l], lru_wa[l], lru_ba[l], lru_wi[l], lru_bi[l], lru_lam[l])
        mix = jnp.concatenate([attn, rec], axis=-1) @ w_out[l]
        h = layer_norm(DN_ALPHA * h + mix, ln1_g[l], ln1_b[l])
        ffn = moe_ffn(h.reshape(B * L, D_MODEL), router_w[l], router_b[l], exp_w1[l], exp_b1[l],
                      exp_w2[l], exp_b2[l]).reshape(B, L, D_MODEL)
        h = layer_norm(DN_ALPHA * h + ffn, ln2_g[l], ln2_b[l])
    return h[:, N_META:]


def setup_inputs(seed: int = 0) -> dict:
    key = jax.random.key(seed)
    ks = jax.random.split(key, 28)
    f32 = jnp.float32
    nrm = lambda k, shape, s: jax.random.normal(k, shape, f32) * s
    a0 = jax.random.uniform(ks[10], (DEPTH, 2, LRU_WIDTH), f32, 0.9, 0.999) ** (1.0 / LRU_C)
    return {
        'x_prompt': nrm(ks[0], (BATCH, SEQ, D_MODEL), 1.0),
        'x_sample': nrm(ks[1], (DEC_BATCH, DEC_SEQ, D_MODEL), 1.0),
        'meta_tokens': nrm(ks[2], (N_META, D_MODEL), 1.0),
        'ln_in_g': 1.0 + nrm(ks[3], (D_MODEL,), 0.02),
        'ln_in_b': nrm(ks[4], (D_MODEL,), 0.02),
        'rel_bias': nrm(ks[5], (N_BUCKETS, N_HEADS), 0.5),
        'w_in': nrm(ks[6], (DEPTH, D_MODEL, IN_COLS), D_MODEL ** -0.5),
        'conv_w': nrm(ks[7], (DEPTH, CONV_WIDTH, LRU_WIDTH), CONV_WIDTH ** -0.5),
        'conv_b': nrm(ks[8], (DEPTH, LRU_WIDTH), 0.02),
        'lru_wa': nrm(ks[9], (DEPTH, 2, LRU_BLOCKS, LRU_BLOCK_W, LRU_BLOCK_W), LRU_BLOCK_W ** -0.5),
        'lru_ba': nrm(ks[11], (DEPTH, 2, LRU_WIDTH), 0.02),
        'lru_wi': nrm(ks[12], (DEPTH, 2, LRU_BLOCKS, LRU_BLOCK_W, LRU_BLOCK_W), LRU_BLOCK_W ** -0.5),
        'lru_bi': nrm(ks[13], (DEPTH, 2, LRU_WIDTH), 0.02),
        'lru_lam': jnp.log(a0) - jnp.log1p(-a0),
        'attn_sink': nrm(ks[14], (DEPTH, N_HEADS), 0.5),
        'w_out': nrm(ks[15], (DEPTH, MIX_WIDTH, D_MODEL), MIX_WIDTH ** -0.5 * DN_BETA),
        'ln1_g': 1.0 + nrm(ks[16], (DEPTH, D_MODEL), 0.02),
        'ln1_b': nrm(ks[17], (DEPTH, D_MODEL), 0.02),
        'router_w': nrm(ks[18], (DEPTH, D_MODEL, N_EXPERTS), D_MODEL ** -0.5),
        'router_b': nrm(ks[19], (DEPTH, N_EXPERTS), 0.01),
        'exp_w1': nrm(ks[20], (DEPTH, N_EXPERTS, D_MODEL, 2 * D_FF), D_MODEL ** -0.5),
        'exp_b1': nrm(ks[21], (DEPTH, N_EXPERTS, 2 * D_FF), 0.02),
        'exp_w2': nrm(ks[22], (DEPTH, N_EXPERTS, D_FF, D_MODEL), D_FF ** -0.5 * DN_BETA),
        'exp_b2': nrm(ks[23], (DEPTH, N_EXPERTS, D_MODEL), 0.02),
        'ln2_g': 1.0 + nrm(ks[24], (DEPTH, D_MODEL), 0.02),
        'ln2_b': nrm(ks[25], (DEPTH, D_MODEL), 0.02),
    }


def reference(x_prompt, x_sample, meta_tokens, ln_in_g, ln_in_b, rel_bias, w_in, conv_w, conv_b,
              lru_wa, lru_ba, lru_wi, lru_bi, lru_lam, attn_sink, w_out, ln1_g, ln1_b,
              router_w, router_b, exp_w1, exp_b1, exp_w2, exp_b2, ln2_g, ln2_b):
    params = (meta_tokens, ln_in_g, ln_in_b, rel_bias, w_in, conv_w, conv_b, lru_wa, lru_ba,
              lru_wi, lru_bi, lru_lam, attn_sink, w_out, ln1_g, ln1_b, router_w, router_b,
              exp_w1, exp_b1, exp_w2, exp_b2, ln2_g, ln2_b)
    y_prompt = encode(x_prompt, *params)
    y_sample = encode(x_sample, *params)
    return (y_prompt, y_sample)
```

```python
import functools
import math

import jax
import jax.numpy as jnp
from jax import lax
from jax.experimental import pallas as pl
from jax.experimental.pallas import tpu as pltpu

D_MODEL = 2048
HEAD_DIM = 128
N_HEADS = 8
N_KV_HEADS = 2
GQA_GROUP = N_HEADS // N_KV_HEADS
Q_COLS = N_HEADS * HEAD_DIM
KV_COLS = N_KV_HEADS * HEAD_DIM
LRU_WIDTH = 1024
LRU_BLOCKS = 8
LRU_BLOCK_W = LRU_WIDTH // LRU_BLOCKS
LRU_C = 8.0
IN_COLS = Q_COLS + 2 * KV_COLS + 2 * LRU_WIDTH
WINDOW = 128
BLOCK = 128
N_BUCKETS = 32
MAX_DISTANCE = 128
N_META = 16
N_EXPERTS = 32
TOP_K = 4
D_FF = D_MODEL
SWIGLU_LIMIT = 7.0
SWIGLU_ALPHA = 1.702
DEPTH = 1
DN_ALPHA = (2.0 * DEPTH) ** 0.25
LN_EPS = 1e-5
NEG_INF = -1e30

SUBLANES = 8
LANES = 128
VMEM_LIMIT_BYTES = 56 * 1024 * 1024

TOKEN_TILE = 512
LRU_CHUNK = 128
MOE_SUB = 256
MOE_SUBS_PER_TILE = 8
MOE_TILE = MOE_SUB * MOE_SUBS_PER_TILE
MOE_FF_CHUNK = 256
COMBINE_TILE = 256

_BF16 = jnp.bfloat16
_F32 = jnp.float32


def _layer_norm(x, g, b):
    mu = jnp.mean(x, axis=-1, keepdims=True)
    xc = x - mu
    var = jnp.mean(xc * xc, axis=-1, keepdims=True)
    return xc * lax.rsqrt(var + LN_EPS) * g + b


def _dot(a, b):
    return jnp.dot(a, b, preferred_element_type=_F32)


def _dot_nt(a, b):
    return lax.dot_general(a, b, (((1,), (1,)), ((), ())), preferred_element_type=_F32)


def _params(*semantics):
    return pltpu.CompilerParams(dimension_semantics=semantics, vmem_limit_bytes=VMEM_LIMIT_BYTES)


def _const_spec(shape):
    nd = len(shape)
    return pl.BlockSpec(shape, lambda *_: (0,) * nd, pipeline_mode=pl.Buffered(1))


def _in_proj_kernel(x_ref, g_ref, b_ref, w_ref, q_ref, k_ref, v_ref, xr_ref, yg_ref):
    h = _layer_norm(x_ref[...], g_ref[...], b_ref[...]).astype(_BF16)
    c0, c1, c2, c3 = Q_COLS, Q_COLS + KV_COLS, Q_COLS + 2 * KV_COLS, Q_COLS + 2 * KV_COLS + LRU_WIDTH
    q_ref[...] = _dot(h, w_ref[:, 0:c0]).astype(_BF16)
    k_ref[...] = _dot(h, w_ref[:, c0:c1]).astype(_BF16)
    v_ref[...] = _dot(h, w_ref[:, c1:c2]).astype(_BF16)
    xr_ref[...] = _dot(h, w_ref[:, c2:c3])
    yg_ref[...] = _dot(h, w_ref[:, c3:IN_COLS])


def _in_proj(x, g, b, w_bf16):
    t = x.shape[0]
    tm = min(TOKEN_TILE, t)
    assert t % tm == 0
    row = lambda n: pl.BlockSpec((tm, n), lambda i: (i, 0))
    return pl.pallas_call(
        _in_proj_kernel,
        grid=(t // tm,),
        in_specs=[row(D_MODEL), _const_spec((1, D_MODEL)), _const_spec((1, D_MODEL)),
                  _const_spec((D_MODEL, IN_COLS))],
        out_specs=[row(Q_COLS), row(KV_COLS), row(KV_COLS), row(LRU_WIDTH), row(LRU_WIDTH)],
        out_shape=[jax.ShapeDtypeStruct((t, Q_COLS), _BF16),
                   jax.ShapeDtypeStruct((t, KV_COLS), _BF16),
                   jax.ShapeDtypeStruct((t, KV_COLS), _BF16),
                   jax.ShapeDtypeStruct((t, LRU_WIDTH), _F32),
                   jax.ShapeDtypeStruct((t, LRU_WIDTH), _F32)],
        compiler_params=_params("arbitrary"),
        name="in_proj",
    )(x, g, b, w_bf16)


def _attn_kernel(q_ref, kp_ref, kc_ref, kn_ref, vp_ref, vc_ref, vn_ref, kt_ref, vt_ref,
                 bb_ref, tb_ref, o_ref, *, nblk):
    i = pl.program_id(2)
    q = q_ref[0]
    kb = jnp.concatenate([kp_ref[0], kc_ref[0], kn_ref[0]], axis=0)
    vb = jnp.concatenate([vp_ref[0], vc_ref[0], vn_ref[0]], axis=0)
    kt = kt_ref[0]
    vt = vt_ref[0]
    qi = lax.broadcasted_iota(jnp.int32, (BLOCK, 3 * BLOCK), 0)
    kj = lax.broadcasted_iota(jnp.int32, (BLOCK, 3 * BLOCK), 1)
    rel = kj - BLOCK - qi
    valid = (jnp.abs(rel) <= WINDOW) & ((kj >= BLOCK) | (i > 0)) & ((kj < 2 * BLOCK) | (i < nblk - 1))
    tail_valid = lax.broadcasted_iota(jnp.int32, (BLOCK, BLOCK), 1) <= N_META
    scale = HEAD_DIM ** -0.5
    for g in range(GQA_GROUP):
        qh = q[:, g * HEAD_DIM:(g + 1) * HEAD_DIM]
        s_b = jnp.where(valid, _dot_nt(qh, kb) * scale + bb_ref[g], NEG_INF)
        s_t = jnp.where(tail_valid, _dot_nt(qh, kt) * scale + tb_ref[0, g], NEG_INF)
        m = jnp.maximum(jnp.max(s_b, axis=-1, keepdims=True), jnp.max(s_t, axis=-1, keepdims=True))
        e_b = jnp.exp(s_b - m)
        e_t = jnp.exp(s_t - m)
        den = jnp.sum(e_b, axis=-1, keepdims=True) + jnp.sum(e_t, axis=-1, keepdims=True)
        p_b = (e_b / den).astype(_BF16)
        p_t = (e_t / den).astype(_BF16)
        o = _dot(p_b, vb) + _dot(p_t, vt)
        o_ref[0, :, g * HEAD_DIM:(g + 1) * HEAD_DIM] = o.astype(_BF16)


def _attention(q, k, v, k_tail, v_tail, band_bias, tail_bias):
    bsz, s = q.shape[0], q.shape[1]
    nblk = s // BLOCK
    gw = GQA_GROUP * HEAD_DIM
    kv_spec = lambda d: pl.BlockSpec(
        (1, BLOCK, HEAD_DIM), lambda b, h, i: (b, jnp.clip(i + d, 0, nblk - 1), h))
    tail_spec = pl.BlockSpec((1, BLOCK, HEAD_DIM), lambda b, h, i: (h, 0, 0))
    return pl.pallas_call(
        functools.partial(_attn_kernel, nblk=nblk),
        grid=(bsz, N_KV_HEADS, nblk),
        in_specs=[pl.BlockSpec((1, BLOCK, gw), lambda b, h, i: (b, i, h)),
                  kv_spec(-1), kv_spec(0), kv_spec(1), kv_spec(-1), kv_spec(0), kv_spec(1),
                  tail_spec, tail_spec,
                  pl.BlockSpec((GQA_GROUP, BLOCK, 3 * BLOCK), lambda b, h, i: (h, 0, 0)),
                  pl.BlockSpec((1, GQA_GROUP, BLOCK, BLOCK), lambda b, h, i: (i, h, 0, 0))],
        out_specs=pl.BlockSpec((1, BLOCK, gw), lambda b, h, i: (b, i, h)),
        out_shape=jax.ShapeDtypeStruct((bsz, s, Q_COLS), _BF16),
        compiler_params=_params("arbitrary", "arbitrary", "arbitrary"),
        name="attention",
    )(q, k, k, k, v, v, v, k_tail, v_tail, band_bias, tail_bias)


_X0 = 24


def _tile_scan(a, u, h_in, reverse):
    n = a.shape[0]
    nt = n // SUBLANES
    a3 = a.reshape(nt, SUBLANES, LANES)
    u3 = u.reshape(nt, SUBLANES, LANES)
    row = lax.broadcasted_iota(jnp.int32, (nt, SUBLANES, LANES), 1)
    for step in (1, 2, 4):
        shift = (SUBLANES - step) if reverse else step
        a_sh = pltpu.roll(a3, shift, 1)
        u_sh = pltpu.roll(u3, shift, 1)
        ok = (row < SUBLANES - step) if reverse else (row >= step)
        u3 = jnp.where(ok, a3 * u_sh + u3, u3)
        a3 = jnp.where(ok, a3 * a_sh, a3)
    out = [None] * nt
    h = h_in
    order = range(nt - 1, -1, -1) if reverse else range(nt)
    edge = 0 if reverse else SUBLANES - 1
    for j in order:
        ht = u3[j] + a3[j] * h
        out[j] = ht
        h = ht[edge:edge + 1]
    return jnp.concatenate(out, axis=0), h


def _lru_kernel(xr_ref, yg_ref, xm_ref, cw_ref, cb_ref, w_ref, b_ref, c_ref, o_ref,
                xext, hf, ab, ub, *, seq):
    nchunk = seq // LRU_CHUNK
    zeros8 = jnp.zeros((SUBLANES, LANES), _F32)
    xext[0:SUBLANES, :] = zeros8
    xext[SUBLANES:_X0, :] = xm_ref[...]
    xext[_X0:_X0 + seq, :] = xr_ref[0]
    xext[_X0 + seq:_X0 + seq + SUBLANES, :] = zeros8
    cw = cw_ref[...]
    cb = cb_ref[...]
    w = w_ref[0]
    bias = b_ref[0]
    c_f = c_ref[0:1, :]
    c_b = c_ref[1:2, :]

    def gates(r0, n):
        win = xext[pl.ds(r0 - SUBLANES, n + 2 * SUBLANES), :]
        span = n + 2 * SUBLANES
        tap = lambda d: pltpu.roll(win, (span - d) % span, 0)[SUBLANES:SUBLANES + n]
        xc = cb + cw[0:1] * tap(-2)
        xc = xc + cw[1:2] * tap(-1)
        xc = xc + cw[2:3] * win[SUBLANES:SUBLANES + n]
        xc = xc + cw[3:4] * tap(1)
        g = _dot(xc.astype(_BF16), w) + bias
        r_f = jax.nn.sigmoid(g[:, 0:LANES])
        r_b = jax.nn.sigmoid(g[:, LANES:2 * LANES])
        i_f = jax.nn.sigmoid(g[:, 2 * LANES:3 * LANES])
        i_b = jax.nn.sigmoid(g[:, 3 * LANES:4 * LANES])
        la_f = c_f * r_f
        la_b = c_b * r_b
        a_f = jnp.exp(la_f)
        a_b = jnp.exp(la_b)
        u_f = jnp.sqrt(1.0 - a_f * a_f) * (i_f * xc)
        u_b = jnp.sqrt(1.0 - a_b * a_b) * (i_b * xc)
        return a_f, u_f, a_b, u_b

    a_f, u_f, _, _ = gates(SUBLANES, N_META)
    _, h0 = _tile_scan(a_f, u_f, jnp.zeros((1, LANES), _F32), reverse=False)

    def fwd(c, h):
        t0 = pl.multiple_of(c * LRU_CHUNK, LRU_CHUNK)
        a_f, u_f, a_b, u_b = gates(_X0 + t0, LRU_CHUNK)
        hs, h = _tile_scan(a_f, u_f, h, reverse=False)
        hf[pl.ds(t0, LRU_CHUNK), :] = hs
        ab[pl.ds(t0, LRU_CHUNK), :] = a_b
        ub[pl.ds(t0, LRU_CHUNK), :] = u_b
        return h

    lax.fori_loop(0, nchunk, fwd, h0)

    def bwd(cc, h):
        c = nchunk - 1 - cc
        t0 = pl.multiple_of(c * LRU_CHUNK, LRU_CHUNK)
        hs, h = _tile_scan(ab[pl.ds(t0, LRU_CHUNK), :], ub[pl.ds(t0, LRU_CHUNK), :], h, reverse=True)
        tot = hf[pl.ds(t0, LRU_CHUNK), :] + hs
        o_ref[0, pl.ds(t0, LRU_CHUNK), :] = (tot * jax.nn.gelu(yg_ref[0, pl.ds(t0, LRU_CHUNK), :])).astype(_BF16)
        return h

    lax.fori_loop(0, nchunk, bwd, jnp.zeros((1, LANES), _F32))


def _rg_lru(xr, yg, xr_meta, conv_w, conv_b, w_cat, b_cat, c_decay):
    bsz, s = xr.shape[0], xr.shape[1]
    assert s % LRU_CHUNK == 0
    slab = pl.BlockSpec((1, s, LANES), lambda b, n: (b, 0, n))
    col = lambda r: pl.BlockSpec((r, LANES), lambda b, n: (0, n))
    return pl.pallas_call(
        functools.partial(_lru_kernel, seq=s),
        grid=(bsz, LRU_BLOCKS),
        in_specs=[slab, slab, col(N_META), col(4), col(1),
                  pl.BlockSpec((1, LANES, 4 * LANES), lambda b, n: (n, 0, 0)),
                  pl.BlockSpec((1, 1, 4 * LANES), lambda b, n: (n, 0, 0)),
                  col(2)],
        out_specs=slab,
        out_shape=jax.ShapeDtypeStruct((bsz, s, LRU_WIDTH), _BF16),
        scratch_shapes=[pltpu.VMEM((s + _X0 + SUBLANES, LANES), _F32),
                        pltpu.VMEM((s, LANES), _F32),
                        pltpu.VMEM((s, LANES), _F32),
                        pltpu.VMEM((s, LANES), _F32)],
        compiler_params=_params("arbitrary", "arbitrary"),
        name="rg_lru",
    )(xr, yg, xr_meta, conv_w, conv_b, w_cat, b_cat, c_decay)


def _mix_kernel(xp_ref, xs_ref, ap_ref, as_ref, rp_ref, rs_ref, g0_ref, b0_ref, wo_ref, g1_ref, b1_ref,
                rwt_ref, rb_ref, h1_ref, te_ref, gt_ref, rk_ref, cnt_ref, carry, *, n_prompt_tiles, tm):
    i = pl.program_id(0)

    @pl.when(i == 0)
    def _():
        carry[...] = jnp.zeros_like(carry)

    first = i < n_prompt_tiles
    x = jnp.where(first, xp_ref[...], xs_ref[...])
    attn = jnp.where(first, ap_ref[...], as_ref[...])
    rec = jnp.where(first, rp_ref[...], rs_ref[...])
    h0 = _layer_norm(x, g0_ref[...], b0_ref[...])
    mix = _dot(attn, wo_ref[0:Q_COLS, :]) + _dot(rec, wo_ref[Q_COLS:Q_COLS + LRU_WIDTH, :])
    h1 = _layer_norm(DN_ALPHA * h0 + mix, g1_ref[...], b1_ref[...])
    h1_ref[...] = h1

    logits = _dot_nt(rwt_ref[...], h1.astype(_BF16)) + rb_ref[:, 0:1]
    eidx = lax.broadcasted_iota(jnp.int32, (N_EXPERTS, tm), 0)
    work = logits
    vals, idxs, hots = [], [], []
    for _ in range(TOP_K):
        m = jnp.max(work, axis=0, keepdims=True)
        idx = jnp.min(jnp.where(work == m, eidx, N_EXPERTS), axis=0, keepdims=True)
        hot = eidx == idx
        vals.append(m)
        idxs.append(idx)
        hots.append(hot)
        work = jnp.where(hot, -jnp.inf, work)
    exps = [jnp.exp(v - vals[0]) for v in vals]
    den = exps[0] + exps[1] + exps[2] + exps[3]
    hot_all = (hots[0] | hots[1] | hots[2] | hots[3])
    hot_f = jnp.where(hot_all, 1.0, 0.0).astype(_F32)
    upper = (lax.broadcasted_iota(jnp.int32, (tm, tm), 0) < lax.broadcasted_iota(jnp.int32, (tm, tm), 1))
    before = _dot(hot_f.astype(_BF16), jnp.where(upper, 1.0, 0.0).astype(_BF16))
    base = carry[:, 0:1] + before
    zero_rows = jnp.zeros((SUBLANES - TOP_K, tm), _F32)
    ranks = [jnp.sum(jnp.where(h, base, 0.0), axis=0, keepdims=True) for h in hots]
    te_ref[...] = jnp.concatenate(idxs + [zero_rows.astype(jnp.int32)], axis=0)
    gt_ref[...] = jnp.concatenate([e / den for e in exps] + [zero_rows], axis=0)
    rk_ref[...] = jnp.concatenate(ranks + [zero_rows], axis=0).astype(jnp.int32)
    carry[...] = carry[...] + jnp.sum(hot_f, axis=1, keepdims=True)
    cnt_ref[...] = carry[...]


def _mix(x_p, x_s, attn_p, attn_s, rec_p, rec_s, g0, b0, w_out_bf16, g1, b1, rw_t, rb):
    tp, ts = x_p.shape[0], x_s.shape[0]
    t = tp + ts
    tm = min(TOKEN_TILE, tp, ts)
    assert tp % tm == 0 and ts % tm == 0
    npt = tp // tm
    first = lambda n: pl.BlockSpec((tm, n), lambda i: (jnp.minimum(i, npt - 1), 0))
    second = lambda n: pl.BlockSpec((tm, n), lambda i: (jnp.maximum(i - npt, 0), 0))
    lane_row = pl.BlockSpec((SUBLANES, tm), lambda i: (0, i))
    return pl.pallas_call(
        functools.partial(_mix_kernel, n_prompt_tiles=npt, tm=tm),
        grid=(t // tm,),
        in_specs=[first(D_MODEL), second(D_MODEL), first(Q_COLS), second(Q_COLS),
                  first(LRU_WIDTH), second(LRU_WIDTH),
                  _const_spec((1, D_MODEL)), _const_spec((1, D_MODEL)),
                  _const_spec((D_MODEL, D_MODEL)),
                  _const_spec((1, D_MODEL)), _const_spec((1, D_MODEL)),
                  _const_spec((N_EXPERTS, D_MODEL)), _const_spec((N_EXPERTS, LANES))],
        out_specs=[pl.BlockSpec((tm, D_MODEL), lambda i: (i, 0)), lane_row, lane_row, lane_row,
                   pl.BlockSpec((N_EXPERTS, LANES), lambda i: (0, 0))],
        out_shape=[jax.ShapeDtypeStruct((t, D_MODEL), _F32),
                   jax.ShapeDtypeStruct((SUBLANES, t), jnp.int32),
                   jax.ShapeDtypeStruct((SUBLANES, t), _F32),
                   jax.ShapeDtypeStruct((SUBLANES, t), jnp.int32),
                   jax.ShapeDtypeStruct((N_EXPERTS, LANES), _F32)],
        scratch_shapes=[pltpu.VMEM((N_EXPERTS, LANES), _F32)],
        compiler_params=_params("arbitrary"),
        name="mix_router",
    )(x_p, x_s, attn_p, attn_s, rec_p, rec_s, g0, b0, w_out_bf16, g1, b1, rw_t, rb)


def _moe_kernel(te_ref, tns_ref, tnv_ref, tok_ref, slot_ref, h1_hbm,
                w1g_ref, w1l_ref, b1g_ref, b1l_ref, w2_ref, b2_ref, yk_hbm,
                land, xb, acc, w1g_b, w1l_b, w2_b, gsem, ssem, *, n_ff_steps):
    i = pl.program_id(0)
    f = pl.program_id(1)
    nsub = tns_ref[i]
    nvalid = tnv_ref[i]

    def gather_copy(tok, buf, j):
        return pltpu.make_async_copy(h1_hbm.at[pl.ds(tok, 1)], land.at[buf, pl.ds(j, 1)], gsem.at[buf])

    def issue_gather(s, buf):
        def body(j, c):
            gather_copy(tok_ref[0, 0, s * MOE_SUB + j], buf, j).start()
            return c
        lax.fori_loop(0, MOE_SUB, body, 0)

    def wait_gather(buf):
        def body(j, c):
            gather_copy(0, buf, j).wait()
            return c
        lax.fori_loop(0, MOE_SUB, body, 0)

    def scatter_copy(j):
        return pltpu.make_async_copy(acc.at[pl.ds(j, 1)], yk_hbm.at[pl.ds(slot_ref[0, 0, j], 1)], ssem.at[0])

    @pl.when(nsub > 0)
    def _active():
        @pl.when(f == 0)
        def _gather():
            issue_gather(0, 0)

            def sub(s, c):
                buf = lax.rem(s, 2)

                @pl.when(s + 1 < nsub)
                def _():
                    issue_gather(s + 1, 1 - buf)

                wait_gather(buf)
                xb[pl.ds(pl.multiple_of(s * MOE_SUB, MOE_SUB), MOE_SUB), :] = land[buf].astype(_BF16)
                return c

            lax.fori_loop(0, nsub, sub, 0)

        w1g_b[...] = w1g_ref[0].astype(_BF16)
        w1l_b[...] = w1l_ref[0].astype(_BF16)
        w2_b[...] = w2_ref[0].astype(_BF16)
        b1g = b1g_ref[0]
        b1l = b1l_ref[0]
        b2 = b2_ref[0]

        def sub(s, c):
            rows = pl.ds(pl.multiple_of(s * MOE_SUB, MOE_SUB), MOE_SUB)
            x = xb[rows, :]
            glu = jnp.minimum(_dot(x, w1g_b[...]) + b1g, SWIGLU_LIMIT)
            lin = jnp.clip(_dot(x, w1l_b[...]) + b1l, -SWIGLU_LIMIT, SWIGLU_LIMIT)
            act = glu * jax.nn.sigmoid(SWIGLU_ALPHA * glu) * (lin + 1.0)
            part = _dot(act.astype(_BF16), w2_b[...])

            @pl.when(f == 0)
            def _():
                acc[rows, :] = part + b2

            @pl.when(f > 0)
            def _():
                acc[rows, :] += part

            return c

        lax.fori_loop(0, nsub, sub, 0)

        @pl.when(f == n_ff_steps - 1)
        def _scatter():
            def start(j, c):
                scatter_copy(j).start()
                return c

            def wait(j, c):
                scatter_copy(j).wait()
                return c

            lax.fori_loop(0, nvalid, start, 0)
            lax.fori_loop(0, nvalid, wait, 0)


def _moe(tile_e, tile_nsub, tile_nvalid, tok_idx, slot_idx, h1, w1, b1, w2, b2):
    t = h1.shape[0]
    n_tiles = tile_e.shape[0]
    nf = D_FF // MOE_FF_CHUNK
    last = nf - 1

    def ff(i, f, tns):
        return jnp.where(tns[i] > 0, f, last)

    idx_spec = pl.BlockSpec((1, 1, MOE_TILE), lambda i, f, te, tns, tnv: (i, 0, 0), memory_space=pltpu.SMEM)
    grid_spec = pltpu.PrefetchScalarGridSpec(
        num_scalar_prefetch=3,
        grid=(n_tiles, nf),
        in_specs=[
            idx_spec, idx_spec,
            pl.BlockSpec(memory_space=pl.ANY),
            pl.BlockSpec((1, D_MODEL, MOE_FF_CHUNK), lambda i, f, te, tns, tnv: (te[i], 0, ff(i, f, tns))),
            pl.BlockSpec((1, D_MODEL, MOE_FF_CHUNK), lambda i, f, te, tns, tnv: (te[i], 0, nf + ff(i, f, tns))),
            pl.BlockSpec((1, 1, MOE_FF_CHUNK), lambda i, f, te, tns, tnv: (te[i], 0, ff(i, f, tns))),
            pl.BlockSpec((1, 1, MOE_FF_CHUNK), lambda i, f, te, tns, tnv: (te[i], 0, nf + ff(i, f, tns))),
            pl.BlockSpec((1, MOE_FF_CHUNK, D_MODEL), lambda i, f, te, tns, tnv: (te[i], ff(i, f, tns), 0)),
            pl.BlockSpec((1, 1, D_MODEL), lambda i, f, te, tns, tnv: (te[i], 0, 0)),
        ],
        out_specs=pl.BlockSpec(memory_space=pl.ANY),
        scratch_shapes=[
            pltpu.VMEM((2, MOE_SUB, D_MODEL), _F32),
            pltpu.VMEM((MOE_TILE, D_MODEL), _BF16),
            pltpu.VMEM((MOE_TILE, D_MODEL), _F32),
            pltpu.VMEM((D_MODEL, MOE_FF_CHUNK), _BF16),
            pltpu.VMEM((D_MODEL, MOE_FF_CHUNK), _BF16),
            pltpu.VMEM((MOE_FF_CHUNK, D_MODEL), _BF16),
            pltpu.SemaphoreType.DMA((2,)),
            pltpu.SemaphoreType.DMA((1,)),
        ],
    )
    return pl.pallas_call(
        functools.partial(_moe_kernel, n_ff_steps=nf),
        grid_spec=grid_spec,
        out_shape=jax.ShapeDtypeStruct((TOP_K * t, D_MODEL), _F32),
        compiler_params=_params("arbitrary", "arbitrary"),
        name="moe_experts",
    )(tile_e, tile_nsub, tile_nvalid, tok_idx, slot_idx, h1, w1, w1, b1, b1, w2, b2)


def _combine_kernel(h1_ref, yk_ref, gt_ref, g_ref, b_ref, o_ref):
    gates = jnp.transpose(gt_ref[...])
    ffn = yk_ref[0] * gates[:, 0:1]
    for k in range(1, TOP_K):
        ffn = ffn + yk_ref[k] * gates[:, k:k + 1]
    o_ref[...] = _layer_norm(DN_ALPHA * h1_ref[...] + ffn, g_ref[...], b_ref[...])


def _combine(h1, yk, gates_t, g2, b2, tok0, ntok):
    t = h1.shape[0]
    tm = COMBINE_TILE
    assert tok0 % tm == 0 and ntok % tm == 0
    off = tok0 // tm
    return pl.pallas_call(
        _combine_kernel,
        grid=(ntok // tm,),
        in_specs=[pl.BlockSpec((tm, D_MODEL), lambda i: (off + i, 0)),
                  pl.BlockSpec((TOP_K, tm, D_MODEL), lambda i: (0, off + i, 0)),
                  pl.BlockSpec((SUBLANES, tm), lambda i: (0, off + i)),
                  _const_spec((1, D_MODEL)), _const_spec((1, D_MODEL))],
        out_specs=pl.BlockSpec((tm, D_MODEL), lambda i: (i, 0)),
        out_shape=jax.ShapeDtypeStruct((ntok, D_MODEL), _F32),
        compiler_params=_params("arbitrary"),
        name="combine_ln2",
    )(h1, yk.reshape(TOP_K, t, D_MODEL), gates_t, g2, b2)


def _t5_bucket(rel):
    half = N_BUCKETS // 2
    exact = half // 2
    n = jnp.abs(rel)
    large = exact + (jnp.log(jnp.maximum(n, 1).astype(_F32) / exact)
                     / math.log(MAX_DISTANCE / exact) * (half - exact)).astype(jnp.int32)
    large = jnp.minimum(large, half - 1)
    return jnp.where(rel > 0, half, 0) + jnp.where(n < exact, n, large)


def _bias_tables(rel_bias, sink, nblk):
    qi = jnp.arange(BLOCK)
    kj = jnp.arange(3 * BLOCK)
    rel_band = kj[None, :] - BLOCK - qi[:, None]
    band = rel_bias[_t5_bucket(rel_band)].transpose(2, 0, 1).astype(_F32)
    front = BLOCK - N_META
    qpos = BLOCK + jnp.arange(nblk * BLOCK)
    rel_meta = (front + jnp.arange(N_META))[None, :] - qpos[:, None]
    meta = rel_bias[_t5_bucket(rel_meta)].astype(_F32)
    meta = meta.reshape(nblk, BLOCK, N_META, N_HEADS).transpose(0, 3, 1, 2)
    sink_col = jnp.broadcast_to(sink.astype(_F32)[None, :, None, None], (nblk, N_HEADS, BLOCK, 1))
    pad = jnp.zeros((nblk, N_HEADS, BLOCK, BLOCK - N_META - 1), _F32)
    return band, jnp.concatenate([meta, sink_col, pad], axis=-1)


def _dispatch_tables(top_e, rank, counts, t, n_tiles):
    c = counts.astype(jnp.int32)
    nb = (c + MOE_SUB - 1) // MOE_SUB
    ns = (nb + MOE_SUBS_PER_TILE - 1) // MOE_SUBS_PER_TILE
    ends = jnp.cumsum(ns)
    first_tile = ends - ns
    q = nb // jnp.maximum(ns, 1)
    rem = nb - q * ns
    big = rem * (q + 1)

    def locate(e, sb):
        qe, re, be = q[e], rem[e], big[e]
        in_big = sb < be
        qs = jnp.maximum(qe, 1)
        j = jnp.where(in_big, sb // (qe + 1), re + (sb - be) // qs)
        lsb = jnp.where(in_big, sb % (qe + 1), (sb - be) % qs)
        return j, lsb

    e = top_e[:TOP_K]
    r = rank[:TOP_K]
    j, lsb = locate(e, r // MOE_SUB)
    pos = (first_tile[e] + j) * MOE_TILE + lsb * MOE_SUB + r % MOE_SUB
    tok = jnp.broadcast_to(jnp.arange(t, dtype=jnp.int32)[None, :], (TOP_K, t))
    slot = tok + jnp.arange(TOP_K, dtype=jnp.int32)[:, None] * t
    flat = pos.reshape(-1)
    zeros = jnp.zeros((n_tiles * MOE_TILE,), jnp.int32)
    tok_idx = zeros.at[flat].set(tok.reshape(-1), unique_indices=True).reshape(n_tiles, 1, MOE_TILE)
    slot_idx = zeros.at[flat].set(slot.reshape(-1), unique_indices=True).reshape(n_tiles, 1, MOE_TILE)

    tid = jnp.arange(n_tiles, dtype=jnp.int32)
    used = tid < ends[-1]
    te = jnp.minimum(jnp.searchsorted(ends, tid, side='right').astype(jnp.int32), N_EXPERTS - 1)
    last_e = te[jnp.maximum(ends[-1] - 1, 0)]
    te = jnp.where(used, te, last_e)
    jt = tid - first_tile[te]
    nsub = jnp.where(used, q[te] + (jt < rem[te]).astype(jnp.int32), 0)
    start_sb = jnp.where(jt < rem[te], jt * (q[te] + 1), big[te] + (jt - rem[te]) * q[te])
    nvalid = jnp.where(used, jnp.clip(c[te] - start_sb * MOE_SUB, 0, nsub * MOE_SUB), 0)
    return te, nsub.astype(jnp.int32), nvalid.astype(jnp.int32), tok_idx, slot_idx


def kernel(x_prompt, x_sample, meta_tokens, ln_in_g, ln_in_b, rel_bias, w_in, conv_w, conv_b, lru_wa, lru_ba,
           lru_wi, lru_bi, lru_lam, attn_sink, w_out, ln1_g, ln1_b, router_w, router_b, exp_w1, exp_b1,
           exp_w2, exp_b2, ln2_g, ln2_b):
    assert DEPTH == 1 and w_in.shape[0] == 1
    row = lambda a: a.reshape(1, -1).astype(_F32)
    g0, b0 = row(ln_in_g), row(ln_in_b)
    w_in_b = w_in[0].astype(_BF16)
    w_out_b = w_out[0].astype(_BF16)

    wa, wi = lru_wa[0], lru_wi[0]
    w_cat = jnp.concatenate([wa[0], wa[1], wi[0], wi[1]], axis=-1).astype(_BF16)
    blk = lambda v: v.reshape(LRU_BLOCKS, 1, LRU_BLOCK_W)
    b_cat = jnp.concatenate([blk(lru_ba[0, 0]), blk(lru_ba[0, 1]), blk(lru_bi[0, 0]), blk(lru_bi[0, 1])],
                            axis=-1).astype(_F32)
    c_decay = -LRU_C * jax.nn.softplus(-lru_lam[0].astype(_F32))

    groups = [x_prompt, x_sample]
    nblk_max = max(x.shape[1] for x in groups) // BLOCK
    band_bias, tail_bias = _bias_tables(rel_bias, attn_sink[0], nblk_max)

    _, k_m, v_m, xr_m, _ = _in_proj(meta_tokens.astype(_F32), g0, b0, w_in_b)
    tail = lambda a: jnp.zeros((N_KV_HEADS, BLOCK, HEAD_DIM), _BF16).at[:, :N_META].set(
        a.reshape(N_META, N_KV_HEADS, HEAD_DIM).transpose(1, 0, 2))
    k_tail, v_tail = tail(k_m), tail(v_m)

    xs, attns, recs = [], [], []
    for x in groups:
        bsz, s = x.shape[0], x.shape[1]
        x2 = x.reshape(bsz * s, D_MODEL)
        q, k, v, xr, yg = _in_proj(x2, g0, b0, w_in_b)
        attn = _attention(q.reshape(bsz, s, Q_COLS), k.reshape(bsz, s, KV_COLS), v.reshape(bsz, s, KV_COLS),
                          k_tail, v_tail, band_bias, tail_bias)
        rec = _rg_lru(xr.reshape(bsz, s, LRU_WIDTH), yg.reshape(bsz, s, LRU_WIDTH), xr_m,
                      conv_w[0].astype(_F32), row(conv_b[0]), w_cat, b_cat, c_decay)
        xs.append(x2)
        attns.append(attn.reshape(bsz * s, Q_COLS))
        recs.append(rec.reshape(bsz * s, LRU_WIDTH))

    rw_t = router_w[0].T.astype(_BF16)
    rb = jnp.broadcast_to(router_b[0].astype(_F32)[:, None], (N_EXPERTS, LANES))
    h1, top_e, gates_t, rank, counts = _mix(xs[0], xs[1], attns[0], attns[1], recs[0], recs[1], g0, b0,
                                            w_out_b, row(ln1_g[0]), row(ln1_b[0]), rw_t, rb)

    t = h1.shape[0]
    n_sub_max = TOP_K * t // MOE_SUB + N_EXPERTS
    n_tiles = -(-n_sub_max // MOE_SUBS_PER_TILE) + N_EXPERTS
    te, nsub, nvalid, tok_idx, slot_idx = _dispatch_tables(top_e, rank, counts[:, 0], t, n_tiles)
    yk = _moe(te, nsub, nvalid, tok_idx, slot_idx, h1,
              exp_w1[0], exp_b1[0].reshape(N_EXPERTS, 1, 2 * D_FF), exp_w2[0],
              exp_b2[0].reshape(N_EXPERTS, 1, D_MODEL))

    outs = []
    tok0 = 0
    for x in groups:
        bsz, s = x.shape[0], x.shape[1]
        y = _combine(h1, yk, gates_t, row(ln2_g[0]), row(ln2_b[0]), tok0, bsz * s)
        outs.append(y.reshape(bsz, s, D_MODEL))
        tok0 += bsz * s
    return tuple(outs)
```

```python
import functools
import math

import jax
import jax.numpy as jnp
from jax import lax
from jax.experimental import pallas as pl
from jax.experimental.pallas import tpu as pltpu

D_MODEL = 2048
HEAD_DIM = 128
N_HEADS = 8
N_KV_HEADS = 2
GQA_GROUP = N_HEADS // N_KV_HEADS
Q_COLS = N_HEADS * HEAD_DIM
KV_COLS = N_KV_HEADS * HEAD_DIM
LRU_WIDTH = 1024
LRU_BLOCKS = 8
LRU_BLOCK_W = LRU_WIDTH // LRU_BLOCKS
LRU_C = 8.0
IN_COLS = Q_COLS + 2 * KV_COLS + 2 * LRU_WIDTH
WINDOW = 128
BLOCK = 128
N_BUCKETS = 32
MAX_DISTANCE = 128
N_META = 16
N_EXPERTS = 32
TOP_K = 4
D_FF = D_MODEL
SWIGLU_LIMIT = 7.0
SWIGLU_ALPHA = 1.702
DEPTH = 1
DN_ALPHA = (2.0 * DEPTH) ** 0.25
LN_EPS = 1e-5
NEG_INF = -1e30

SUBLANES = 8
LANES = 128
VMEM_LIMIT_BYTES = 56 * 1024 * 1024

TOKEN_TILE = 512
LRU_CHUNK = 128
MOE_SUB = 256
MOE_SUBS_PER_TILE = 8
MOE_TILE = MOE_SUB * MOE_SUBS_PER_TILE
MOE_SUBS_PER_CHUNK = 2
MOE_CHUNK = MOE_SUB * MOE_SUBS_PER_CHUNK
MOE_DMA_UNROLL = 8
MOE_FF_CHUNK = 256
COMBINE_TILE = 256

_BF16 = jnp.bfloat16
_F32 = jnp.float32


def _layer_norm(x, g, b):
    mu = jnp.mean(x, axis=-1, keepdims=True)
    xc = x - mu
    var = jnp.mean(xc * xc, axis=-1, keepdims=True)
    return xc * lax.rsqrt(var + LN_EPS) * g + b


def _dot(a, b):
    return jnp.dot(a, b, preferred_element_type=_F32)


def _dot_nt(a, b):
    return lax.dot_general(a, b, (((1,), (1,)), ((), ())), preferred_element_type=_F32)


def _params(*semantics):
    return pltpu.CompilerParams(dimension_semantics=semantics, vmem_limit_bytes=VMEM_LIMIT_BYTES)


def _const_spec(shape):
    nd = len(shape)
    return pl.BlockSpec(shape, lambda *_: (0,) * nd, pipeline_mode=pl.Buffered(1))


def _in_proj_kernel(x_ref, g_ref, b_ref, w_ref, q_ref, k_ref, v_ref, xr_ref, yg_ref):
    h = _layer_norm(x_ref[...], g_ref[...], b_ref[...]).astype(_BF16)
    c0, c1, c2, c3 = Q_COLS, Q_COLS + KV_COLS, Q_COLS + 2 * KV_COLS, Q_COLS + 2 * KV_COLS + LRU_WIDTH
    q_ref[...] = _dot(h, w_ref[:, 0:c0]).astype(_BF16)
    k_ref[...] = _dot(h, w_ref[:, c0:c1]).astype(_BF16)
    v_ref[...] = _dot(h, w_ref[:, c1:c2]).astype(_BF16)
    xr_ref[...] = _dot(h, w_ref[:, c2:c3])
    yg_ref[...] = _dot(h, w_ref[:, c3:IN_COLS])


def _in_proj(x, g, b, w_bf16):
    t = x.shape[0]
    tm = min(TOKEN_TILE, t)
    assert t % tm == 0
    row = lambda n: pl.BlockSpec((tm, n), lambda i: (i, 0))
    return pl.pallas_call(
        _in_proj_kernel,
        grid=(t // tm,),
        in_specs=[row(D_MODEL), _const_spec((1, D_MODEL)), _const_spec((1, D_MODEL)),
                  _const_spec((D_MODEL, IN_COLS))],
        out_specs=[row(Q_COLS), row(KV_COLS), row(KV_COLS), row(LRU_WIDTH), row(LRU_WIDTH)],
        out_shape=[jax.ShapeDtypeStruct((t, Q_COLS), _BF16),
                   jax.ShapeDtypeStruct((t, KV_COLS), _BF16),
                   jax.ShapeDtypeStruct((t, KV_COLS), _BF16),
                   jax.ShapeDtypeStruct((t, LRU_WIDTH), _F32),
                   jax.ShapeDtypeStruct((t, LRU_WIDTH), _F32)],
        compiler_params=_params("arbitrary"),
        name="in_proj",
    )(x, g, b, w_bf16)


def _attn_kernel(q_ref, kp_ref, kc_ref, kn_ref, vp_ref, vc_ref, vn_ref, kt_ref, vt_ref,
                 bb_ref, tb_ref, o_ref, *, nblk):
    i = pl.program_id(2)
    q = q_ref[0]
    kb = jnp.concatenate([kp_ref[0], kc_ref[0], kn_ref[0]], axis=0)
    vb = jnp.concatenate([vp_ref[0], vc_ref[0], vn_ref[0]], axis=0)
    kt = kt_ref[0]
    vt = vt_ref[0]
    qi = lax.broadcasted_iota(jnp.int32, (BLOCK, 3 * BLOCK), 0)
    kj = lax.broadcasted_iota(jnp.int32, (BLOCK, 3 * BLOCK), 1)
    rel = kj - BLOCK - qi
    valid = (jnp.abs(rel) <= WINDOW) & ((kj >= BLOCK) | (i > 0)) & ((kj < 2 * BLOCK) | (i < nblk - 1))
    tail_valid = lax.broadcasted_iota(jnp.int32, (BLOCK, BLOCK), 1) <= N_META
    scale = HEAD_DIM ** -0.5
    for g in range(GQA_GROUP):
        qh = q[:, g * HEAD_DIM:(g + 1) * HEAD_DIM]
        s_b = jnp.where(valid, _dot_nt(qh, kb) * scale + bb_ref[g], NEG_INF)
        s_t = jnp.where(tail_valid, _dot_nt(qh, kt) * scale + tb_ref[0, g], NEG_INF)
        m = jnp.maximum(jnp.max(s_b, axis=-1, keepdims=True), jnp.max(s_t, axis=-1, keepdims=True))
        e_b = jnp.exp(s_b - m)
        e_t = jnp.exp(s_t - m)
        den = jnp.sum(e_b, axis=-1, keepdims=True) + jnp.sum(e_t, axis=-1, keepdims=True)
        p_b = (e_b / den).astype(_BF16)
        p_t = (e_t / den).astype(_BF16)
        o = _dot(p_b, vb) + _dot(p_t, vt)
        o_ref[0, :, g * HEAD_DIM:(g + 1) * HEAD_DIM] = o.astype(_BF16)


def _attention(q, k, v, k_tail, v_tail, band_bias, tail_bias):
    bsz, s = q.shape[0], q.shape[1]
    nblk = s // BLOCK
    gw = GQA_GROUP * HEAD_DIM
    kv_spec = lambda d: pl.BlockSpec(
        (1, BLOCK, HEAD_DIM), lambda b, h, i: (b, jnp.clip(i + d, 0, nblk - 1), h))
    tail_spec = pl.BlockSpec((1, BLOCK, HEAD_DIM), lambda b, h, i: (h, 0, 0))
    return pl.pallas_call(
        functools.partial(_attn_kernel, nblk=nblk),
        grid=(bsz, N_KV_HEADS, nblk),
        in_specs=[pl.BlockSpec((1, BLOCK, gw), lambda b, h, i: (b, i, h)),
                  kv_spec(-1), kv_spec(0), kv_spec(1), kv_spec(-1), kv_spec(0), kv_spec(1),
                  tail_spec, tail_spec,
                  pl.BlockSpec((GQA_GROUP, BLOCK, 3 * BLOCK), lambda b, h, i: (h, 0, 0)),
                  pl.BlockSpec((1, GQA_GROUP, BLOCK, BLOCK), lambda b, h, i: (i, h, 0, 0))],
        out_specs=pl.BlockSpec((1, BLOCK, gw), lambda b, h, i: (b, i, h)),
        out_shape=jax.ShapeDtypeStruct((bsz, s, Q_COLS), _BF16),
        compiler_params=_params("arbitrary", "arbitrary", "arbitrary"),
        name="attention",
    )(q, k, k, k, v, v, v, k_tail, v_tail, band_bias, tail_bias)


_X0 = 24


def _tile_scan(a, u, h_in, reverse):
    n = a.shape[0]
    nt = n // SUBLANES
    a3 = a.reshape(nt, SUBLANES, LANES)
    u3 = u.reshape(nt, SUBLANES, LANES)
    row = lax.broadcasted_iota(jnp.int32, (nt, SUBLANES, LANES), 1)
    for step in (1, 2, 4):
        shift = (SUBLANES - step) if reverse else step
        a_sh = pltpu.roll(a3, shift, 1)
        u_sh = pltpu.roll(u3, shift, 1)
        ok = (row < SUBLANES - step) if reverse else (row >= step)
        u3 = jnp.where(ok, a3 * u_sh + u3, u3)
        a3 = jnp.where(ok, a3 * a_sh, a3)
    out = [None] * nt
    h = h_in
    order = range(nt - 1, -1, -1) if reverse else range(nt)
    edge = 0 if reverse else SUBLANES - 1
    for j in order:
        ht = u3[j] + a3[j] * h
        out[j] = ht
        h = ht[edge:edge + 1]
    return jnp.concatenate(out, axis=0), h


def _lru_kernel(xr_ref, yg_ref, xm_ref, cw_ref, cb_ref, w_ref, b_ref, c_ref, o_ref,
                xext, hf, ab, ub, *, seq):
    nchunk = seq // LRU_CHUNK
    zeros8 = jnp.zeros((SUBLANES, LANES), _F32)
    xext[0:SUBLANES, :] = zeros8
    xext[SUBLANES:_X0, :] = xm_ref[...]
    xext[_X0:_X0 + seq, :] = xr_ref[0]
    xext[_X0 + seq:_X0 + seq + SUBLANES, :] = zeros8
    cw = cw_ref[...]
    cb = cb_ref[...]
    w = w_ref[0]
    bias = b_ref[0]
    c_f = c_ref[0:1, :]
    c_b = c_ref[1:2, :]

    def gates(r0, n):
        win = xext[pl.ds(r0 - SUBLANES, n + 2 * SUBLANES), :]
        span = n + 2 * SUBLANES
        tap = lambda d: pltpu.roll(win, (span - d) % span, 0)[SUBLANES:SUBLANES + n]
        xc = cb + cw[0:1] * tap(-2)
        xc = xc + cw[1:2] * tap(-1)
        xc = xc + cw[2:3] * win[SUBLANES:SUBLANES + n]
        xc = xc + cw[3:4] * tap(1)
        g = _dot(xc.astype(_BF16), w) + bias
        r_f = jax.nn.sigmoid(g[:, 0:LANES])
        r_b = jax.nn.sigmoid(g[:, LANES:2 * LANES])
        i_f = jax.nn.sigmoid(g[:, 2 * LANES:3 * LANES])
        i_b = jax.nn.sigmoid(g[:, 3 * LANES:4 * LANES])
        la_f = c_f * r_f
        la_b = c_b * r_b
        a_f = jnp.exp(la_f)
        a_b = jnp.exp(la_b)
        u_f = jnp.sqrt(1.0 - a_f * a_f) * (i_f * xc)
        u_b = jnp.sqrt(1.0 - a_b * a_b) * (i_b * xc)
        return a_f, u_f, a_b, u_b

    a_f, u_f, _, _ = gates(SUBLANES, N_META)
    _, h0 = _tile_scan(a_f, u_f, jnp.zeros((1, LANES), _F32), reverse=False)

    def fwd(c, h):
        t0 = pl.multiple_of(c * LRU_CHUNK, LRU_CHUNK)
        a_f, u_f, a_b, u_b = gates(_X0 + t0, LRU_CHUNK)
        hs, h = _tile_scan(a_f, u_f, h, reverse=False)
        hf[pl.ds(t0, LRU_CHUNK), :] = hs
        ab[pl.ds(t0, LRU_CHUNK), :] = a_b
        ub[pl.ds(t0, LRU_CHUNK), :] = u_b
        return h

    lax.fori_loop(0, nchunk, fwd, h0)

    def bwd(cc, h):
        c = nchunk - 1 - cc
        t0 = pl.multiple_of(c * LRU_CHUNK, LRU_CHUNK)
        hs, h = _tile_scan(ab[pl.ds(t0, LRU_CHUNK), :], ub[pl.ds(t0, LRU_CHUNK), :], h, reverse=True)
        tot = hf[pl.ds(t0, LRU_CHUNK), :] + hs
        o_ref[0, pl.ds(t0, LRU_CHUNK), :] = (tot * jax.nn.gelu(yg_ref[0, pl.ds(t0, LRU_CHUNK), :])).astype(_BF16)
        return h

    lax.fori_loop(0, nchunk, bwd, jnp.zeros((1, LANES), _F32))


def _rg_lru(xr, yg, xr_meta, conv_w, conv_b, w_cat, b_cat, c_decay):
    bsz, s = xr.shape[0], xr.shape[1]
    assert s % LRU_CHUNK == 0
    slab = pl.BlockSpec((1, s, LANES), lambda b, n: (b, 0, n))
    col = lambda r: pl.BlockSpec((r, LANES), lambda b, n: (0, n))
    return pl.pallas_call(
        functools.partial(_lru_kernel, seq=s),
        grid=(bsz, LRU_BLOCKS),
        in_specs=[slab, slab, col(N_META), col(4), col(1),
                  pl.BlockSpec((1, LANES, 4 * LANES), lambda b, n: (n, 0, 0)),
                  pl.BlockSpec((1, 1, 4 * LANES), lambda b, n: (n, 0, 0)),
                  col(2)],
        out_specs=slab,
        out_shape=jax.ShapeDtypeStruct((bsz, s, LRU_WIDTH), _BF16),
        scratch_shapes=[pltpu.VMEM((s + _X0 + SUBLANES, LANES), _F32),
                        pltpu.VMEM((s, LANES), _F32),
                        pltpu.VMEM((s, LANES), _F32),
                        pltpu.VMEM((s, LANES), _F32)],
        compiler_params=_params("arbitrary", "arbitrary"),
        name="rg_lru",
    )(xr, yg, xr_meta, conv_w, conv_b, w_cat, b_cat, c_decay)


def _mix_kernel(xp_ref, xs_ref, ap_ref, as_ref, rp_ref, rs_ref, g0_ref, b0_ref, wo_ref, g1_ref, b1_ref,
                rwt_ref, rb_ref, h1_ref, te_ref, gt_ref, rk_ref, cnt_ref, carry, *, n_prompt_tiles, tm):
    i = pl.program_id(0)

    @pl.when(i == 0)
    def _():
        carry[...] = jnp.zeros_like(carry)

    first = i < n_prompt_tiles
    x = jnp.where(first, xp_ref[...], xs_ref[...])
    attn = jnp.where(first, ap_ref[...], as_ref[...])
    rec = jnp.where(first, rp_ref[...], rs_ref[...])
    h0 = _layer_norm(x, g0_ref[...], b0_ref[...])
    mix = _dot(attn, wo_ref[0:Q_COLS, :]) + _dot(rec, wo_ref[Q_COLS:Q_COLS + LRU_WIDTH, :])
    h1 = _layer_norm(DN_ALPHA * h0 + mix, g1_ref[...], b1_ref[...])
    h1_ref[...] = h1

    logits = _dot_nt(rwt_ref[...], h1.astype(_BF16)) + rb_ref[:, 0:1]
    eidx = lax.broadcasted_iota(jnp.int32, (N_EXPERTS, tm), 0)
    work = logits
    vals, idxs, hots = [], [], []
    for _ in range(TOP_K):
        m = jnp.max(work, axis=0, keepdims=True)
        idx = jnp.min(jnp.where(work == m, eidx, N_EXPERTS), axis=0, keepdims=True)
        hot = eidx == idx
        vals.append(m)
        idxs.append(idx)
        hots.append(hot)
        work = jnp.where(hot, -jnp.inf, work)
    exps = [jnp.exp(v - vals[0]) for v in vals]
    den = exps[0] + exps[1] + exps[2] + exps[3]
    hot_all = (hots[0] | hots[1] | hots[2] | hots[3])
    hot_f = jnp.where(hot_all, 1.0, 0.0).astype(_F32)
    upper = (lax.broadcasted_iota(jnp.int32, (tm, tm), 0) < lax.broadcasted_iota(jnp.int32, (tm, tm), 1))
    before = _dot(hot_f.astype(_BF16), jnp.where(upper, 1.0, 0.0).astype(_BF16))
    base = carry[:, 0:1] + before
    zero_rows = jnp.zeros((SUBLANES - TOP_K, tm), _F32)
    ranks = [jnp.sum(jnp.where(h, base, 0.0), axis=0, keepdims=True) for h in hots]
    te_ref[...] = jnp.concatenate(idxs + [zero_rows.astype(jnp.int32)], axis=0)
    gt_ref[...] = jnp.concatenate([e / den for e in exps] + [zero_rows], axis=0)
    rk_ref[...] = jnp.concatenate(ranks + [zero_rows], axis=0).astype(jnp.int32)
    carry[...] = carry[...] + jnp.sum(hot_f, axis=1, keepdims=True)
    cnt_ref[...] = carry[...]


def _mix(x_p, x_s, attn_p, attn_s, rec_p, rec_s, g0, b0, w_out_bf16, g1, b1, rw_t, rb):
    tp, ts = x_p.shape[0], x_s.shape[0]
    t = tp + ts
    tm = min(TOKEN_TILE, tp, ts)
    assert tp % tm == 0 and ts % tm == 0
    npt = tp // tm
    first = lambda n: pl.BlockSpec((tm, n), lambda i: (jnp.minimum(i, npt - 1), 0))
    second = lambda n: pl.BlockSpec((tm, n), lambda i: (jnp.maximum(i - npt, 0), 0))
    lane_row = pl.BlockSpec((SUBLANES, tm), lambda i: (0, i))
    return pl.pallas_call(
        functools.partial(_mix_kernel, n_prompt_tiles=npt, tm=tm),
        grid=(t // tm,),
        in_specs=[first(D_MODEL), second(D_MODEL), first(Q_COLS), second(Q_COLS),
                  first(LRU_WIDTH), second(LRU_WIDTH),
                  _const_spec((1, D_MODEL)), _const_spec((1, D_MODEL)),
                  _const_spec((D_MODEL, D_MODEL)),
                  _const_spec((1, D_MODEL)), _const_spec((1, D_MODEL)),
                  _const_spec((N_EXPERTS, D_MODEL)), _const_spec((N_EXPERTS, LANES))],
        out_specs=[pl.BlockSpec((tm, D_MODEL), lambda i: (i, 0)), lane_row, lane_row, lane_row,
                   pl.BlockSpec((N_EXPERTS, LANES), lambda i: (0, 0))],
        out_shape=[jax.ShapeDtypeStruct((t, D_MODEL), _F32),
                   jax.ShapeDtypeStruct((SUBLANES, t), jnp.int32),
                   jax.ShapeDtypeStruct((SUBLANES, t), _F32),
                   jax.ShapeDtypeStruct((SUBLANES, t), jnp.int32),
                   jax.ShapeDtypeStruct((N_EXPERTS, LANES), _F32)],
        scratch_shapes=[pltpu.VMEM((N_EXPERTS, LANES), _F32)],
        compiler_params=_params("arbitrary"),
        name="mix_router",
    )(x_p, x_s, attn_p, attn_s, rec_p, rec_s, g0, b0, w_out_bf16, g1, b1, rw_t, rb)


def _moe_kernel(te_ref, tns_ref, slot_ref, h1_hbm,
                w1g_ref, w1l_ref, b1g_ref, b1l_ref, w2_ref, b2_ref, yk_hbm,
                land, xb, acc, w1g_b, w1l_b, w2_b, gsem, ssem, *, n_ff_steps, n_tokens):
    i = pl.program_id(0)
    f = pl.program_id(1)
    nsub = tns_ref[i]

    def issue_gather(s, buf):
        def body(j, c):
            tok = jnp.minimum(lax.shift_right_logical(slot_ref[0, 0, s * MOE_SUB + j], 2), n_tokens - 1)
            pltpu.make_async_copy(h1_hbm.at[pl.ds(tok, 1)], land.at[buf, pl.ds(j, 1)], gsem.at[buf]).start()
            return c
        lax.fori_loop(0, MOE_SUB, body, 0, unroll=MOE_DMA_UNROLL)

    def wait_gather(buf):
        pltpu.make_async_copy(h1_hbm.at[pl.ds(0, MOE_SUB)], land.at[buf], gsem.at[buf]).wait()

    def issue_scatter(r0, n):
        def body(j, c):
            r = r0 + j
            pltpu.make_async_copy(acc.at[pl.ds(r, 1)], yk_hbm.at[pl.ds(slot_ref[0, 0, r], 1)], ssem.at[0]).start()
            return c
        lax.fori_loop(0, n, body, 0, unroll=MOE_DMA_UNROLL)

    def wait_scatter():
        pltpu.make_async_copy(acc.at[pl.ds(0, MOE_SUB)], yk_hbm.at[pl.ds(0, MOE_SUB)], ssem.at[0]).wait()

    @pl.when(nsub > 0)
    def _active():
        @pl.when(f == 0)
        def _gather():
            issue_gather(0, 0)

            def sub(s, c):
                buf = lax.rem(s, 2)

                @pl.when(s + 1 < nsub)
                def _():
                    issue_gather(s + 1, 1 - buf)

                wait_gather(buf)
                xb[pl.ds(pl.multiple_of(s * MOE_SUB, MOE_SUB), MOE_SUB), :] = land[buf].astype(_BF16)
                return c

            lax.fori_loop(0, nsub, sub, 0)

        w1g_b[...] = w1g_ref[0].astype(_BF16)
        w1l_b[...] = w1l_ref[0].astype(_BF16)
        w2_b[...] = w2_ref[0].astype(_BF16)
        b1g = b1g_ref[0]
        b1l = b1l_ref[0]
        b2 = b2_ref[0]

        def compute(r0, n):
            rows = pl.ds(r0, n)
            x = xb[rows, :]
            glu = jnp.minimum(_dot(x, w1g_b[...]) + b1g, SWIGLU_LIMIT)
            lin = jnp.clip(_dot(x, w1l_b[...]) + b1l, -SWIGLU_LIMIT, SWIGLU_LIMIT)
            act = glu * jax.nn.sigmoid(SWIGLU_ALPHA * glu) * (lin + 1.0)
            part = _dot(act.astype(_BF16), w2_b[...])

            @pl.when(f == 0)
            def _():
                acc[rows, :] = part + b2

            @pl.when(f > 0)
            def _():
                acc[rows, :] += part

            @pl.when(f == n_ff_steps - 1)
            def _():
                issue_scatter(r0, n)

        n_big = nsub // MOE_SUBS_PER_CHUNK

        def big(s, c):
            compute(pl.multiple_of(s * MOE_CHUNK, MOE_CHUNK), MOE_CHUNK)
            return c

        lax.fori_loop(0, n_big, big, 0)

        def small(s, c):
            compute(pl.multiple_of(s * MOE_SUB, MOE_SUB), MOE_SUB)
            return c

        lax.fori_loop(n_big * MOE_SUBS_PER_CHUNK, nsub, small, 0)

        @pl.when(f == n_ff_steps - 1)
        def _drain():
            def body(s, c):
                wait_scatter()
                return c
            lax.fori_loop(0, nsub, body, 0)


def _moe(tile_e, tile_nsub, slot_idx, h1, w1, b1, w2, b2):
    t = h1.shape[0]
    n_tiles = tile_e.shape[0]
    nf = D_FF // MOE_FF_CHUNK
    last = nf - 1

    def ff(i, f, tns):
        return jnp.where(tns[i] > 0, f, last)

    grid_spec = pltpu.PrefetchScalarGridSpec(
        num_scalar_prefetch=2,
        grid=(n_tiles, nf),
        in_specs=[
            pl.BlockSpec((1, 1, MOE_TILE), lambda i, f, te, tns: (i, 0, 0), memory_space=pltpu.SMEM),
            pl.BlockSpec(memory_space=pl.ANY),
            pl.BlockSpec((1, D_MODEL, MOE_FF_CHUNK), lambda i, f, te, tns: (te[i], 0, ff(i, f, tns))),
            pl.BlockSpec((1, D_MODEL, MOE_FF_CHUNK), lambda i, f, te, tns: (te[i], 0, nf + ff(i, f, tns))),
            pl.BlockSpec((1, 1, MOE_FF_CHUNK), lambda i, f, te, tns: (te[i], 0, ff(i, f, tns))),
            pl.BlockSpec((1, 1, MOE_FF_CHUNK), lambda i, f, te, tns: (te[i], 0, nf + ff(i, f, tns))),
            pl.BlockSpec((1, MOE_FF_CHUNK, D_MODEL), lambda i, f, te, tns: (te[i], ff(i, f, tns), 0)),
            pl.BlockSpec((1, 1, D_MODEL), lambda i, f, te, tns: (te[i], 0, 0)),
        ],
        out_specs=pl.BlockSpec(memory_space=pl.ANY),
        scratch_shapes=[
            pltpu.VMEM((2, MOE_SUB, D_MODEL), _F32),
            pltpu.VMEM((MOE_TILE, D_MODEL), _BF16),
            pltpu.VMEM((MOE_TILE, D_MODEL), _F32),
            pltpu.VMEM((D_MODEL, MOE_FF_CHUNK), _BF16),
            pltpu.VMEM((D_MODEL, MOE_FF_CHUNK), _BF16),
            pltpu.VMEM((MOE_FF_CHUNK, D_MODEL), _BF16),
            pltpu.SemaphoreType.DMA((2,)),
            pltpu.SemaphoreType.DMA((1,)),
        ],
    )
    return pl.pallas_call(
        functools.partial(_moe_kernel, n_ff_steps=nf, n_tokens=t),
        grid_spec=grid_spec,
        out_shape=jax.ShapeDtypeStruct((TOP_K * t + MOE_TILE, D_MODEL), _F32),
        compiler_params=_params("arbitrary", "arbitrary"),
        name="moe_experts",
    )(tile_e, tile_nsub, slot_idx, h1, w1, w1, b1, b1, w2, b2)


def _combine_kernel(h1_ref, yk_ref, gt_ref, g_ref, b_ref, o_ref):
    gates = jnp.transpose(gt_ref[...])
    ffn = yk_ref[:, 0:D_MODEL] * gates[:, 0:1]
    for k in range(1, TOP_K):
        ffn = ffn + yk_ref[:, k * D_MODEL:(k + 1) * D_MODEL] * gates[:, k:k + 1]
    o_ref[...] = _layer_norm(DN_ALPHA * h1_ref[...] + ffn, g_ref[...], b_ref[...])


def _combine(h1, yk, gates_t, g2, b2, tok0, ntok):
    tm = COMBINE_TILE
    assert tok0 % tm == 0 and ntok % tm == 0
    off = tok0 // tm
    return pl.pallas_call(
        _combine_kernel,
        grid=(ntok // tm,),
        in_specs=[pl.BlockSpec((tm, D_MODEL), lambda i: (off + i, 0)),
                  pl.BlockSpec((tm, TOP_K * D_MODEL), lambda i: (off + i, 0)),
                  pl.BlockSpec((SUBLANES, tm), lambda i: (0, off + i)),
                  _const_spec((1, D_MODEL)), _const_spec((1, D_MODEL))],
        out_specs=pl.BlockSpec((tm, D_MODEL), lambda i: (i, 0)),
        out_shape=jax.ShapeDtypeStruct((ntok, D_MODEL), _F32),
        compiler_params=_params("arbitrary"),
        name="combine_ln2",
    )(h1, yk.reshape(-1, TOP_K * D_MODEL), gates_t, g2, b2)


def _t5_bucket(rel):
    half = N_BUCKETS // 2
    exact = half // 2
    n = jnp.abs(rel)
    large = exact + (jnp.log(jnp.maximum(n, 1).astype(_F32) / exact)
                     / math.log(MAX_DISTANCE / exact) * (half - exact)).astype(jnp.int32)
    large = jnp.minimum(large, half - 1)
    return jnp.where(rel > 0, half, 0) + jnp.where(n < exact, n, large)


def _lookup(table, idx):
    out = jnp.zeros(idx.shape + table.shape[1:], table.dtype)
    expand = (...,) + (None,) * (table.ndim - 1)
    for j in range(table.shape[0]):
        out = out + jnp.where((idx == j)[expand], table[j], jnp.zeros((), table.dtype))
    return out


def _bias_tables(rel_bias, sink, nblk):
    rb = rel_bias.astype(_F32)
    qi = jnp.arange(BLOCK)
    kj = jnp.arange(3 * BLOCK)
    rel_band = kj[None, :] - BLOCK - qi[:, None]
    band = _lookup(rb, _t5_bucket(rel_band)).transpose(2, 0, 1)
    dist = jnp.arange(1, nblk * BLOCK + N_META + 1)
    by_dist = _lookup(rb, _t5_bucket(-dist))
    s = nblk * BLOCK
    meta = jnp.stack([by_dist[N_META - m - 1:N_META - m - 1 + s] for m in range(N_META)], axis=1)
    meta = meta.reshape(nblk, BLOCK, N_META, N_HEADS).transpose(0, 3, 1, 2)
    sink_col = jnp.broadcast_to(sink.astype(_F32)[None, :, None, None], (nblk, N_HEADS, BLOCK, 1))
    pad = jnp.zeros((nblk, N_HEADS, BLOCK, BLOCK - N_META - 1), _F32)
    return band, jnp.concatenate([meta, sink_col, pad], axis=-1)


def _dispatch_tables(top_e, rank, counts, t, n_tiles):
    c = counts.astype(jnp.int32)
    nb = (c + MOE_SUB - 1) // MOE_SUB
    ns = (nb + MOE_SUBS_PER_TILE - 1) // MOE_SUBS_PER_TILE
    ends = jnp.cumsum(ns)
    first_tile = ends - ns
    q = nb // jnp.maximum(ns, 1)
    rem = nb - q * ns
    big = rem * (q + 1)

    e = top_e[:TOP_K]
    r = rank[:TOP_K]
    per_e = _lookup(jnp.stack([q, rem, big, first_tile], axis=1), e)
    qe, re, be, fe = per_e[..., 0], per_e[..., 1], per_e[..., 2], per_e[..., 3]
    sb = r // MOE_SUB
    in_big = sb < be
    qs = jnp.maximum(qe, 1)
    j = jnp.where(in_big, sb // (qe + 1), re + (sb - be) // qs)
    lsb = jnp.where(in_big, sb % (qe + 1), (sb - be) % qs)
    pos = (fe + j) * MOE_TILE + lsb * MOE_SUB + r % MOE_SUB
    slot = jnp.arange(t, dtype=jnp.int32)[None, :] * TOP_K + jnp.arange(TOP_K, dtype=jnp.int32)[:, None]
    trash = TOP_K * t + jnp.arange(n_tiles * MOE_TILE, dtype=jnp.int32) % MOE_TILE
    slot_idx = trash.at[pos.reshape(-1)].set(slot.reshape(-1), unique_indices=True).reshape(n_tiles, 1, MOE_TILE)

    tid = jnp.arange(n_tiles, dtype=jnp.int32)
    used = tid < ends[-1]
    te = jnp.minimum(jnp.searchsorted(ends, tid, side='right').astype(jnp.int32), N_EXPERTS - 1)
    last_e = te[jnp.maximum(ends[-1] - 1, 0)]
    te = jnp.where(used, te, last_e)
    jt = tid - first_tile[te]
    nsub = jnp.where(used, q[te] + (jt < rem[te]).astype(jnp.int32), 0)
    return te, nsub.astype(jnp.int32), slot_idx


def kernel(x_prompt, x_sample, meta_tokens, ln_in_g, ln_in_b, rel_bias, w_in, conv_w, conv_b, lru_wa, lru_ba,
           lru_wi, lru_bi, lru_lam, attn_sink, w_out, ln1_g, ln1_b, router_w, router_b, exp_w1, exp_b1,
           exp_w2, exp_b2, ln2_g, ln2_b):
    assert DEPTH == 1 and w_in.shape[0] == 1
    row = lambda a: a.reshape(1, -1).astype(_F32)
    g0, b0 = row(ln_in_g), row(ln_in_b)
    w_in_b = w_in[0].astype(_BF16)
    w_out_b = w_out[0].astype(_BF16)

    wa, wi = lru_wa[0], lru_wi[0]
    w_cat = jnp.concatenate([wa[0], wa[1], wi[0], wi[1]], axis=-1).astype(_BF16)
    blk = lambda v: v.reshape(LRU_BLOCKS, 1, LRU_BLOCK_W)
    b_cat = jnp.concatenate([blk(lru_ba[0, 0]), blk(lru_ba[0, 1]), blk(lru_bi[0, 0]), blk(lru_bi[0, 1])],
                            axis=-1).astype(_F32)
    c_decay = -LRU_C * jax.nn.softplus(-lru_lam[0].astype(_F32))

    groups = [x_prompt, x_sample]
    nblk_max = max(x.shape[1] for x in groups) // BLOCK
    band_bias, tail_bias = _bias_tables(rel_bias, attn_sink[0], nblk_max)

    _, k_m, v_m, xr_m, _ = _in_proj(meta_tokens.astype(_F32), g0, b0, w_in_b)
    tail = lambda a: jnp.zeros((N_KV_HEADS, BLOCK, HEAD_DIM), _BF16).at[:, :N_META].set(
        a.reshape(N_META, N_KV_HEADS, HEAD_DIM).transpose(1, 0, 2))
    k_tail, v_tail = tail(k_m), tail(v_m)

    xs, attns, recs = [], [], []
    for x in groups:
        bsz, s = x.shape[0], x.shape[1]
        x2 = x.reshape(bsz * s, D_MODEL)
        q, k, v, xr, yg = _in_proj(x2, g0, b0, w_in_b)
        attn = _attention(q.reshape(bsz, s, Q_COLS), k.reshape(bsz, s, KV_COLS), v.reshape(bsz, s, KV_COLS),
                          k_tail, v_tail, band_bias, tail_bias)
        rec = _rg_lru(xr.reshape(bsz, s, LRU_WIDTH), yg.reshape(bsz, s, LRU_WIDTH), xr_m,
                      conv_w[0].astype(_F32), row(conv_b[0]), w_cat, b_cat, c_decay)
        xs.append(x2)
        attns.append(attn.reshape(bsz * s, Q_COLS))
        recs.append(rec.reshape(bsz * s, LRU_WIDTH))

    rw_t = router_w[0].T.astype(_BF16)
    rb = jnp.broadcast_to(router_b[0].astype(_F32)[:, None], (N_EXPERTS, LANES))
    h1, top_e, gates_t, rank, counts = _mix(xs[0], xs[1], attns[0], attns[1], recs[0], recs[1], g0, b0,
                                            w_out_b, row(ln1_g[0]), row(ln1_b[0]), rw_t, rb)

    t = h1.shape[0]
    n_sub_max = TOP_K * t // MOE_SUB + N_EXPERTS
    n_tiles = -(-n_sub_max // MOE_SUBS_PER_TILE) + N_EXPERTS
    te, nsub, slot_idx = _dispatch_tables(top_e, rank, counts[:, 0], t, n_tiles)
    yk = _moe(te, nsub, slot_idx, h1,
              exp_w1[0], exp_b1[0].reshape(N_EXPERTS, 1, 2 * D_FF), exp_w2[0],
              exp_b2[0].reshape(N_EXPERTS, 1, D_MODEL))

    outs = []
    tok0 = 0
    for x in groups:
        bsz, s = x.shape[0], x.shape[1]
        y = _combine(h1, yk, gates_t, row(ln2_g[0]), row(ln2_b[0]), tok0, bsz * s)
        outs.append(y.reshape(bsz, s, D_MODEL))
        tok0 += bsz * s
    return tuple(outs)
```

```python
import functools
import math

import jax
import jax.numpy as jnp
from jax import lax
from jax.experimental import pallas as pl
from jax.experimental.pallas import tpu as pltpu

D_MODEL = 2048
HEAD_DIM = 128
N_HEADS = 8
N_KV_HEADS = 2
GQA_GROUP = N_HEADS // N_KV_HEADS
Q_COLS = N_HEADS * HEAD_DIM
KV_COLS = N_KV_HEADS * HEAD_DIM
LRU_WIDTH = 1024
LRU_BLOCKS = 8
LRU_BLOCK_W = LRU_WIDTH // LRU_BLOCKS
LRU_C = 8.0
IN_COLS = Q_COLS + 2 * KV_COLS + 2 * LRU_WIDTH
WINDOW = 128
BLOCK = 128
N_BUCKETS = 32
MAX_DISTANCE = 128
N_META = 16
N_EXPERTS = 32
TOP_K = 4
D_FF = D_MODEL
SWIGLU_LIMIT = 7.0
SWIGLU_ALPHA = 1.702
DEPTH = 1
DN_ALPHA = (2.0 * DEPTH) ** 0.25
LN_EPS = 1e-5
NEG_INF = -1e30

SUBLANES = 8
LANES = 128
VMEM_LIMIT_BYTES = 56 * 1024 * 1024

TOKEN_TILE = 512
LRU_CHUNK = 128
MOE_SUB = 256
MOE_SUBS_PER_TILE = 8
MOE_TILE = MOE_SUB * MOE_SUBS_PER_TILE
MOE_SLOT_STRIDE = 1 << 16
MOE_FF_CHUNK = 256
COMBINE_TILE = 256

_BF16 = jnp.bfloat16
_F32 = jnp.float32


def _layer_norm(x, g, b):
    mu = jnp.mean(x, axis=-1, keepdims=True)
    xc = x - mu
    var = jnp.mean(xc * xc, axis=-1, keepdims=True)
    return xc * lax.rsqrt(var + LN_EPS) * g + b


def _dot(a, b):
    return jnp.dot(a, b, preferred_element_type=_F32)


def _dot_nt(a, b):
    return lax.dot_general(a, b, (((1,), (1,)), ((), ())), preferred_element_type=_F32)


def _params(*semantics):
    return pltpu.CompilerParams(dimension_semantics=semantics, vmem_limit_bytes=VMEM_LIMIT_BYTES)


def _const_spec(shape):
    nd = len(shape)
    return pl.BlockSpec(shape, lambda *_: (0,) * nd, pipeline_mode=pl.Buffered(1))


def _in_proj_kernel(x_ref, g_ref, b_ref, w_ref, q_ref, k_ref, v_ref, xr_ref, yg_ref):
    h = _layer_norm(x_ref[...], g_ref[...], b_ref[...]).astype(_BF16)
    c0, c1, c2, c3 = Q_COLS, Q_COLS + KV_COLS, Q_COLS + 2 * KV_COLS, Q_COLS + 2 * KV_COLS + LRU_WIDTH
    q_ref[...] = _dot(h, w_ref[:, 0:c0]).astype(_BF16)
    k_ref[...] = _dot(h, w_ref[:, c0:c1]).astype(_BF16)
    v_ref[...] = _dot(h, w_ref[:, c1:c2]).astype(_BF16)
    xr_ref[...] = _dot(h, w_ref[:, c2:c3])
    yg_ref[...] = _dot(h, w_ref[:, c3:IN_COLS])


def _in_proj(x, g, b, w_bf16):
    t = x.shape[0]
    tm = min(TOKEN_TILE, t)
    assert t % tm == 0
    row = lambda n: pl.BlockSpec((tm, n), lambda i: (i, 0))
    return pl.pallas_call(
        _in_proj_kernel,
        grid=(t // tm,),
        in_specs=[row(D_MODEL), _const_spec((1, D_MODEL)), _const_spec((1, D_MODEL)),
                  _const_spec((D_MODEL, IN_COLS))],
        out_specs=[row(Q_COLS), row(KV_COLS), row(KV_COLS), row(LRU_WIDTH), row(LRU_WIDTH)],
        out_shape=[jax.ShapeDtypeStruct((t, Q_COLS), _BF16),
                   jax.ShapeDtypeStruct((t, KV_COLS), _BF16),
                   jax.ShapeDtypeStruct((t, KV_COLS), _BF16),
                   jax.ShapeDtypeStruct((t, LRU_WIDTH), _F32),
                   jax.ShapeDtypeStruct((t, LRU_WIDTH), _F32)],
        compiler_params=_params("arbitrary"),
        name="in_proj",
    )(x, g, b, w_bf16)


def _attn_kernel(q_ref, kp_ref, kc_ref, kn_ref, vp_ref, vc_ref, vn_ref, kt_ref, vt_ref,
                 bb_ref, tb_ref, o_ref, *, nblk):
    i = pl.program_id(2)
    q = q_ref[0]
    kb = jnp.concatenate([kp_ref[0], kc_ref[0], kn_ref[0]], axis=0)
    vb = jnp.concatenate([vp_ref[0], vc_ref[0], vn_ref[0]], axis=0)
    kt = kt_ref[0]
    vt = vt_ref[0]
    qi = lax.broadcasted_iota(jnp.int32, (BLOCK, 3 * BLOCK), 0)
    kj = lax.broadcasted_iota(jnp.int32, (BLOCK, 3 * BLOCK), 1)
    rel = kj - BLOCK - qi
    valid = (jnp.abs(rel) <= WINDOW) & ((kj >= BLOCK) | (i > 0)) & ((kj < 2 * BLOCK) | (i < nblk - 1))
    tail_valid = lax.broadcasted_iota(jnp.int32, (BLOCK, BLOCK), 1) <= N_META
    scale = HEAD_DIM ** -0.5
    for g in range(GQA_GROUP):
        qh = q[:, g * HEAD_DIM:(g + 1) * HEAD_DIM]
        s_b = jnp.where(valid, _dot_nt(qh, kb) * scale + bb_ref[g], NEG_INF)
        s_t = jnp.where(tail_valid, _dot_nt(qh, kt) * scale + tb_ref[0, g], NEG_INF)
        m = jnp.maximum(jnp.max(s_b, axis=-1, keepdims=True), jnp.max(s_t, axis=-1, keepdims=True))
        e_b = jnp.exp(s_b - m)
        e_t = jnp.exp(s_t - m)
        den = jnp.sum(e_b, axis=-1, keepdims=True) + jnp.sum(e_t, axis=-1, keepdims=True)
        p_b = (e_b / den).astype(_BF16)
        p_t = (e_t / den).astype(_BF16)
        o = _dot(p_b, vb) + _dot(p_t, vt)
        o_ref[0, :, g * HEAD_DIM:(g + 1) * HEAD_DIM] = o.astype(_BF16)


def _attention(q, k, v, k_tail, v_tail, band_bias, tail_bias):
    bsz, s = q.shape[0], q.shape[1]
    nblk = s // BLOCK
    gw = GQA_GROUP * HEAD_DIM
    kv_spec = lambda d: pl.BlockSpec(
        (1, BLOCK, HEAD_DIM), lambda b, h, i: (b, jnp.clip(i + d, 0, nblk - 1), h))
    tail_spec = pl.BlockSpec((1, BLOCK, HEAD_DIM), lambda b, h, i: (h, 0, 0))
    return pl.pallas_call(
        functools.partial(_attn_kernel, nblk=nblk),
        grid=(bsz, N_KV_HEADS, nblk),
        in_specs=[pl.BlockSpec((1, BLOCK, gw), lambda b, h, i: (b, i, h)),
                  kv_spec(-1), kv_spec(0), kv_spec(1), kv_spec(-1), kv_spec(0), kv_spec(1),
                  tail_spec, tail_spec,
                  pl.BlockSpec((GQA_GROUP, BLOCK, 3 * BLOCK), lambda b, h, i: (h, 0, 0)),
                  pl.BlockSpec((1, GQA_GROUP, BLOCK, BLOCK), lambda b, h, i: (i, h, 0, 0))],
        out_specs=pl.BlockSpec((1, BLOCK, gw), lambda b, h, i: (b, i, h)),
        out_shape=jax.ShapeDtypeStruct((bsz, s, Q_COLS), _BF16),
        compiler_params=_params("arbitrary", "arbitrary", "arbitrary"),
        name="attention",
    )(q, k, k, k, v, v, v, k_tail, v_tail, band_bias, tail_bias)


_X0 = 24


def _tile_scan(a, u, h_in, reverse):
    n = a.shape[0]
    nt = n // SUBLANES
    a3 = a.reshape(nt, SUBLANES, LANES)
    u3 = u.reshape(nt, SUBLANES, LANES)
    row = lax.broadcasted_iota(jnp.int32, (nt, SUBLANES, LANES), 1)
    for step in (1, 2, 4):
        shift = (SUBLANES - step) if reverse else step
        a_sh = pltpu.roll(a3, shift, 1)
        u_sh = pltpu.roll(u3, shift, 1)
        ok = (row < SUBLANES - step) if reverse else (row >= step)
        u3 = jnp.where(ok, a3 * u_sh + u3, u3)
        a3 = jnp.where(ok, a3 * a_sh, a3)
    out = [None] * nt
    h = h_in
    order = range(nt - 1, -1, -1) if reverse else range(nt)
    edge = 0 if reverse else SUBLANES - 1
    for j in order:
        ht = u3[j] + a3[j] * h
        out[j] = ht
        h = ht[edge:edge + 1]
    return jnp.concatenate(out, axis=0), h


def _lru_kernel(xr_ref, yg_ref, xm_ref, cw_ref, cb_ref, w_ref, b_ref, c_ref, o_ref,
                xext, hf, ab, ub, *, seq):
    nchunk = seq // LRU_CHUNK
    zeros8 = jnp.zeros((SUBLANES, LANES), _F32)
    xext[0:SUBLANES, :] = zeros8
    xext[SUBLANES:_X0, :] = xm_ref[...]
    xext[_X0:_X0 + seq, :] = xr_ref[0]
    xext[_X0 + seq:_X0 + seq + SUBLANES, :] = zeros8
    cw = cw_ref[...]
    cb = cb_ref[...]
    w = w_ref[0]
    bias = b_ref[0]
    c_f = c_ref[0:1, :]
    c_b = c_ref[1:2, :]

    def gates(r0, n):
        win = xext[pl.ds(r0 - SUBLANES, n + 2 * SUBLANES), :]
        span = n + 2 * SUBLANES
        tap = lambda d: pltpu.roll(win, (span - d) % span, 0)[SUBLANES:SUBLANES + n]
        xc = cb + cw[0:1] * tap(-2)
        xc = xc + cw[1:2] * tap(-1)
        xc = xc + cw[2:3] * win[SUBLANES:SUBLANES + n]
        xc = xc + cw[3:4] * tap(1)
        g = _dot(xc.astype(_BF16), w) + bias
        r_f = jax.nn.sigmoid(g[:, 0:LANES])
        r_b = jax.nn.sigmoid(g[:, LANES:2 * LANES])
        i_f = jax.nn.sigmoid(g[:, 2 * LANES:3 * LANES])
        i_b = jax.nn.sigmoid(g[:, 3 * LANES:4 * LANES])
        la_f = c_f * r_f
        la_b = c_b * r_b
        a_f = jnp.exp(la_f)
        a_b = jnp.exp(la_b)
        u_f = jnp.sqrt(1.0 - a_f * a_f) * (i_f * xc)
        u_b = jnp.sqrt(1.0 - a_b * a_b) * (i_b * xc)
        return a_f, u_f, a_b, u_b

    a_f, u_f, _, _ = gates(SUBLANES, N_META)
    _, h0 = _tile_scan(a_f, u_f, jnp.zeros((1, LANES), _F32), reverse=False)

    def fwd(c, h):
        t0 = pl.multiple_of(c * LRU_CHUNK, LRU_CHUNK)
        a_f, u_f, a_b, u_b = gates(_X0 + t0, LRU_CHUNK)
        hs, h = _tile_scan(a_f, u_f, h, reverse=False)
        hf[pl.ds(t0, LRU_CHUNK), :] = hs
        ab[pl.ds(t0, LRU_CHUNK), :] = a_b
        ub[pl.ds(t0, LRU_CHUNK), :] = u_b
        return h

    lax.fori_loop(0, nchunk, fwd, h0)

    def bwd(cc, h):
        c = nchunk - 1 - cc
        t0 = pl.multiple_of(c * LRU_CHUNK, LRU_CHUNK)
        hs, h = _tile_scan(ab[pl.ds(t0, LRU_CHUNK), :], ub[pl.ds(t0, LRU_CHUNK), :], h, reverse=True)
        tot = hf[pl.ds(t0, LRU_CHUNK), :] + hs
        o_ref[0, pl.ds(t0, LRU_CHUNK), :] = (tot * jax.nn.gelu(yg_ref[0, pl.ds(t0, LRU_CHUNK), :])).astype(_BF16)
        return h

    lax.fori_loop(0, nchunk, bwd, jnp.zeros((1, LANES), _F32))


def _rg_lru(xr, yg, xr_meta, conv_w, conv_b, w_cat, b_cat, c_decay):
    bsz, s = xr.shape[0], xr.shape[1]
    assert s % LRU_CHUNK == 0
    slab = pl.BlockSpec((1, s, LANES), lambda b, n: (b, 0, n))
    col = lambda r: pl.BlockSpec((r, LANES), lambda b, n: (0, n))
    return pl.pallas_call(
        functools.partial(_lru_kernel, seq=s),
        grid=(bsz, LRU_BLOCKS),
        in_specs=[slab, slab, col(N_META), col(4), col(1),
                  pl.BlockSpec((1, LANES, 4 * LANES), lambda b, n: (n, 0, 0)),
                  pl.BlockSpec((1, 1, 4 * LANES), lambda b, n: (n, 0, 0)),
                  col(2)],
        out_specs=slab,
        out_shape=jax.ShapeDtypeStruct((bsz, s, LRU_WIDTH), _BF16),
        scratch_shapes=[pltpu.VMEM((s + _X0 + SUBLANES, LANES), _F32),
                        pltpu.VMEM((s, LANES), _F32),
                        pltpu.VMEM((s, LANES), _F32),
                        pltpu.VMEM((s, LANES), _F32)],
        compiler_params=_params("arbitrary", "arbitrary"),
        name="rg_lru",
    )(xr, yg, xr_meta, conv_w, conv_b, w_cat, b_cat, c_decay)


def _mix_kernel(xp_ref, xs_ref, ap_ref, as_ref, rp_ref, rs_ref, g0_ref, b0_ref, wo_ref, g1_ref, b1_ref,
                rwt_ref, rb_ref, h1_ref, te_ref, gt_ref, rk_ref, cnt_ref, carry, *, n_prompt_tiles, tm):
    i = pl.program_id(0)

    @pl.when(i == 0)
    def _():
        carry[...] = jnp.zeros_like(carry)

    first = i < n_prompt_tiles
    x = jnp.where(first, xp_ref[...], xs_ref[...])
    attn = jnp.where(first, ap_ref[...], as_ref[...])
    rec = jnp.where(first, rp_ref[...], rs_ref[...])
    h0 = _layer_norm(x, g0_ref[...], b0_ref[...])
    mix = _dot(attn, wo_ref[0:Q_COLS, :]) + _dot(rec, wo_ref[Q_COLS:Q_COLS + LRU_WIDTH, :])
    h1 = _layer_norm(DN_ALPHA * h0 + mix, g1_ref[...], b1_ref[...])
    h1_ref[...] = h1

    logits = _dot_nt(rwt_ref[...], h1.astype(_BF16)) + rb_ref[:, 0:1]
    eidx = lax.broadcasted_iota(jnp.int32, (N_EXPERTS, tm), 0)
    work = logits
    vals, idxs, hots = [], [], []
    for _ in range(TOP_K):
        m = jnp.max(work, axis=0, keepdims=True)
        idx = jnp.min(jnp.where(work == m, eidx, N_EXPERTS), axis=0, keepdims=True)
        hot = eidx == idx
        vals.append(m)
        idxs.append(idx)
        hots.append(hot)
        work = jnp.where(hot, -jnp.inf, work)
    exps = [jnp.exp(v - vals[0]) for v in vals]
    den = exps[0] + exps[1] + exps[2] + exps[3]
    hot_all = (hots[0] | hots[1] | hots[2] | hots[3])
    hot_f = jnp.where(hot_all, 1.0, 0.0).astype(_F32)
    upper = (lax.broadcasted_iota(jnp.int32, (tm, tm), 0) < lax.broadcasted_iota(jnp.int32, (tm, tm), 1))
    before = _dot(hot_f.astype(_BF16), jnp.where(upper, 1.0, 0.0).astype(_BF16))
    base = carry[:, 0:1] + before
    zero_rows = jnp.zeros((SUBLANES - TOP_K, tm), _F32)
    ranks = [jnp.sum(jnp.where(h, base, 0.0), axis=0, keepdims=True) for h in hots]
    te_ref[...] = jnp.concatenate(idxs + [zero_rows.astype(jnp.int32)], axis=0)
    gt_ref[...] = jnp.concatenate([e / den for e in exps] + [zero_rows], axis=0)
    rk_ref[...] = jnp.concatenate(ranks + [zero_rows], axis=0).astype(jnp.int32)
    carry[...] = carry[...] + jnp.sum(hot_f, axis=1, keepdims=True)
    cnt_ref[...] = carry[...]


def _mix(x_p, x_s, attn_p, attn_s, rec_p, rec_s, g0, b0, w_out_bf16, g1, b1, rw_t, rb):
    tp, ts = x_p.shape[0], x_s.shape[0]
    t = tp + ts
    tm = min(TOKEN_TILE, tp, ts)
    assert tp % tm == 0 and ts % tm == 0
    npt = tp // tm
    first = lambda n: pl.BlockSpec((tm, n), lambda i: (jnp.minimum(i, npt - 1), 0))
    second = lambda n: pl.BlockSpec((tm, n), lambda i: (jnp.maximum(i - npt, 0), 0))
    lane_row = pl.BlockSpec((SUBLANES, tm), lambda i: (0, i))
    return pl.pallas_call(
        functools.partial(_mix_kernel, n_prompt_tiles=npt, tm=tm),
        grid=(t // tm,),
        in_specs=[first(D_MODEL), second(D_MODEL), first(Q_COLS), second(Q_COLS),
                  first(LRU_WIDTH), second(LRU_WIDTH),
                  _const_spec((1, D_MODEL)), _const_spec((1, D_MODEL)),
                  _const_spec((D_MODEL, D_MODEL)),
                  _const_spec((1, D_MODEL)), _const_spec((1, D_MODEL)),
                  _const_spec((N_EXPERTS, D_MODEL)), _const_spec((N_EXPERTS, LANES))],
        out_specs=[pl.BlockSpec((tm, D_MODEL), lambda i: (i, 0)), lane_row, lane_row, lane_row,
                   pl.BlockSpec((N_EXPERTS, LANES), lambda i: (0, 0))],
        out_shape=[jax.ShapeDtypeStruct((t, D_MODEL), _F32),
                   jax.ShapeDtypeStruct((SUBLANES, t), jnp.int32),
                   jax.ShapeDtypeStruct((SUBLANES, t), _F32),
                   jax.ShapeDtypeStruct((SUBLANES, t), jnp.int32),
                   jax.ShapeDtypeStruct((N_EXPERTS, LANES), _F32)],
        scratch_shapes=[pltpu.VMEM((N_EXPERTS, LANES), _F32)],
        compiler_params=_params("arbitrary"),
        name="mix_router",
    )(x_p, x_s, attn_p, attn_s, rec_p, rec_s, g0, b0, w_out_bf16, g1, b1, rw_t, rb)


def _moe_kernel(te_ref, tns_ref, slot_ref, h1_hbm,
                w1g_ref, w1l_ref, b1g_ref, b1l_ref, w2_ref, b2_ref, yk_hbm,
                land, xb, acc, w1g_b, w1l_b, w2_b, gsem, ssem, *, n_ff_steps):
    i = pl.program_id(0)
    f = pl.program_id(1)
    nsub = tns_ref[i]
    sub_tiles = MOE_SUB // SUBLANES

    def issue_gather(s, buf):
        def body(jj, c):
            for u in range(SUBLANES):
                tok = slot_ref[0, 0, s * MOE_SUB + jj * SUBLANES + u] & (MOE_SLOT_STRIDE - 1)
                pltpu.make_async_copy(h1_hbm.at[pl.ds(tok, 1)], land.at[buf, jj, pl.ds(u, 1)],
                                      gsem.at[buf]).start()
            return c
        lax.fori_loop(0, sub_tiles, body, 0)

    def wait_gather(buf):
        pltpu.make_async_copy(acc.at[pl.ds(0, sub_tiles)], land.at[buf], gsem.at[buf]).wait()

    def issue_scatter(t0, nt):
        def body(jj, c):
            for u in range(SUBLANES):
                dst = slot_ref[0, 0, (t0 + jj) * SUBLANES + u]
                pltpu.make_async_copy(acc.at[t0 + jj, pl.ds(u, 1)], yk_hbm.at[pl.ds(dst, 1)], ssem.at[0]).start()
            return c
        lax.fori_loop(0, nt, body, 0)

    def wait_scatter():
        pltpu.make_async_copy(acc.at[pl.ds(0, sub_tiles)], land.at[0], ssem.at[0]).wait()

    @pl.when(nsub > 0)
    def _active():
        @pl.when(f == 0)
        def _gather():
            issue_gather(0, 0)

            def sub(s, c):
                buf = lax.rem(s, 2)

                @pl.when(s + 1 < nsub)
                def _():
                    issue_gather(s + 1, 1 - buf)

                wait_gather(buf)
                xb[pl.ds(pl.multiple_of(s * MOE_SUB, MOE_SUB), MOE_SUB), :] = (
                    land[buf].reshape(MOE_SUB, D_MODEL).astype(_BF16))
                acc[pl.ds(pl.multiple_of(s * sub_tiles, sub_tiles), sub_tiles)] = jnp.broadcast_to(
                    b2_ref[0].reshape(1, 1, D_MODEL), (sub_tiles, SUBLANES, D_MODEL))
                return c

            lax.fori_loop(0, nsub, sub, 0)

        w1g_b[...] = w1g_ref[0].astype(_BF16)
        w1l_b[...] = w1l_ref[0].astype(_BF16)
        w2_b[...] = w2_ref[0].astype(_BF16)
        b1g = b1g_ref[0]
        b1l = b1l_ref[0]

        def compute(r0, n):
            nt = n // SUBLANES
            t0 = pl.multiple_of(r0 // SUBLANES, MOE_SUB // SUBLANES)
            x = xb[pl.ds(r0, n), :]
            glu = jnp.minimum(_dot(x, w1g_b[...]) + b1g, SWIGLU_LIMIT)
            lin = jnp.clip(_dot(x, w1l_b[...]) + b1l, -SWIGLU_LIMIT, SWIGLU_LIMIT)
            act = glu * jax.nn.sigmoid(SWIGLU_ALPHA * glu) * (lin + 1.0)
            acc[pl.ds(t0, nt)] += _dot(act.astype(_BF16), w2_b[...]).reshape(nt, SUBLANES, D_MODEL)

            @pl.when(f == n_ff_steps - 1)
            def _():
                issue_scatter(t0, nt)

        def big(s, c):
            compute(pl.multiple_of(s * (4 * MOE_SUB), 4 * MOE_SUB), 4 * MOE_SUB)
            return c

        n_big = nsub // 4
        lax.fori_loop(0, n_big, big, 0)
        done = n_big * 4

        @pl.when((nsub & 2) != 0)
        def _():
            compute(pl.multiple_of(done * MOE_SUB, MOE_SUB), 2 * MOE_SUB)

        @pl.when((nsub & 1) != 0)
        def _():
            compute(pl.multiple_of((done + (nsub & 2)) * MOE_SUB, MOE_SUB), MOE_SUB)

        @pl.when(f == n_ff_steps - 1)
        def _drain():
            def body(s, c):
                wait_scatter()
                return c
            lax.fori_loop(0, nsub, body, 0)


def _moe(tile_e, tile_nsub, slot_idx, h1, w1, b1, w2, b2):
    n_tiles = tile_e.shape[0]
    nf = D_FF // MOE_FF_CHUNK
    last = nf - 1

    def ff(i, f, tns):
        return jnp.where(tns[i] > 0, f, last)

    grid_spec = pltpu.PrefetchScalarGridSpec(
        num_scalar_prefetch=2,
        grid=(n_tiles, nf),
        in_specs=[
            pl.BlockSpec((1, 1, MOE_TILE), lambda i, f, te, tns: (i, 0, 0), memory_space=pltpu.SMEM),
            pl.BlockSpec(memory_space=pl.ANY),
            pl.BlockSpec((1, D_MODEL, MOE_FF_CHUNK), lambda i, f, te, tns: (te[i], 0, ff(i, f, tns))),
            pl.BlockSpec((1, D_MODEL, MOE_FF_CHUNK), lambda i, f, te, tns: (te[i], 0, nf + ff(i, f, tns))),
            pl.BlockSpec((1, 1, MOE_FF_CHUNK), lambda i, f, te, tns: (te[i], 0, ff(i, f, tns))),
            pl.BlockSpec((1, 1, MOE_FF_CHUNK), lambda i, f, te, tns: (te[i], 0, nf + ff(i, f, tns))),
            pl.BlockSpec((1, MOE_FF_CHUNK, D_MODEL), lambda i, f, te, tns: (te[i], ff(i, f, tns), 0)),
            pl.BlockSpec((1, 1, D_MODEL), lambda i, f, te, tns: (te[i], 0, 0)),
        ],
        out_specs=pl.BlockSpec(memory_space=pl.ANY),
        scratch_shapes=[
            pltpu.VMEM((2, MOE_SUB // SUBLANES, SUBLANES, D_MODEL), _F32),
            pltpu.VMEM((MOE_TILE, D_MODEL), _BF16),
            pltpu.VMEM((MOE_TILE // SUBLANES, SUBLANES, D_MODEL), _F32),
            pltpu.VMEM((D_MODEL, MOE_FF_CHUNK), _BF16),
            pltpu.VMEM((D_MODEL, MOE_FF_CHUNK), _BF16),
            pltpu.VMEM((MOE_FF_CHUNK, D_MODEL), _BF16),
            pltpu.SemaphoreType.DMA((2,)),
            pltpu.SemaphoreType.DMA((1,)),
        ],
    )
    return pl.pallas_call(
        functools.partial(_moe_kernel, n_ff_steps=nf),
        grid_spec=grid_spec,
        out_shape=jax.ShapeDtypeStruct((TOP_K * MOE_SLOT_STRIDE + MOE_TILE, D_MODEL), _F32),
        compiler_params=_params("arbitrary", "arbitrary"),
        name="moe_experts",
    )(tile_e, tile_nsub, slot_idx, h1, w1, w1, b1, b1, w2, b2)


def _combine_kernel(h1_ref, y0_ref, y1_ref, y2_ref, y3_ref, gt_ref, g_ref, b_ref, o_ref):
    gates = jnp.transpose(gt_ref[...])
    ffn = y0_ref[...] * gates[:, 0:1]
    for k, y_ref in enumerate((y1_ref, y2_ref, y3_ref), start=1):
        ffn = ffn + y_ref[...] * gates[:, k:k + 1]
    o_ref[...] = _layer_norm(DN_ALPHA * h1_ref[...] + ffn, g_ref[...], b_ref[...])


def _combine(h1, yk, gates_t, g2, b2, tok0, ntok):
    tm = COMBINE_TILE
    assert tok0 % tm == 0 and ntok % tm == 0 and MOE_SLOT_STRIDE % tm == 0
    off = tok0 // tm
    expert_rows = lambda k: pl.BlockSpec((tm, D_MODEL), lambda i: (k * (MOE_SLOT_STRIDE // tm) + off + i, 0))
    return pl.pallas_call(
        _combine_kernel,
        grid=(ntok // tm,),
        in_specs=[pl.BlockSpec((tm, D_MODEL), lambda i: (off + i, 0)),
                  expert_rows(0), expert_rows(1), expert_rows(2), expert_rows(3),
                  pl.BlockSpec((SUBLANES, tm), lambda i: (0, off + i)),
                  _const_spec((1, D_MODEL)), _const_spec((1, D_MODEL))],
        out_specs=pl.BlockSpec((tm, D_MODEL), lambda i: (i, 0)),
        out_shape=jax.ShapeDtypeStruct((ntok, D_MODEL), _F32),
        compiler_params=_params("arbitrary"),
        name="combine_ln2",
    )(h1, yk, yk, yk, yk, gates_t, g2, b2)


def _t5_bucket(rel):
    half = N_BUCKETS // 2
    exact = half // 2
    n = jnp.abs(rel)
    large = exact + (jnp.log(jnp.maximum(n, 1).astype(_F32) / exact)
                     / math.log(MAX_DISTANCE / exact) * (half - exact)).astype(jnp.int32)
    large = jnp.minimum(large, half - 1)
    return jnp.where(rel > 0, half, 0) + jnp.where(n < exact, n, large)


def _lookup(table, idx):
    out = jnp.zeros(idx.shape + table.shape[1:], table.dtype)
    expand = (...,) + (None,) * (table.ndim - 1)
    for j in range(table.shape[0]):
        out = out + jnp.where((idx == j)[expand], table[j], jnp.zeros((), table.dtype))
    return out


def _bias_tables(rel_bias, sink, nblk):
    rb = rel_bias.astype(_F32)
    qi = jnp.arange(BLOCK)
    kj = jnp.arange(3 * BLOCK)
    rel_band = kj[None, :] - BLOCK - qi[:, None]
    band = _lookup(rb, _t5_bucket(rel_band)).transpose(2, 0, 1)
    dist = jnp.arange(1, nblk * BLOCK + N_META + 1)
    by_dist = _lookup(rb, _t5_bucket(-dist))
    s = nblk * BLOCK
    meta = jnp.stack([by_dist[N_META - m - 1:N_META - m - 1 + s] for m in range(N_META)], axis=1)
    meta = meta.reshape(nblk, BLOCK, N_META, N_HEADS).transpose(0, 3, 1, 2)
    sink_col = jnp.broadcast_to(sink.astype(_F32)[None, :, None, None], (nblk, N_HEADS, BLOCK, 1))
    pad = jnp.zeros((nblk, N_HEADS, BLOCK, BLOCK - N_META - 1), _F32)
    return band, jnp.concatenate([meta, sink_col, pad], axis=-1)


def _dispatch_tables(top_e, rank, counts, t, n_tiles):
    assert t <= MOE_SLOT_STRIDE
    c = counts.astype(jnp.int32)
    nb = (c + MOE_SUB - 1) // MOE_SUB
    ns = (nb + MOE_SUBS_PER_TILE - 1) // MOE_SUBS_PER_TILE
    ends = jnp.cumsum(ns)
    first_tile = ends - ns

    e = top_e[:TOP_K]
    pos = _lookup(first_tile, e) * MOE_TILE + rank[:TOP_K]
    slot = jnp.arange(t, dtype=jnp.int32)[None, :] + jnp.arange(TOP_K, dtype=jnp.int32)[:, None] * MOE_SLOT_STRIDE
    trash = TOP_K * MOE_SLOT_STRIDE + jnp.arange(n_tiles * MOE_TILE, dtype=jnp.int32) % MOE_TILE
    slot_idx = trash.at[pos.reshape(-1)].set(slot.reshape(-1), unique_indices=True).reshape(n_tiles, 1, MOE_TILE)

    tid = jnp.arange(n_tiles, dtype=jnp.int32)
    used = tid < ends[-1]
    te = jnp.minimum(jnp.searchsorted(ends, tid, side='right').astype(jnp.int32), N_EXPERTS - 1)
    last_e = te[jnp.maximum(ends[-1] - 1, 0)]
    te = jnp.where(used, te, last_e)
    jt = tid - first_tile[te]
    nsub = jnp.where(used, jnp.clip(nb[te] - jt * MOE_SUBS_PER_TILE, 0, MOE_SUBS_PER_TILE), 0)
    return te, nsub.astype(jnp.int32), slot_idx


def kernel(x_prompt, x_sample, meta_tokens, ln_in_g, ln_in_b, rel_bias, w_in, conv_w, conv_b, lru_wa, lru_ba,
           lru_wi, lru_bi, lru_lam, attn_sink, w_out, ln1_g, ln1_b, router_w, router_b, exp_w1, exp_b1,
           exp_w2, exp_b2, ln2_g, ln2_b):
    assert DEPTH == 1 and w_in.shape[0] == 1
    row = lambda a: a.reshape(1, -1).astype(_F32)
    g0, b0 = row(ln_in_g), row(ln_in_b)
    w_in_b = w_in[0].astype(_BF16)
    w_out_b = w_out[0].astype(_BF16)

    wa, wi = lru_wa[0], lru_wi[0]
    w_cat = jnp.concatenate([wa[0], wa[1], wi[0], wi[1]], axis=-1).astype(_BF16)
    blk = lambda v: v.reshape(LRU_BLOCKS, 1, LRU_BLOCK_W)
    b_cat = jnp.concatenate([blk(lru_ba[0, 0]), blk(lru_ba[0, 1]), blk(lru_bi[0, 0]), blk(lru_bi[0, 1])],
                            axis=-1).astype(_F32)
    c_decay = -LRU_C * jax.nn.softplus(-lru_lam[0].astype(_F32))

    groups = [x_prompt, x_sample]
    nblk_max = max(x.shape[1] for x in groups) // BLOCK
    band_bias, tail_bias = _bias_tables(rel_bias, attn_sink[0], nblk_max)

    _, k_m, v_m, xr_m, _ = _in_proj(meta_tokens.astype(_F32), g0, b0, w_in_b)
    tail = lambda a: jnp.zeros((N_KV_HEADS, BLOCK, HEAD_DIM), _BF16).at[:, :N_META].set(
        a.reshape(N_META, N_KV_HEADS, HEAD_DIM).transpose(1, 0, 2))
    k_tail, v_tail = tail(k_m), tail(v_m)

    xs, attns, recs = [], [], []
    for x in groups:
        bsz, s = x.shape[0], x.shape[1]
        x2 = x.reshape(bsz * s, D_MODEL)
        q, k, v, xr, yg = _in_proj(x2, g0, b0, w_in_b)
        attn = _attention(q.reshape(bsz, s, Q_COLS), k.reshape(bsz, s, KV_COLS), v.reshape(bsz, s, KV_COLS),
                          k_tail, v_tail, band_bias, tail_bias)
        rec = _rg_lru(xr.reshape(bsz, s, LRU_WIDTH), yg.reshape(bsz, s, LRU_WIDTH), xr_m,
                      conv_w[0].astype(_F32), row(conv_b[0]), w_cat, b_cat, c_decay)
        xs.append(x2)
        attns.append(attn.reshape(bsz * s, Q_COLS))
        recs.append(rec.reshape(bsz * s, LRU_WIDTH))

    rw_t = router_w[0].T.astype(_BF16)
    rb = jnp.broadcast_to(router_b[0].astype(_F32)[:, None], (N_EXPERTS, LANES))
    h1, top_e, gates_t, rank, counts = _mix(xs[0], xs[1], attns[0], attns[1], recs[0], recs[1], g0, b0,
                                            w_out_b, row(ln1_g[0]), row(ln1_b[0]), rw_t, rb)

    t = h1.shape[0]
    n_sub_max = TOP_K * t // MOE_SUB + N_EXPERTS
    n_tiles = -(-n_sub_max // MOE_SUBS_PER_TILE) + N_EXPERTS
    te, nsub, slot_idx = _dispatch_tables(top_e, rank, counts[:, 0], t, n_tiles)
    yk = _moe(te, nsub, slot_idx, h1,
              exp_w1[0], exp_b1[0].reshape(N_EXPERTS, 1, 2 * D_FF), exp_w2[0],
              exp_b2[0].reshape(N_EXPERTS, 1, D_MODEL))

    outs = []
    tok0 = 0
    for x in groups:
        bsz, s = x.shape[0], x.shape[1]
        y = _combine(h1, yk, gates_t, row(ln2_g[0]), row(ln2_b[0]), tok0, bsz * s)
        outs.append(y.reshape(bsz, s, D_MODEL))
        tok0 += bsz * s
    return tuple(outs)
```

```python
import functools
import math

import jax
import jax.numpy as jnp
from jax import lax
from jax.experimental import pallas as pl
from jax.experimental.pallas import tpu as pltpu

D_MODEL = 2048
HEAD_DIM = 128
N_HEADS = 8
N_KV_HEADS = 2
GQA_GROUP = N_HEADS // N_KV_HEADS
Q_COLS = N_HEADS * HEAD_DIM
KV_COLS = N_KV_HEADS * HEAD_DIM
LRU_WIDTH = 1024
LRU_BLOCKS = 8
LRU_BLOCK_W = LRU_WIDTH // LRU_BLOCKS
LRU_C = 8.0
IN_COLS = Q_COLS + 2 * KV_COLS + 2 * LRU_WIDTH
WINDOW = 128
BLOCK = 128
N_BUCKETS = 32
MAX_DISTANCE = 128
N_META = 16
N_EXPERTS = 32
TOP_K = 4
D_FF = D_MODEL
SWIGLU_LIMIT = 7.0
SWIGLU_ALPHA = 1.702
DEPTH = 1
DN_ALPHA = (2.0 * DEPTH) ** 0.25
LN_EPS = 1e-5
NEG_INF = -1e30

SUBLANES = 8
LANES = 128
VMEM_LIMIT_BYTES = 56 * 1024 * 1024

TOKEN_TILE = 512
LRU_CHUNK = 256
MOE_SUB = 256
MOE_SUBS_PER_TILE = 8
MOE_TILE = MOE_SUB * MOE_SUBS_PER_TILE
MOE_SLOT_STRIDE = 1 << 16
MOE_FF_CHUNK = 256
COMBINE_TILE = 256

_BF16 = jnp.bfloat16
_F32 = jnp.float32


def _layer_norm(x, g, b):
    mu = jnp.mean(x, axis=-1, keepdims=True)
    xc = x - mu
    var = jnp.mean(xc * xc, axis=-1, keepdims=True)
    return xc * lax.rsqrt(var + LN_EPS) * g + b


def _dot(a, b):
    return jnp.dot(a, b, preferred_element_type=_F32)


def _dot_nt(a, b):
    return lax.dot_general(a, b, (((1,), (1,)), ((), ())), preferred_element_type=_F32)


def _params(*semantics):
    return pltpu.CompilerParams(dimension_semantics=semantics, vmem_limit_bytes=VMEM_LIMIT_BYTES)


def _const_spec(shape):
    nd = len(shape)
    return pl.BlockSpec(shape, lambda *_: (0,) * nd, pipeline_mode=pl.Buffered(1))


def _in_proj_kernel(x_ref, g_ref, b_ref, w_ref, q_ref, k_ref, v_ref, xr_ref, yg_ref):
    h = _layer_norm(x_ref[...], g_ref[...], b_ref[...]).astype(_BF16)
    c0, c1, c2, c3 = Q_COLS, Q_COLS + KV_COLS, Q_COLS + 2 * KV_COLS, Q_COLS + 2 * KV_COLS + LRU_WIDTH
    q_ref[...] = _dot(h, w_ref[:, 0:c0]).astype(_BF16)
    k_ref[...] = _dot(h, w_ref[:, c0:c1]).astype(_BF16)
    v_ref[...] = _dot(h, w_ref[:, c1:c2]).astype(_BF16)
    xr_ref[...] = _dot(h, w_ref[:, c2:c3])
    yg_ref[...] = _dot(h, w_ref[:, c3:IN_COLS])


def _in_proj(x, g, b, w_bf16):
    t = x.shape[0]
    tm = min(TOKEN_TILE, t)
    assert t % tm == 0
    row = lambda n: pl.BlockSpec((tm, n), lambda i: (i, 0))
    return pl.pallas_call(
        _in_proj_kernel,
        grid=(t // tm,),
        in_specs=[row(D_MODEL), _const_spec((1, D_MODEL)), _const_spec((1, D_MODEL)),
                  _const_spec((D_MODEL, IN_COLS))],
        out_specs=[row(Q_COLS), row(KV_COLS), row(KV_COLS), row(LRU_WIDTH), row(LRU_WIDTH)],
        out_shape=[jax.ShapeDtypeStruct((t, Q_COLS), _BF16),
                   jax.ShapeDtypeStruct((t, KV_COLS), _BF16),
                   jax.ShapeDtypeStruct((t, KV_COLS), _BF16),
                   jax.ShapeDtypeStruct((t, LRU_WIDTH), _F32),
                   jax.ShapeDtypeStruct((t, LRU_WIDTH), _F32)],
        compiler_params=_params("arbitrary"),
        name="in_proj",
    )(x, g, b, w_bf16)


def _attn_kernel(q_ref, kp_ref, kc_ref, kn_ref, vp_ref, vc_ref, vn_ref, kt_ref, vt_ref,
                 bb_ref, tb_ref, o_ref, *, nblk):
    i = pl.program_id(2)
    q = q_ref[0]
    kb = jnp.concatenate([kp_ref[0], kc_ref[0], kn_ref[0]], axis=0)
    vb = jnp.concatenate([vp_ref[0], vc_ref[0], vn_ref[0]], axis=0)
    kt = kt_ref[0]
    vt = vt_ref[0]
    qi = lax.broadcasted_iota(jnp.int32, (BLOCK, 3 * BLOCK), 0)
    kj = lax.broadcasted_iota(jnp.int32, (BLOCK, 3 * BLOCK), 1)
    rel = kj - BLOCK - qi
    valid = (jnp.abs(rel) <= WINDOW) & ((kj >= BLOCK) | (i > 0)) & ((kj < 2 * BLOCK) | (i < nblk - 1))
    tail_valid = lax.broadcasted_iota(jnp.int32, (BLOCK, BLOCK), 1) <= N_META
    scale = HEAD_DIM ** -0.5
    qs = jnp.concatenate([q[:, g * HEAD_DIM:(g + 1) * HEAD_DIM] for g in range(GQA_GROUP)], axis=0)
    valid = jnp.concatenate([valid] * GQA_GROUP, axis=0)
    tail_valid = jnp.concatenate([tail_valid] * GQA_GROUP, axis=0)
    band_bias = bb_ref[...].reshape(GQA_GROUP * BLOCK, 3 * BLOCK)
    tail_bias = tb_ref[0].reshape(GQA_GROUP * BLOCK, BLOCK)
    s_b = jnp.where(valid, _dot_nt(qs, kb) * scale + band_bias, NEG_INF)
    s_t = jnp.where(tail_valid, _dot_nt(qs, kt) * scale + tail_bias, NEG_INF)
    m = jnp.maximum(jnp.max(s_b, axis=-1, keepdims=True), jnp.max(s_t, axis=-1, keepdims=True))
    e_b = jnp.exp(s_b - m)
    e_t = jnp.exp(s_t - m)
    den = jnp.sum(e_b, axis=-1, keepdims=True) + jnp.sum(e_t, axis=-1, keepdims=True)
    inv = 1.0 / den
    p_b = (e_b * inv).astype(_BF16)
    p_t = (e_t * inv).astype(_BF16)
    o = (_dot(p_b, vb) + _dot(p_t, vt)).astype(_BF16)
    for g in range(GQA_GROUP):
        o_ref[0, :, g * HEAD_DIM:(g + 1) * HEAD_DIM] = o[g * BLOCK:(g + 1) * BLOCK]


def _attention(q, k, v, k_tail, v_tail, band_bias, tail_bias):
    bsz, s = q.shape[0], q.shape[1]
    nblk = s // BLOCK
    gw = GQA_GROUP * HEAD_DIM
    kv_spec = lambda d: pl.BlockSpec(
        (1, BLOCK, HEAD_DIM), lambda b, h, i: (b, jnp.clip(i + d, 0, nblk - 1), h))
    tail_spec = pl.BlockSpec((1, BLOCK, HEAD_DIM), lambda b, h, i: (h, 0, 0))
    return pl.pallas_call(
        functools.partial(_attn_kernel, nblk=nblk),
        grid=(bsz, N_KV_HEADS, nblk),
        in_specs=[pl.BlockSpec((1, BLOCK, gw), lambda b, h, i: (b, i, h)),
                  kv_spec(-1), kv_spec(0), kv_spec(1), kv_spec(-1), kv_spec(0), kv_spec(1),
                  tail_spec, tail_spec,
                  pl.BlockSpec((GQA_GROUP, BLOCK, 3 * BLOCK), lambda b, h, i: (h, 0, 0)),
                  pl.BlockSpec((1, GQA_GROUP, BLOCK, BLOCK), lambda b, h, i: (i, h, 0, 0))],
        out_specs=pl.BlockSpec((1, BLOCK, gw), lambda b, h, i: (b, i, h)),
        out_shape=jax.ShapeDtypeStruct((bsz, s, Q_COLS), _BF16),
        compiler_params=_params("arbitrary", "arbitrary", "arbitrary"),
        name="attention",
    )(q, k, k, k, v, v, v, k_tail, v_tail, band_bias, tail_bias)


_X0 = 24


def _tile_scan(a, u, h_in, reverse):
    n = a.shape[0]
    nt = n // SUBLANES
    a3 = a.reshape(nt, SUBLANES, LANES)
    u3 = u.reshape(nt, SUBLANES, LANES)
    row = lax.broadcasted_iota(jnp.int32, (nt, SUBLANES, LANES), 1)
    for step in (1, 2, 4):
        shift = (SUBLANES - step) if reverse else step
        a_sh = pltpu.roll(a3, shift, 1)
        u_sh = pltpu.roll(u3, shift, 1)
        ok = (row < SUBLANES - step) if reverse else (row >= step)
        u3 = jnp.where(ok, a3 * u_sh + u3, u3)
        a3 = jnp.where(ok, a3 * a_sh, a3)
    out = [None] * nt
    h = h_in
    order = range(nt - 1, -1, -1) if reverse else range(nt)
    edge = 0 if reverse else SUBLANES - 1
    for j in order:
        ht = u3[j] + a3[j] * h
        out[j] = ht
        h = ht[edge:edge + 1]
    return jnp.concatenate(out, axis=0), h


def _lru_kernel(xr_ref, yg_ref, xm_ref, cw_ref, cb_ref, w_ref, b_ref, c_ref, o_ref,
                xext, hf, ab, ub, *, seq):
    nchunk = seq // LRU_CHUNK
    zeros8 = jnp.zeros((SUBLANES, LANES), _F32)
    xext[0:SUBLANES, :] = zeros8
    xext[SUBLANES:_X0, :] = xm_ref[...]
    xext[_X0:_X0 + seq, :] = xr_ref[0]
    xext[_X0 + seq:_X0 + seq + SUBLANES, :] = zeros8
    cw = cw_ref[...]
    cb = cb_ref[...]
    w = w_ref[0]
    bias = b_ref[0]
    c_f = c_ref[0:1, :]
    c_b = c_ref[1:2, :]

    def gates(r0, n):
        win = xext[pl.ds(r0 - SUBLANES, n + 2 * SUBLANES), :]
        span = n + 2 * SUBLANES
        tap = lambda d: pltpu.roll(win, (span - d) % span, 0)[SUBLANES:SUBLANES + n]
        xc = cb + cw[0:1] * tap(-2)
        xc = xc + cw[1:2] * tap(-1)
        xc = xc + cw[2:3] * win[SUBLANES:SUBLANES + n]
        xc = xc + cw[3:4] * tap(1)
        g = _dot(xc.astype(_BF16), w) + bias
        r_f = jax.nn.sigmoid(g[:, 0:LANES])
        r_b = jax.nn.sigmoid(g[:, LANES:2 * LANES])
        i_f = jax.nn.sigmoid(g[:, 2 * LANES:3 * LANES])
        i_b = jax.nn.sigmoid(g[:, 3 * LANES:4 * LANES])
        la_f = c_f * r_f
        la_b = c_b * r_b
        a_f = jnp.exp(la_f)
        a_b = jnp.exp(la_b)
        u_f = jnp.sqrt(1.0 - a_f * a_f) * (i_f * xc)
        u_b = jnp.sqrt(1.0 - a_b * a_b) * (i_b * xc)
        return a_f, u_f, a_b, u_b

    a_f, u_f, _, _ = gates(SUBLANES, N_META)
    _, h0 = _tile_scan(a_f, u_f, jnp.zeros((1, LANES), _F32), reverse=False)

    def fwd(c, h):
        t0 = pl.multiple_of(c * LRU_CHUNK, LRU_CHUNK)
        a_f, u_f, a_b, u_b = gates(_X0 + t0, LRU_CHUNK)
        hs, h = _tile_scan(a_f, u_f, h, reverse=False)
        hf[pl.ds(t0, LRU_CHUNK), :] = hs
        ab[pl.ds(t0, LRU_CHUNK), :] = a_b
        ub[pl.ds(t0, LRU_CHUNK), :] = u_b
        return h

    lax.fori_loop(0, nchunk, fwd, h0)

    def bwd(cc, h):
        c = nchunk - 1 - cc
        t0 = pl.multiple_of(c * LRU_CHUNK, LRU_CHUNK)
        hs, h = _tile_scan(ab[pl.ds(t0, LRU_CHUNK), :], ub[pl.ds(t0, LRU_CHUNK), :], h, reverse=True)
        tot = hf[pl.ds(t0, LRU_CHUNK), :] + hs
        o_ref[0, pl.ds(t0, LRU_CHUNK), :] = (tot * jax.nn.gelu(yg_ref[0, pl.ds(t0, LRU_CHUNK), :])).astype(_BF16)
        return h

    lax.fori_loop(0, nchunk, bwd, jnp.zeros((1, LANES), _F32))


def _rg_lru(xr, yg, xr_meta, conv_w, conv_b, w_cat, b_cat, c_decay):
    bsz, s = xr.shape[0], xr.shape[1]
    assert s % LRU_CHUNK == 0
    slab = pl.BlockSpec((1, s, LANES), lambda b, n: (b, 0, n))
    col = lambda r: pl.BlockSpec((r, LANES), lambda b, n: (0, n))
    return pl.pallas_call(
        functools.partial(_lru_kernel, seq=s),
        grid=(bsz, LRU_BLOCKS),
        in_specs=[slab, slab, col(N_META), col(4), col(1),
                  pl.BlockSpec((1, LANES, 4 * LANES), lambda b, n: (n, 0, 0)),
                  pl.BlockSpec((1, 1, 4 * LANES), lambda b, n: (n, 0, 0)),
                  col(2)],
        out_specs=slab,
        out_shape=jax.ShapeDtypeStruct((bsz, s, LRU_WIDTH), _BF16),
        scratch_shapes=[pltpu.VMEM((s + _X0 + SUBLANES, LANES), _F32),
                        pltpu.VMEM((s, LANES), _F32),
                        pltpu.VMEM((s, LANES), _F32),
                        pltpu.VMEM((s, LANES), _F32)],
        compiler_params=_params("arbitrary", "arbitrary"),
        name="rg_lru",
    )(xr, yg, xr_meta, conv_w, conv_b, w_cat, b_cat, c_decay)


def _mix_kernel(xp_ref, xs_ref, ap_ref, as_ref, rp_ref, rs_ref, g0_ref, b0_ref, wo_ref, g1_ref, b1_ref,
                rwt_ref, rb_ref, h1_ref, te_ref, gt_ref, rk_ref, cnt_ref, carry, *, n_prompt_tiles, tm):
    i = pl.program_id(0)

    @pl.when(i == 0)
    def _():
        carry[...] = jnp.zeros_like(carry)

    first = i < n_prompt_tiles
    x = jnp.where(first, xp_ref[...], xs_ref[...])
    attn = jnp.where(first, ap_ref[...], as_ref[...])
    rec = jnp.where(first, rp_ref[...], rs_ref[...])
    h0 = _layer_norm(x, g0_ref[...], b0_ref[...])
    mix = _dot(attn, wo_ref[0:Q_COLS, :]) + _dot(rec, wo_ref[Q_COLS:Q_COLS + LRU_WIDTH, :])
    h1 = _layer_norm(DN_ALPHA * h0 + mix, g1_ref[...], b1_ref[...])
    h1_ref[...] = h1

    logits = _dot_nt(rwt_ref[...], h1.astype(_BF16)) + rb_ref[:, 0:1]
    eidx = lax.broadcasted_iota(jnp.int32, (N_EXPERTS, tm), 0)
    work = logits
    vals, idxs, hots = [], [], []
    for _ in range(TOP_K):
        m = jnp.max(work, axis=0, keepdims=True)
        idx = jnp.min(jnp.where(work == m, eidx, N_EXPERTS), axis=0, keepdims=True)
        hot = eidx == idx
        vals.append(m)
        idxs.append(idx)
        hots.append(hot)
        work = jnp.where(hot, -jnp.inf, work)
    exps = [jnp.exp(v - vals[0]) for v in vals]
    den = exps[0] + exps[1] + exps[2] + exps[3]
    hot_all = (hots[0] | hots[1] | hots[2] | hots[3])
    hot_f = jnp.where(hot_all, 1.0, 0.0).astype(_F32)
    upper = (lax.broadcasted_iota(jnp.int32, (tm, tm), 0) < lax.broadcasted_iota(jnp.int32, (tm, tm), 1))
    before = _dot(hot_f.astype(_BF16), jnp.where(upper, 1.0, 0.0).astype(_BF16))
    base = carry[:, 0:1] + before
    zero_rows = jnp.zeros((SUBLANES - TOP_K, tm), _F32)
    ranks = [jnp.sum(jnp.where(h, base, 0.0), axis=0, keepdims=True) for h in hots]
    te_ref[...] = jnp.concatenate(idxs + [zero_rows.astype(jnp.int32)], axis=0)
    gt_ref[...] = jnp.concatenate([e / den for e in exps] + [zero_rows], axis=0)
    rk_ref[...] = jnp.concatenate(ranks + [zero_rows], axis=0).astype(jnp.int32)
    carry[...] = carry[...] + jnp.sum(hot_f, axis=1, keepdims=True)
    cnt_ref[...] = carry[...]


def _mix(x_p, x_s, attn_p, attn_s, rec_p, rec_s, g0, b0, w_out_bf16, g1, b1, rw_t, rb):
    tp, ts = x_p.shape[0], x_s.shape[0]
    t = tp + ts
    tm = min(TOKEN_TILE, tp, ts)
    assert tp % tm == 0 and ts % tm == 0
    npt = tp // tm
    first = lambda n: pl.BlockSpec((tm, n), lambda i: (jnp.minimum(i, npt - 1), 0))
    second = lambda n: pl.BlockSpec((tm, n), lambda i: (jnp.maximum(i - npt, 0), 0))
    lane_row = pl.BlockSpec((SUBLANES, tm), lambda i: (0, i))
    return pl.pallas_call(
        functools.partial(_mix_kernel, n_prompt_tiles=npt, tm=tm),
        grid=(t // tm,),
        in_specs=[first(D_MODEL), second(D_MODEL), first(Q_COLS), second(Q_COLS),
                  first(LRU_WIDTH), second(LRU_WIDTH),
                  _const_spec((1, D_MODEL)), _const_spec((1, D_MODEL)),
                  _const_spec((D_MODEL, D_MODEL)),
                  _const_spec((1, D_MODEL)), _const_spec((1, D_MODEL)),
                  _const_spec((N_EXPERTS, D_MODEL)), _const_spec((N_EXPERTS, LANES))],
        out_specs=[pl.BlockSpec((tm, D_MODEL), lambda i: (i, 0)), lane_row, lane_row, lane_row,
                   pl.BlockSpec((N_EXPERTS, LANES), lambda i: (0, 0))],
        out_shape=[jax.ShapeDtypeStruct((t, D_MODEL), _F32),
                   jax.ShapeDtypeStruct((SUBLANES, t), jnp.int32),
                   jax.ShapeDtypeStruct((SUBLANES, t), _F32),
                   jax.ShapeDtypeStruct((SUBLANES, t), jnp.int32),
                   jax.ShapeDtypeStruct((N_EXPERTS, LANES), _F32)],
        scratch_shapes=[pltpu.VMEM((N_EXPERTS, LANES), _F32)],
        compiler_params=_params("arbitrary"),
        name="mix_router",
    )(x_p, x_s, attn_p, attn_s, rec_p, rec_s, g0, b0, w_out_bf16, g1, b1, rw_t, rb)


def _moe_kernel(te_ref, tns_ref, slot_ref, h1_hbm,
                w1g_ref, w1l_ref, b1g_ref, b1l_ref, w2_ref, b2_ref, yk_hbm,
                land, xb, acc, gsem, ssem, *, n_ff_steps):
    i = pl.program_id(0)
    f = pl.program_id(1)
    nsub = tns_ref[i]
    sub_tiles = MOE_SUB // SUBLANES

    def issue_gather(s, buf):
        def body(jj, c):
            for u in range(SUBLANES):
                tok = slot_ref[0, 0, s * MOE_SUB + jj * SUBLANES + u] & (MOE_SLOT_STRIDE - 1)
                pltpu.make_async_copy(h1_hbm.at[pl.ds(tok, 1)], land.at[buf, jj, pl.ds(u, 1)],
                                      gsem.at[buf]).start()
            return c
        lax.fori_loop(0, sub_tiles, body, 0)

    def wait_gather(buf):
        pltpu.make_async_copy(acc.at[pl.ds(0, sub_tiles)], land.at[buf], gsem.at[buf]).wait()

    def issue_scatter(t0, nt):
        def body(jj, c):
            for u in range(SUBLANES):
                dst = slot_ref[0, 0, (t0 + jj) * SUBLANES + u]
                pltpu.make_async_copy(acc.at[t0 + jj, pl.ds(u, 1)], yk_hbm.at[pl.ds(dst, 1)], ssem.at[0]).start()
            return c
        lax.fori_loop(0, nt, body, 0)

    def wait_scatter():
        pltpu.make_async_copy(acc.at[pl.ds(0, sub_tiles)], land.at[0], ssem.at[0]).wait()

    @pl.when(nsub > 0)
    def _active():
        @pl.when(f == 0)
        def _gather():
            issue_gather(0, 0)

            def sub(s, c):
                buf = lax.rem(s, 2)

                @pl.when(s + 1 < nsub)
                def _():
                    issue_gather(s + 1, 1 - buf)

                wait_gather(buf)
                xb[pl.ds(pl.multiple_of(s * MOE_SUB, MOE_SUB), MOE_SUB), :] = (
                    land[buf].reshape(MOE_SUB, D_MODEL).astype(_BF16))
                acc[pl.ds(pl.multiple_of(s * sub_tiles, sub_tiles), sub_tiles)] = jnp.broadcast_to(
                    b2_ref[0].reshape(1, 1, D_MODEL), (sub_tiles, SUBLANES, D_MODEL))
                return c

            lax.fori_loop(0, nsub, sub, 0)

        b1g = b1g_ref[0]
        b1l = b1l_ref[0]

        def compute(r0, n):
            nt = n // SUBLANES
            t0 = pl.multiple_of(r0 // SUBLANES, MOE_SUB // SUBLANES)
            x = xb[pl.ds(r0, n), :]
            glu = jnp.minimum(_dot(x, w1g_ref[0].astype(_BF16)) + b1g, SWIGLU_LIMIT)
            lin = jnp.clip(_dot(x, w1l_ref[0].astype(_BF16)) + b1l, -SWIGLU_LIMIT, SWIGLU_LIMIT)
            act = glu * jax.nn.sigmoid(SWIGLU_ALPHA * glu) * (lin + 1.0)
            acc[pl.ds(t0, nt)] += _dot(act.astype(_BF16), w2_ref[0].astype(_BF16)).reshape(nt, SUBLANES, D_MODEL)

            @pl.when(f == n_ff_steps - 1)
            def _():
                issue_scatter(t0, nt)

        def big(s, c):
            compute(pl.multiple_of(s * (4 * MOE_SUB), 4 * MOE_SUB), 4 * MOE_SUB)
            return c

        n_big = nsub // 4
        lax.fori_loop(0, n_big, big, 0)
        done = n_big * 4

        @pl.when((nsub & 2) != 0)
        def _():
            compute(pl.multiple_of(done * MOE_SUB, MOE_SUB), 2 * MOE_SUB)

        @pl.when((nsub & 1) != 0)
        def _():
            compute(pl.multiple_of((done + (nsub & 2)) * MOE_SUB, MOE_SUB), MOE_SUB)

        @pl.when(f == n_ff_steps - 1)
        def _drain():
            def body(s, c):
                wait_scatter()
                return c
            lax.fori_loop(0, nsub, body, 0)


def _moe(tile_e, tile_nsub, slot_idx, h1, w1, b1, w2, b2):
    n_tiles = tile_e.shape[0]
    nf = D_FF // MOE_FF_CHUNK
    last = nf - 1

    def ff(i, f, tns):
        return jnp.where(tns[i] > 0, f, last)

    grid_spec = pltpu.PrefetchScalarGridSpec(
        num_scalar_prefetch=2,
        grid=(n_tiles, nf),
        in_specs=[
            pl.BlockSpec((1, 1, MOE_TILE), lambda i, f, te, tns: (i, 0, 0), memory_space=pltpu.SMEM),
            pl.BlockSpec(memory_space=pl.ANY),
            pl.BlockSpec((1, D_MODEL, MOE_FF_CHUNK), lambda i, f, te, tns: (te[i], 0, ff(i, f, tns))),
            pl.BlockSpec((1, D_MODEL, MOE_FF_CHUNK), lambda i, f, te, tns: (te[i], 0, nf + ff(i, f, tns))),
            pl.BlockSpec((1, 1, MOE_FF_CHUNK), lambda i, f, te, tns: (te[i], 0, ff(i, f, tns))),
            pl.BlockSpec((1, 1, MOE_FF_CHUNK), lambda i, f, te, tns: (te[i], 0, nf + ff(i, f, tns))),
            pl.BlockSpec((1, MOE_FF_CHUNK, D_MODEL), lambda i, f, te, tns: (te[i], ff(i, f, tns), 0)),
            pl.BlockSpec((1, 1, D_MODEL), lambda i, f, te, tns: (te[i], 0, 0)),
        ],
        out_specs=pl.BlockSpec(memory_space=pl.ANY),
        scratch_shapes=[
            pltpu.VMEM((2, MOE_SUB // SUBLANES, SUBLANES, D_MODEL), _F32),
            pltpu.VMEM((MOE_TILE, D_MODEL), _BF16),
            pltpu.VMEM((MOE_TILE // SUBLANES, SUBLANES, D_MODEL), _F32),
            pltpu.SemaphoreType.DMA((2,)),
            pltpu.SemaphoreType.DMA((1,)),
        ],
    )
    return pl.pallas_call(
        functools.partial(_moe_kernel, n_ff_steps=nf),
        grid_spec=grid_spec,
        out_shape=jax.ShapeDtypeStruct((TOP_K * MOE_SLOT_STRIDE + MOE_TILE, D_MODEL), _F32),
        compiler_params=_params("arbitrary", "arbitrary"),
        name="moe_experts",
    )(tile_e, tile_nsub, slot_idx, h1, w1, w1, b1, b1, w2, b2)


def _combine_kernel(h1_ref, y0_ref, y1_ref, y2_ref, y3_ref, gt_ref, g_ref, b_ref, o_ref):
    gates = jnp.transpose(gt_ref[...])
    ffn = y0_ref[...] * gates[:, 0:1]
    for k, y_ref in enumerate((y1_ref, y2_ref, y3_ref), start=1):
        ffn = ffn + y_ref[...] * gates[:, k:k + 1]
    o_ref[...] = _layer_norm(DN_ALPHA * h1_ref[...] + ffn, g_ref[...], b_ref[...])


def _combine(h1, yk, gates_t, g2, b2, tok0, ntok):
    tm = COMBINE_TILE
    assert tok0 % tm == 0 and ntok % tm == 0 and MOE_SLOT_STRIDE % tm == 0
    off = tok0 // tm
    expert_rows = lambda k: pl.BlockSpec((tm, D_MODEL), lambda i: (k * (MOE_SLOT_STRIDE // tm) + off + i, 0))
    return pl.pallas_call(
        _combine_kernel,
        grid=(ntok // tm,),
        in_specs=[pl.BlockSpec((tm, D_MODEL), lambda i: (off + i, 0)),
                  expert_rows(0), expert_rows(1), expert_rows(2), expert_rows(3),
                  pl.BlockSpec((SUBLANES, tm), lambda i: (0, off + i)),
                  _const_spec((1, D_MODEL)), _const_spec((1, D_MODEL))],
        out_specs=pl.BlockSpec((tm, D_MODEL), lambda i: (i, 0)),
        out_shape=jax.ShapeDtypeStruct((ntok, D_MODEL), _F32),
        compiler_params=_params("arbitrary"),
        name="combine_ln2",
    )(h1, yk, yk, yk, yk, gates_t, g2, b2)


def _t5_bucket(rel):
    half = N_BUCKETS // 2
    exact = half // 2
    n = jnp.abs(rel)
    large = exact + (jnp.log(jnp.maximum(n, 1).astype(_F32) / exact)
                     / math.log(MAX_DISTANCE / exact) * (half - exact)).astype(jnp.int32)
    large = jnp.minimum(large, half - 1)
    return jnp.where(rel > 0, half, 0) + jnp.where(n < exact, n, large)


def _lookup(table, idx):
    out = jnp.zeros(idx.shape + table.shape[1:], table.dtype)
    expand = (...,) + (None,) * (table.ndim - 1)
    for j in range(table.shape[0]):
        out = out + jnp.where((idx == j)[expand], table[j], jnp.zeros((), table.dtype))
    return out


def _bias_tables(rel_bias, sink, nblk):
    rb = rel_bias.astype(_F32)
    qi = jnp.arange(BLOCK)
    kj = jnp.arange(3 * BLOCK)
    rel_band = kj[None, :] - BLOCK - qi[:, None]
    band = _lookup(rb, _t5_bucket(rel_band)).transpose(2, 0, 1)
    dist = jnp.arange(1, nblk * BLOCK + N_META + 1)
    by_dist = _lookup(rb, _t5_bucket(-dist))
    s = nblk * BLOCK
    meta = jnp.stack([by_dist[N_META - m - 1:N_META - m - 1 + s] for m in range(N_META)], axis=1)
    meta = meta.reshape(nblk, BLOCK, N_META, N_HEADS).transpose(0, 3, 1, 2)
    sink_col = jnp.broadcast_to(sink.astype(_F32)[None, :, None, None], (nblk, N_HEADS, BLOCK, 1))
    pad = jnp.zeros((nblk, N_HEADS, BLOCK, BLOCK - N_META - 1), _F32)
    return band, jnp.concatenate([meta, sink_col, pad], axis=-1)


def _dispatch_tables(top_e, rank, counts, t, n_tiles):
    assert t <= MOE_SLOT_STRIDE
    c = counts.astype(jnp.int32)
    nb = (c + MOE_SUB - 1) // MOE_SUB
    ns = (nb + MOE_SUBS_PER_TILE - 1) // MOE_SUBS_PER_TILE
    ends = jnp.cumsum(ns)
    first_tile = ends - ns

    e = top_e[:TOP_K]
    pos = _lookup(first_tile, e) * MOE_TILE + rank[:TOP_K]
    slot = jnp.arange(t, dtype=jnp.int32)[None, :] + jnp.arange(TOP_K, dtype=jnp.int32)[:, None] * MOE_SLOT_STRIDE
    trash = TOP_K * MOE_SLOT_STRIDE + jnp.arange(n_tiles * MOE_TILE, dtype=jnp.int32) % MOE_TILE
    slot_idx = trash.at[pos.reshape(-1)].set(slot.reshape(-1), unique_indices=True).reshape(n_tiles, 1, MOE_TILE)

    tid = jnp.arange(n_tiles, dtype=jnp.int32)
    used = tid < ends[-1]
    te = jnp.minimum(jnp.searchsorted(ends, tid, side='right').astype(jnp.int32), N_EXPERTS - 1)
    last_e = te[jnp.maximum(ends[-1] - 1, 0)]
    te = jnp.where(used, te, last_e)
    jt = tid - first_tile[te]
    nsub = jnp.where(used, jnp.clip(nb[te] - jt * MOE_SUBS_PER_TILE, 0, MOE_SUBS_PER_TILE), 0)
    return te, nsub.astype(jnp.int32), slot_idx


def kernel(x_prompt, x_sample, meta_tokens, ln_in_g, ln_in_b, rel_bias, w_in, conv_w, conv_b, lru_wa, lru_ba,
           lru_wi, lru_bi, lru_lam, attn_sink, w_out, ln1_g, ln1_b, router_w, router_b, exp_w1, exp_b1,
           exp_w2, exp_b2, ln2_g, ln2_b):
    assert DEPTH == 1 and w_in.shape[0] == 1
    row = lambda a: a.reshape(1, -1).astype(_F32)
    g0, b0 = row(ln_in_g), row(ln_in_b)
    w_in_b = w_in[0].astype(_BF16)
    w_out_b = w_out[0].astype(_BF16)

    wa, wi = lru_wa[0], lru_wi[0]
    w_cat = jnp.concatenate([wa[0], wa[1], wi[0], wi[1]], axis=-1).astype(_BF16)
    blk = lambda v: v.reshape(LRU_BLOCKS, 1, LRU_BLOCK_W)
    b_cat = jnp.concatenate([blk(lru_ba[0, 0]), blk(lru_ba[0, 1]), blk(lru_bi[0, 0]), blk(lru_bi[0, 1])],
                            axis=-1).astype(_F32)
    c_decay = -LRU_C * jax.nn.softplus(-lru_lam[0].astype(_F32))

    groups = [x_prompt, x_sample]
    nblk_max = max(x.shape[1] for x in groups) // BLOCK
    band_bias, tail_bias = _bias_tables(rel_bias, attn_sink[0], nblk_max)

    _, k_m, v_m, xr_m, _ = _in_proj(meta_tokens.astype(_F32), g0, b0, w_in_b)
    tail = lambda a: jnp.zeros((N_KV_HEADS, BLOCK, HEAD_DIM), _BF16).at[:, :N_META].set(
        a.reshape(N_META, N_KV_HEADS, HEAD_DIM).transpose(1, 0, 2))
    k_tail, v_tail = tail(k_m), tail(v_m)

    xs, attns, recs = [], [], []
    for x in groups:
        bsz, s = x.shape[0], x.shape[1]
        x2 = x.reshape(bsz * s, D_MODEL)
        q, k, v, xr, yg = _in_proj(x2, g0, b0, w_in_b)
        attn = _attention(q.reshape(bsz, s, Q_COLS), k.reshape(bsz, s, KV_COLS), v.reshape(bsz, s, KV_COLS),
                          k_tail, v_tail, band_bias, tail_bias)
        rec = _rg_lru(xr.reshape(bsz, s, LRU_WIDTH), yg.reshape(bsz, s, LRU_WIDTH), xr_m,
                      conv_w[0].astype(_F32), row(conv_b[0]), w_cat, b_cat, c_decay)
        xs.append(x2)
        attns.append(attn.reshape(bsz * s, Q_COLS))
        recs.append(rec.reshape(bsz * s, LRU_WIDTH))

    rw_t = router_w[0].T.astype(_BF16)
    rb = jnp.broadcast_to(router_b[0].astype(_F32)[:, None], (N_EXPERTS, LANES))
    h1, top_e, gates_t, rank, counts = _mix(xs[0], xs[1], attns[0], attns[1], recs[0], recs[1], g0, b0,
                                            w_out_b, row(ln1_g[0]), row(ln1_b[0]), rw_t, rb)

    t = h1.shape[0]
    n_sub_max = TOP_K * t // MOE_SUB + N_EXPERTS
    n_tiles = -(-n_sub_max // MOE_SUBS_PER_TILE) + N_EXPERTS
    te, nsub, slot_idx = _dispatch_tables(top_e, rank, counts[:, 0], t, n_tiles)
    yk = _moe(te, nsub, slot_idx, h1,
              exp_w1[0], exp_b1[0].reshape(N_EXPERTS, 1, 2 * D_FF), exp_w2[0],
              exp_b2[0].reshape(N_EXPERTS, 1, D_MODEL))

    outs = []
    tok0 = 0
    for x in groups:
        bsz, s = x.shape[0], x.shape[1]
        y = _combine(h1, yk, gates_t, row(ln2_g[0]), row(ln2_b[0]), tok0, bsz * s)
        outs.append(y.reshape(bsz, s, D_MODEL))
        tok0 += bsz * s
    return tuple(outs)
```

```python
import functools
import math

import jax
import jax.numpy as jnp
from jax import lax
from jax.experimental import pallas as pl
from jax.experimental.pallas import tpu as pltpu

D_MODEL = 2048
HEAD_DIM = 128
N_HEADS = 8
N_KV_HEADS = 2
GQA_GROUP = N_HEADS // N_KV_HEADS
Q_COLS = N_HEADS * HEAD_DIM
KV_COLS = N_KV_HEADS * HEAD_DIM
LRU_WIDTH = 1024
LRU_BLOCKS = 8
LRU_BLOCK_W = LRU_WIDTH // LRU_BLOCKS
LRU_C = 8.0
IN_COLS = Q_COLS + 2 * KV_COLS + 2 * LRU_WIDTH
WINDOW = 128
BLOCK = 128
N_BUCKETS = 32
MAX_DISTANCE = 128
N_META = 16
N_EXPERTS = 32
TOP_K = 4
D_FF = D_MODEL
SWIGLU_LIMIT = 7.0
SWIGLU_ALPHA = 1.702
DEPTH = 1
DN_ALPHA = (2.0 * DEPTH) ** 0.25
LN_EPS = 1e-5
NEG_INF = -1e30

SUBLANES = 8
LANES = 128
VMEM_LIMIT_BYTES = 56 * 1024 * 1024

TOKEN_TILE = 512
LRU_CHUNK = 256
MOE_SUB = 256
MOE_SUBS_PER_TILE = 4
MOE_TILE = MOE_SUB * MOE_SUBS_PER_TILE
MOE_FF_CHUNK = 256
ROW_DMA_TOKENS = 256

_BF16 = jnp.bfloat16
_F32 = jnp.float32


def _layer_norm(x, g, b):
    mu = jnp.mean(x, axis=-1, keepdims=True)
    xc = x - mu
    var = jnp.mean(xc * xc, axis=-1, keepdims=True)
    return xc * lax.rsqrt(var + LN_EPS) * g + b


def _dot(a, b):
    return jnp.dot(a, b, preferred_element_type=_F32)


def _dot_nt(a, b):
    return lax.dot_general(a, b, (((1,), (1,)), ((), ())), preferred_element_type=_F32)


def _params(*semantics):
    return pltpu.CompilerParams(dimension_semantics=semantics, vmem_limit_bytes=VMEM_LIMIT_BYTES)


def _const_spec(shape):
    nd = len(shape)
    return pl.BlockSpec(shape, lambda *_: (0,) * nd, pipeline_mode=pl.Buffered(1))


def _in_proj_kernel(x_ref, g_ref, b_ref, w_ref, q_ref, k_ref, v_ref, xr_ref, yg_ref):
    h = _layer_norm(x_ref[...], g_ref[...], b_ref[...]).astype(_BF16)
    c0, c1, c2, c3 = Q_COLS, Q_COLS + KV_COLS, Q_COLS + 2 * KV_COLS, Q_COLS + 2 * KV_COLS + LRU_WIDTH
    q_ref[...] = _dot(h, w_ref[:, 0:c0]).astype(_BF16)
    k_ref[...] = _dot(h, w_ref[:, c0:c1]).astype(_BF16)
    v_ref[...] = _dot(h, w_ref[:, c1:c2]).astype(_BF16)
    xr_ref[...] = _dot(h, w_ref[:, c2:c3])
    yg_ref[...] = _dot(h, w_ref[:, c3:IN_COLS])


def _in_proj(x, g, b, w_bf16):
    t = x.shape[0]
    tm = min(TOKEN_TILE, t)
    assert t % tm == 0
    row = lambda n: pl.BlockSpec((tm, n), lambda i: (i, 0))
    return pl.pallas_call(
        _in_proj_kernel,
        grid=(t // tm,),
        in_specs=[row(D_MODEL), _const_spec((1, D_MODEL)), _const_spec((1, D_MODEL)),
                  _const_spec((D_MODEL, IN_COLS))],
        out_specs=[row(Q_COLS), row(KV_COLS), row(KV_COLS), row(LRU_WIDTH), row(LRU_WIDTH)],
        out_shape=[jax.ShapeDtypeStruct((t, Q_COLS), _BF16),
                   jax.ShapeDtypeStruct((t, KV_COLS), _BF16),
                   jax.ShapeDtypeStruct((t, KV_COLS), _BF16),
                   jax.ShapeDtypeStruct((t, LRU_WIDTH), _F32),
                   jax.ShapeDtypeStruct((t, LRU_WIDTH), _F32)],
        compiler_params=_params("arbitrary"),
        name="in_proj",
    )(x, g, b, w_bf16)


def _attn_kernel(q_ref, kp_ref, kc_ref, kn_ref, vp_ref, vc_ref, vn_ref, kt_ref, vt_ref,
                 bb_ref, tb_ref, o_ref, *, nblk):
    i = pl.program_id(2)
    q = q_ref[0]
    kb = jnp.concatenate([kp_ref[0], kc_ref[0], kn_ref[0]], axis=0)
    vb = jnp.concatenate([vp_ref[0], vc_ref[0], vn_ref[0]], axis=0)
    kt = kt_ref[0]
    vt = vt_ref[0]
    qi = lax.broadcasted_iota(jnp.int32, (BLOCK, 3 * BLOCK), 0)
    kj = lax.broadcasted_iota(jnp.int32, (BLOCK, 3 * BLOCK), 1)
    rel = kj - BLOCK - qi
    valid = (jnp.abs(rel) <= WINDOW) & ((kj >= BLOCK) | (i > 0)) & ((kj < 2 * BLOCK) | (i < nblk - 1))
    tail_valid = lax.broadcasted_iota(jnp.int32, (BLOCK, BLOCK), 1) <= N_META
    scale = HEAD_DIM ** -0.5
    qs = jnp.concatenate([q[:, g * HEAD_DIM:(g + 1) * HEAD_DIM] for g in range(GQA_GROUP)], axis=0)
    valid = jnp.concatenate([valid] * GQA_GROUP, axis=0)
    tail_valid = jnp.concatenate([tail_valid] * GQA_GROUP, axis=0)
    band_bias = bb_ref[...].reshape(GQA_GROUP * BLOCK, 3 * BLOCK)
    tail_bias = tb_ref[0].reshape(GQA_GROUP * BLOCK, BLOCK)
    s_b = jnp.where(valid, _dot_nt(qs, kb) * scale + band_bias, NEG_INF)
    s_t = jnp.where(tail_valid, _dot_nt(qs, kt) * scale + tail_bias, NEG_INF)
    m = jnp.maximum(jnp.max(s_b, axis=-1, keepdims=True), jnp.max(s_t, axis=-1, keepdims=True))
    e_b = jnp.exp(s_b - m)
    e_t = jnp.exp(s_t - m)
    den = jnp.sum(e_b, axis=-1, keepdims=True) + jnp.sum(e_t, axis=-1, keepdims=True)
    inv = 1.0 / den
    p_b = (e_b * inv).astype(_BF16)
    p_t = (e_t * inv).astype(_BF16)
    o = (_dot(p_b, vb) + _dot(p_t, vt)).astype(_BF16)
    for g in range(GQA_GROUP):
        o_ref[0, :, g * HEAD_DIM:(g + 1) * HEAD_DIM] = o[g * BLOCK:(g + 1) * BLOCK]


def _attention(q, k, v, k_tail, v_tail, band_bias, tail_bias):
    bsz, s = q.shape[0], q.shape[1]
    nblk = s // BLOCK
    gw = GQA_GROUP * HEAD_DIM
    kv_spec = lambda d: pl.BlockSpec(
        (1, BLOCK, HEAD_DIM), lambda b, h, i: (b, jnp.clip(i + d, 0, nblk - 1), h))
    tail_spec = pl.BlockSpec((1, BLOCK, HEAD_DIM), lambda b, h, i: (h, 0, 0))
    return pl.pallas_call(
        functools.partial(_attn_kernel, nblk=nblk),
        grid=(bsz, N_KV_HEADS, nblk),
        in_specs=[pl.BlockSpec((1, BLOCK, gw), lambda b, h, i: (b, i, h)),
                  kv_spec(-1), kv_spec(0), kv_spec(1), kv_spec(-1), kv_spec(0), kv_spec(1),
                  tail_spec, tail_spec,
                  pl.BlockSpec((GQA_GROUP, BLOCK, 3 * BLOCK), lambda b, h, i: (h, 0, 0)),
                  pl.BlockSpec((1, GQA_GROUP, BLOCK, BLOCK), lambda b, h, i: (i, h, 0, 0))],
        out_specs=pl.BlockSpec((1, BLOCK, gw), lambda b, h, i: (b, i, h)),
        out_shape=jax.ShapeDtypeStruct((bsz, s, Q_COLS), _BF16),
        compiler_params=_params("arbitrary", "arbitrary", "arbitrary"),
        name="attention",
    )(q, k, k, k, v, v, v, k_tail, v_tail, band_bias, tail_bias)


_X0 = 24


def _tile_scan(a, u, h_in, reverse):
    n = a.shape[0]
    nt = n // SUBLANES
    a3 = a.reshape(nt, SUBLANES, LANES)
    u3 = u.reshape(nt, SUBLANES, LANES)
    row = lax.broadcasted_iota(jnp.int32, (nt, SUBLANES, LANES), 1)
    for step in (1, 2, 4):
        shift = (SUBLANES - step) if reverse else step
        a_sh = pltpu.roll(a3, shift, 1)
        u_sh = pltpu.roll(u3, shift, 1)
        ok = (row < SUBLANES - step) if reverse else (row >= step)
        u3 = jnp.where(ok, a3 * u_sh + u3, u3)
        a3 = jnp.where(ok, a3 * a_sh, a3)
    out = [None] * nt
    h = h_in
    order = range(nt - 1, -1, -1) if reverse else range(nt)
    edge = 0 if reverse else SUBLANES - 1
    for j in order:
        ht = u3[j] + a3[j] * h
        out[j] = ht
        h = ht[edge:edge + 1]
    return jnp.concatenate(out, axis=0), h


def _lru_kernel(xr_ref, yg_ref, xm_ref, cw_ref, cb_ref, w_ref, b_ref, c_ref, o_ref,
                xext, hf, ab, ub, *, seq):
    nchunk = seq // LRU_CHUNK
    zeros8 = jnp.zeros((SUBLANES, LANES), _F32)
    xext[0:SUBLANES, :] = zeros8
    xext[SUBLANES:_X0, :] = xm_ref[...]
    xext[_X0:_X0 + seq, :] = xr_ref[0]
    xext[_X0 + seq:_X0 + seq + SUBLANES, :] = zeros8
    cw = cw_ref[...]
    cb = cb_ref[...]
    w = w_ref[0]
    bias = b_ref[0]
    c_f = c_ref[0:1, :]
    c_b = c_ref[1:2, :]

    def gates(r0, n):
        win = xext[pl.ds(r0 - SUBLANES, n + 2 * SUBLANES), :]
        span = n + 2 * SUBLANES
        tap = lambda d: pltpu.roll(win, (span - d) % span, 0)[SUBLANES:SUBLANES + n]
        xc = cb + cw[0:1] * tap(-2)
        xc = xc + cw[1:2] * tap(-1)
        xc = xc + cw[2:3] * win[SUBLANES:SUBLANES + n]
        xc = xc + cw[3:4] * tap(1)
        g = _dot(xc.astype(_BF16), w) + bias
        r_f = jax.nn.sigmoid(g[:, 0:LANES])
        r_b = jax.nn.sigmoid(g[:, LANES:2 * LANES])
        i_f = jax.nn.sigmoid(g[:, 2 * LANES:3 * LANES])
        i_b = jax.nn.sigmoid(g[:, 3 * LANES:4 * LANES])
        la_f = c_f * r_f
        la_b = c_b * r_b
        a_f = jnp.exp(la_f)
        a_b = jnp.exp(la_b)
        u_f = jnp.sqrt(1.0 - a_f * a_f) * (i_f * xc)
        u_b = jnp.sqrt(1.0 - a_b * a_b) * (i_b * xc)
        return a_f, u_f, a_b, u_b

    a_f, u_f, _, _ = gates(SUBLANES, N_META)
    _, h0 = _tile_scan(a_f, u_f, jnp.zeros((1, LANES), _F32), reverse=False)

    def fwd(c, h):
        t0 = pl.multiple_of(c * LRU_CHUNK, LRU_CHUNK)
        a_f, u_f, a_b, u_b = gates(_X0 + t0, LRU_CHUNK)
        hs, h = _tile_scan(a_f, u_f, h, reverse=False)
        hf[pl.ds(t0, LRU_CHUNK), :] = hs
        ab[pl.ds(t0, LRU_CHUNK), :] = a_b
        ub[pl.ds(t0, LRU_CHUNK), :] = u_b
        return h

    lax.fori_loop(0, nchunk, fwd, h0)

    def bwd(cc, h):
        c = nchunk - 1 - cc
        t0 = pl.multiple_of(c * LRU_CHUNK, LRU_CHUNK)
        hs, h = _tile_scan(ab[pl.ds(t0, LRU_CHUNK), :], ub[pl.ds(t0, LRU_CHUNK), :], h, reverse=True)
        tot = hf[pl.ds(t0, LRU_CHUNK), :] + hs
        o_ref[0, pl.ds(t0, LRU_CHUNK), :] = (tot * jax.nn.gelu(yg_ref[0, pl.ds(t0, LRU_CHUNK), :])).astype(_BF16)
        return h

    lax.fori_loop(0, nchunk, bwd, jnp.zeros((1, LANES), _F32))


def _rg_lru(xr, yg, xr_meta, conv_w, conv_b, w_cat, b_cat, c_decay):
    bsz, s = xr.shape[0], xr.shape[1]
    assert s % LRU_CHUNK == 0
    slab = pl.BlockSpec((1, s, LANES), lambda b, n: (b, 0, n))
    col = lambda r: pl.BlockSpec((r, LANES), lambda b, n: (0, n))
    return pl.pallas_call(
        functools.partial(_lru_kernel, seq=s),
        grid=(bsz, LRU_BLOCKS),
        in_specs=[slab, slab, col(N_META), col(4), col(1),
                  pl.BlockSpec((1, LANES, 4 * LANES), lambda b, n: (n, 0, 0)),
                  pl.BlockSpec((1, 1, 4 * LANES), lambda b, n: (n, 0, 0)),
                  col(2)],
        out_specs=slab,
        out_shape=jax.ShapeDtypeStruct((bsz, s, LRU_WIDTH), _BF16),
        scratch_shapes=[pltpu.VMEM((s + _X0 + SUBLANES, LANES), _F32),
                        pltpu.VMEM((s, LANES), _F32),
                        pltpu.VMEM((s, LANES), _F32),
                        pltpu.VMEM((s, LANES), _F32)],
        compiler_params=_params("arbitrary", "arbitrary"),
        name="rg_lru",
    )(xr, yg, xr_meta, conv_w, conv_b, w_cat, b_cat, c_decay)


def _mix_kernel(xp_ref, xs_ref, ap_ref, as_ref, rp_ref, rs_ref, g0_ref, b0_ref, wo_ref, g1_ref, b1_ref,
                rwt_ref, rb_ref, h1_ref, te_ref, gt_ref, rk_ref, cnt_ref, carry, *, n_prompt_tiles, tm):
    i = pl.program_id(0)

    @pl.when(i == 0)
    def _():
        carry[...] = jnp.zeros_like(carry)

    first = i < n_prompt_tiles
    x = jnp.where(first, xp_ref[...], xs_ref[...])
    attn = jnp.where(first, ap_ref[...], as_ref[...])
    rec = jnp.where(first, rp_ref[...], rs_ref[...])
    h0 = _layer_norm(x, g0_ref[...], b0_ref[...])
    mix = _dot(attn, wo_ref[0:Q_COLS, :]) + _dot(rec, wo_ref[Q_COLS:Q_COLS + LRU_WIDTH, :])
    h1 = _layer_norm(DN_ALPHA * h0 + mix, g1_ref[...], b1_ref[...])
    h1_ref[...] = h1

    logits = _dot_nt(rwt_ref[...], h1.astype(_BF16)) + rb_ref[:, 0:1]
    eidx = lax.broadcasted_iota(jnp.int32, (N_EXPERTS, tm), 0)
    work = logits
    vals, idxs, hots = [], [], []
    for _ in range(TOP_K):
        m = jnp.max(work, axis=0, keepdims=True)
        idx = jnp.min(jnp.where(work == m, eidx, N_EXPERTS), axis=0, keepdims=True)
        hot = eidx == idx
        vals.append(m)
        idxs.append(idx)
        hots.append(hot)
        work = jnp.where(hot, -jnp.inf, work)
    exps = [jnp.exp(v - vals[0]) for v in vals]
    den = exps[0] + exps[1] + exps[2] + exps[3]
    hot_all = (hots[0] | hots[1] | hots[2] | hots[3])
    hot_f = jnp.where(hot_all, 1.0, 0.0).astype(_F32)
    upper = (lax.broadcasted_iota(jnp.int32, (tm, tm), 0) < lax.broadcasted_iota(jnp.int32, (tm, tm), 1))
    before = _dot(hot_f.astype(_BF16), jnp.where(upper, 1.0, 0.0).astype(_BF16))
    base = carry[:, 0:1] + before
    zero_rows = jnp.zeros((SUBLANES - TOP_K, tm), _F32)
    ranks = [jnp.sum(jnp.where(h, base, 0.0), axis=0, keepdims=True) for h in hots]
    te_ref[...] = jnp.concatenate(idxs + [zero_rows.astype(jnp.int32)], axis=0)
    gt_ref[...] = jnp.concatenate([e / den for e in exps] + [zero_rows], axis=0)
    rk_ref[...] = jnp.concatenate(ranks + [zero_rows], axis=0).astype(jnp.int32)
    carry[...] = carry[...] + jnp.sum(hot_f, axis=1, keepdims=True)
    cnt_ref[...] = carry[...]


def _mix(x_p, x_s, attn_p, attn_s, rec_p, rec_s, g0, b0, w_out_bf16, g1, b1, rw_t, rb):
    tp, ts = x_p.shape[0], x_s.shape[0]
    t = tp + ts
    tm = min(TOKEN_TILE, tp, ts)
    assert tp % tm == 0 and ts % tm == 0
    npt = tp // tm
    first = lambda n: pl.BlockSpec((tm, n), lambda i: (jnp.minimum(i, npt - 1), 0))
    second = lambda n: pl.BlockSpec((tm, n), lambda i: (jnp.maximum(i - npt, 0), 0))
    lane_row = pl.BlockSpec((SUBLANES, tm), lambda i: (0, i))
    return pl.pallas_call(
        functools.partial(_mix_kernel, n_prompt_tiles=npt, tm=tm),
        grid=(t // tm,),
        in_specs=[first(D_MODEL), second(D_MODEL), first(Q_COLS), second(Q_COLS),
                  first(LRU_WIDTH), second(LRU_WIDTH),
                  _const_spec((1, D_MODEL)), _const_spec((1, D_MODEL)),
                  _const_spec((D_MODEL, D_MODEL)),
                  _const_spec((1, D_MODEL)), _const_spec((1, D_MODEL)),
                  _const_spec((N_EXPERTS, D_MODEL)), _const_spec((N_EXPERTS, LANES))],
        out_specs=[pl.BlockSpec((tm, D_MODEL), lambda i: (i, 0)), lane_row, lane_row, lane_row,
                   pl.BlockSpec((N_EXPERTS, LANES), lambda i: (0, 0))],
        out_shape=[jax.ShapeDtypeStruct((t, D_MODEL), _F32),
                   jax.ShapeDtypeStruct((SUBLANES, t), jnp.int32),
                   jax.ShapeDtypeStruct((SUBLANES, t), _F32),
                   jax.ShapeDtypeStruct((SUBLANES, t), jnp.int32),
                   jax.ShapeDtypeStruct((N_EXPERTS, LANES), _F32)],
        scratch_shapes=[pltpu.VMEM((N_EXPERTS, LANES), _F32)],
        compiler_params=_params("arbitrary"),
        name="mix_router",
    )(x_p, x_s, attn_p, attn_s, rec_p, rec_s, g0, b0, w_out_bf16, g1, b1, rw_t, rb)


def _dispatch_kernel(ps_ref, pn_ref, pos_ref, h1_ref, xs_hbm, zrow, sem, psem):
    i = pl.program_id(0)
    groups = h1_ref.shape[0]

    for k in range(TOP_K):
        def body(g, c):
            for u in range(SUBLANES):
                dst = pos_ref[0, k, g * SUBLANES + u]
                pltpu.make_async_copy(h1_ref.at[g, pl.ds(u, 1)], xs_hbm.at[pl.ds(dst, 1)], sem.at[0]).start()
            return c
        lax.fori_loop(0, groups, body, 0)

    @pl.when(i == 0)
    def _():
        zrow[...] = jnp.zeros_like(zrow)

        def per_expert(e, c):
            start = ps_ref[e]

            def pad_copy(j):
                return pltpu.make_async_copy(zrow.at[pl.ds(0, 1)], xs_hbm.at[pl.ds(start + j, 1)], psem.at[0])

            def go(j, c2):
                pad_copy(j).start()
                return c2

            def done(j, c2):
                pad_copy(j).wait()
                return c2

            lax.fori_loop(0, pn_ref[e], go, 0)
            lax.fori_loop(0, pn_ref[e], done, 0)
            return c

        lax.fori_loop(0, N_EXPERTS, per_expert, 0)

    for k in range(TOP_K):
        pltpu.make_async_copy(h1_ref, h1_ref, sem.at[0]).wait()


def _dispatch(pad_start, pad_n, pos3, h1, n_rows):
    t = h1.shape[0]
    dt = pos3.shape[2]
    grid_spec = pltpu.PrefetchScalarGridSpec(
        num_scalar_prefetch=2,
        grid=(t // dt,),
        in_specs=[pl.BlockSpec((1, TOP_K, dt), lambda i, ps, pn: (i, 0, 0), memory_space=pltpu.SMEM),
                  pl.BlockSpec((dt // SUBLANES, SUBLANES, D_MODEL), lambda i, ps, pn: (i, 0, 0))],
        out_specs=pl.BlockSpec(memory_space=pl.ANY),
        scratch_shapes=[pltpu.VMEM((SUBLANES, D_MODEL), _F32),
                        pltpu.SemaphoreType.DMA((1,)), pltpu.SemaphoreType.DMA((1,))],
    )
    return pl.pallas_call(
        _dispatch_kernel,
        grid_spec=grid_spec,
        out_shape=jax.ShapeDtypeStruct((n_rows, D_MODEL), _F32),
        compiler_params=_params("arbitrary"),
        name="dispatch_rows",
    )(pad_start, pad_n, pos3, h1.reshape(t // SUBLANES, SUBLANES, D_MODEL))


def _moe_kernel(te_ref, tns_ref, src_ref, x_ref, w1g_ref, w1l_ref, b1g_ref, b1l_ref, w2_ref, b2_ref,
                o_ref, xb):
    i = pl.program_id(0)
    f = pl.program_id(1)
    nsub = tns_ref[i]

    @pl.when(nsub > 0)
    def _active():
        @pl.when(f == 0)
        def _():
            def sub(s, c):
                rows = pl.ds(pl.multiple_of(s * MOE_SUB, MOE_SUB), MOE_SUB)
                xb[rows, :] = x_ref[rows, :].astype(_BF16)
                return c
            lax.fori_loop(0, nsub, sub, 0)
            o_ref[...] = jnp.broadcast_to(b2_ref[0], o_ref.shape)

        b1g = b1g_ref[0]
        b1l = b1l_ref[0]

        def compute(r0, n):
            rows = pl.ds(r0, n)
            x = xb[rows, :]
            glu = jnp.minimum(_dot(x, w1g_ref[0].astype(_BF16)) + b1g, SWIGLU_LIMIT)
            lin = jnp.clip(_dot(x, w1l_ref[0].astype(_BF16)) + b1l, -SWIGLU_LIMIT, SWIGLU_LIMIT)
            act = glu * jax.nn.sigmoid(SWIGLU_ALPHA * glu) * (lin + 1.0)
            o_ref[rows, :] += _dot(act.astype(_BF16), w2_ref[0].astype(_BF16))

        @pl.when(nsub == MOE_SUBS_PER_TILE)
        def _():
            compute(0, MOE_TILE)

        @pl.when((nsub & 2) != 0)
        def _():
            compute(0, 2 * MOE_SUB)

        @pl.when((nsub & 5) == 1)
        def _():
            compute(pl.multiple_of((nsub & 2) * MOE_SUB, MOE_SUB), MOE_SUB)


def _moe(tile_e, tile_nsub, tile_src, xs, w1, b1, w2, b2):
    n_tiles = tile_e.shape[0]
    nf = D_FF // MOE_FF_CHUNK
    last = nf - 1

    def ff(i, f, tns):
        return jnp.where(tns[i] > 0, f, last)

    grid_spec = pltpu.PrefetchScalarGridSpec(
        num_scalar_prefetch=3,
        grid=(n_tiles, nf),
        in_specs=[
            pl.BlockSpec((MOE_TILE, D_MODEL), lambda i, f, te, tns, src: (src[i], 0)),
            pl.BlockSpec((1, D_MODEL, MOE_FF_CHUNK), lambda i, f, te, tns, src: (te[i], 0, ff(i, f, tns))),
            pl.BlockSpec((1, D_MODEL, MOE_FF_CHUNK), lambda i, f, te, tns, src: (te[i], 0, nf + ff(i, f, tns))),
            pl.BlockSpec((1, 1, MOE_FF_CHUNK), lambda i, f, te, tns, src: (te[i], 0, ff(i, f, tns))),
            pl.BlockSpec((1, 1, MOE_FF_CHUNK), lambda i, f, te, tns, src: (te[i], 0, nf + ff(i, f, tns))),
            pl.BlockSpec((1, MOE_FF_CHUNK, D_MODEL), lambda i, f, te, tns, src: (te[i], ff(i, f, tns), 0)),
            pl.BlockSpec((1, 1, D_MODEL), lambda i, f, te, tns, src: (te[i], 0, 0)),
        ],
        out_specs=pl.BlockSpec((MOE_TILE, D_MODEL), lambda i, f, te, tns, src: (src[i], 0)),
        scratch_shapes=[pltpu.VMEM((MOE_TILE, D_MODEL), _BF16)],
    )
    return pl.pallas_call(
        _moe_kernel,
        grid_spec=grid_spec,
        out_shape=jax.ShapeDtypeStruct(xs.shape, _F32),
        compiler_params=_params("arbitrary", "arbitrary"),
        name="moe_experts",
    )(tile_e, tile_nsub, tile_src, xs, w1, w1, b1, b1, w2, b2)


def _combine_kernel(pos_ref, posn_ref, h1_ref, gt_ref, g_ref, b_ref, ys_hbm, o_ref, rows, sem, *, n_steps):
    i = pl.program_id(0)
    ct = h1_ref.shape[0]
    cur = lax.rem(i, 2)

    def fetch(p_ref, buf):
        for k in range(TOP_K):
            def body(g, c):
                for u in range(SUBLANES):
                    src_row = p_ref[0, k, g * SUBLANES + u]
                    pltpu.make_async_copy(ys_hbm.at[pl.ds(src_row, 1)], rows.at[buf, k, g, pl.ds(u, 1)],
                                          sem.at[buf]).start()
                return c
            lax.fori_loop(0, ct // SUBLANES, body, 0)

    @pl.when(i == 0)
    def _():
        fetch(pos_ref, 0)

    @pl.when(i + 1 < n_steps)
    def _():
        fetch(posn_ref, 1 - cur)

    for k in range(TOP_K):
        pltpu.make_async_copy(rows.at[1 - cur, k], rows.at[cur, k], sem.at[cur]).wait()

    gates = jnp.transpose(gt_ref[...])
    expert_rows = lambda k: rows[cur, k].reshape(ct, D_MODEL)
    ffn = expert_rows(0) * gates[:, 0:1]
    for k in range(1, TOP_K):
        ffn = ffn + expert_rows(k) * gates[:, k:k + 1]
    o_ref[...] = _layer_norm(DN_ALPHA * h1_ref[...] + ffn, g_ref[...], b_ref[...])


def _combine(h1, ys, pos3, gates_t, g2, b2, tok0, ntok):
    ct = pos3.shape[2]
    assert tok0 % ct == 0 and ntok % ct == 0
    off = tok0 // ct
    n_steps = ntok // ct
    return pl.pallas_call(
        functools.partial(_combine_kernel, n_steps=n_steps),
        grid=(n_steps,),
        in_specs=[pl.BlockSpec((1, TOP_K, ct), lambda i: (off + i, 0, 0), memory_space=pltpu.SMEM),
                  pl.BlockSpec((1, TOP_K, ct), lambda i: (off + jnp.minimum(i + 1, n_steps - 1), 0, 0),
                               memory_space=pltpu.SMEM),
                  pl.BlockSpec((ct, D_MODEL), lambda i: (off + i, 0)),
                  pl.BlockSpec((SUBLANES, ct), lambda i: (0, off + i)),
                  _const_spec((1, D_MODEL)), _const_spec((1, D_MODEL)),
                  pl.BlockSpec(memory_space=pl.ANY)],
        out_specs=pl.BlockSpec((ct, D_MODEL), lambda i: (i, 0)),
        out_shape=jax.ShapeDtypeStruct((ntok, D_MODEL), _F32),
        scratch_shapes=[pltpu.VMEM((2, TOP_K, ct // SUBLANES, SUBLANES, D_MODEL), _F32),
                        pltpu.SemaphoreType.DMA((2,))],
        compiler_params=_params("arbitrary"),
        name="combine_ln2",
    )(pos3, pos3, h1, gates_t, g2, b2, ys)


def _t5_bucket(rel):
    half = N_BUCKETS // 2
    exact = half // 2
    n = jnp.abs(rel)
    large = exact + (jnp.log(jnp.maximum(n, 1).astype(_F32) / exact)
                     / math.log(MAX_DISTANCE / exact) * (half - exact)).astype(jnp.int32)
    large = jnp.minimum(large, half - 1)
    return jnp.where(rel > 0, half, 0) + jnp.where(n < exact, n, large)


def _lookup(table, idx):
    out = jnp.zeros(idx.shape + table.shape[1:], table.dtype)
    expand = (...,) + (None,) * (table.ndim - 1)
    for j in range(table.shape[0]):
        out = out + jnp.where((idx == j)[expand], table[j], jnp.zeros((), table.dtype))
    return out


def _bias_tables(rel_bias, sink, nblk):
    rb = rel_bias.astype(_F32)
    qi = jnp.arange(BLOCK)
    kj = jnp.arange(3 * BLOCK)
    rel_band = kj[None, :] - BLOCK - qi[:, None]
    band = _lookup(rb, _t5_bucket(rel_band)).transpose(2, 0, 1)
    dist = jnp.arange(1, nblk * BLOCK + N_META + 1)
    by_dist = _lookup(rb, _t5_bucket(-dist))
    s = nblk * BLOCK
    meta = jnp.stack([by_dist[N_META - m - 1:N_META - m - 1 + s] for m in range(N_META)], axis=1)
    meta = meta.reshape(nblk, BLOCK, N_META, N_HEADS).transpose(0, 3, 1, 2)
    sink_col = jnp.broadcast_to(sink.astype(_F32)[None, :, None, None], (nblk, N_HEADS, BLOCK, 1))
    pad = jnp.zeros((nblk, N_HEADS, BLOCK, BLOCK - N_META - 1), _F32)
    return band, jnp.concatenate([meta, sink_col, pad], axis=-1)


def _dispatch_tables(top_e, rank, counts, t, n_tiles):
    c = counts.astype(jnp.int32)
    nb = (c + MOE_SUB - 1) // MOE_SUB
    ns = (nb + MOE_SUBS_PER_TILE - 1) // MOE_SUBS_PER_TILE
    ends = jnp.cumsum(ns)
    first_tile = ends - ns

    pos = _lookup(first_tile, top_e[:TOP_K]) * MOE_TILE + rank[:TOP_K]
    pos3 = pos.reshape(TOP_K, t // ROW_DMA_TOKENS, ROW_DMA_TOKENS).transpose(1, 0, 2)
    pad_start = first_tile * MOE_TILE + c
    pad_n = nb * MOE_SUB - c

    tid = jnp.arange(n_tiles, dtype=jnp.int32)
    n_used = ends[-1]
    used = tid < n_used
    te = jnp.minimum(jnp.searchsorted(ends, tid, side='right').astype(jnp.int32), N_EXPERTS - 1)
    last_used = jnp.maximum(n_used - 1, 0)
    te = jnp.where(used, te, te[last_used])
    jt = tid - first_tile[te]
    nsub = jnp.where(used, jnp.clip(nb[te] - jt * MOE_SUBS_PER_TILE, 0, MOE_SUBS_PER_TILE), 0)
    src = jnp.where(used, tid, last_used)
    return te, nsub.astype(jnp.int32), src.astype(jnp.int32), pos3, pad_start, pad_n


def kernel(x_prompt, x_sample, meta_tokens, ln_in_g, ln_in_b, rel_bias, w_in, conv_w, conv_b, lru_wa, lru_ba,
           lru_wi, lru_bi, lru_lam, attn_sink, w_out, ln1_g, ln1_b, router_w, router_b, exp_w1, exp_b1,
           exp_w2, exp_b2, ln2_g, ln2_b):
    assert DEPTH == 1 and w_in.shape[0] == 1
    row = lambda a: a.reshape(1, -1).astype(_F32)
    g0, b0 = row(ln_in_g), row(ln_in_b)
    w_in_b = w_in[0].astype(_BF16)
    w_out_b = w_out[0].astype(_BF16)

    wa, wi = lru_wa[0], lru_wi[0]
    w_cat = jnp.concatenate([wa[0], wa[1], wi[0], wi[1]], axis=-1).astype(_BF16)
    blk = lambda v: v.reshape(LRU_BLOCKS, 1, LRU_BLOCK_W)
    b_cat = jnp.concatenate([blk(lru_ba[0, 0]), blk(lru_ba[0, 1]), blk(lru_bi[0, 0]), blk(lru_bi[0, 1])],
                            axis=-1).astype(_F32)
    c_decay = -LRU_C * jax.nn.softplus(-lru_lam[0].astype(_F32))

    groups = [x_prompt, x_sample]
    nblk_max = max(x.shape[1] for x in groups) // BLOCK
    band_bias, tail_bias = _bias_tables(rel_bias, attn_sink[0], nblk_max)

    _, k_m, v_m, xr_m, _ = _in_proj(meta_tokens.astype(_F32), g0, b0, w_in_b)
    tail = lambda a: jnp.zeros((N_KV_HEADS, BLOCK, HEAD_DIM), _BF16).at[:, :N_META].set(
        a.reshape(N_META, N_KV_HEADS, HEAD_DIM).transpose(1, 0, 2))
    k_tail, v_tail = tail(k_m), tail(v_m)

    xs, attns, recs = [], [], []
    for x in groups:
        bsz, s = x.shape[0], x.shape[1]
        x2 = x.reshape(bsz * s, D_MODEL)
        q, k, v, xr, yg = _in_proj(x2, g0, b0, w_in_b)
        attn = _attention(q.reshape(bsz, s, Q_COLS), k.reshape(bsz, s, KV_COLS), v.reshape(bsz, s, KV_COLS),
                          k_tail, v_tail, band_bias, tail_bias)
        rec = _rg_lru(xr.reshape(bsz, s, LRU_WIDTH), yg.reshape(bsz, s, LRU_WIDTH), xr_m,
                      conv_w[0].astype(_F32), row(conv_b[0]), w_cat, b_cat, c_decay)
        xs.append(x2)
        attns.append(attn.reshape(bsz * s, Q_COLS))
        recs.append(rec.reshape(bsz * s, LRU_WIDTH))

    rw_t = router_w[0].T.astype(_BF16)
    rb = jnp.broadcast_to(router_b[0].astype(_F32)[:, None], (N_EXPERTS, LANES))
    h1, top_e, gates_t, rank, counts = _mix(xs[0], xs[1], attns[0], attns[1], recs[0], recs[1], g0, b0,
                                            w_out_b, row(ln1_g[0]), row(ln1_b[0]), rw_t, rb)

    t = h1.shape[0]
    n_tiles = TOP_K * t // MOE_TILE + N_EXPERTS
    te, nsub, src_tile, pos3, pad_start, pad_n = _dispatch_tables(top_e, rank, counts[:, 0], t, n_tiles)
    xs = _dispatch(pad_start, pad_n, pos3, h1, n_tiles * MOE_TILE)
    ys = _moe(te, nsub, src_tile, xs,
              exp_w1[0], exp_b1[0].reshape(N_EXPERTS, 1, 2 * D_FF), exp_w2[0],
              exp_b2[0].reshape(N_EXPERTS, 1, D_MODEL))

    outs = []
    tok0 = 0
    for x in groups:
        bsz, s = x.shape[0], x.shape[1]
        y = _combine(h1, ys, pos3, gates_t, row(ln2_g[0]), row(ln2_b[0]), tok0, bsz * s)
        outs.append(y.reshape(bsz, s, D_MODEL))
        tok0 += bsz * s
    return tuple(outs)
```

```python
import functools
import math

import jax
import jax.numpy as jnp
from jax import lax
from jax.experimental import pallas as pl
from jax.experimental.pallas import tpu as pltpu

D_MODEL = 2048
HEAD_DIM = 128
N_HEADS = 8
N_KV_HEADS = 2
GQA_GROUP = N_HEADS // N_KV_HEADS
Q_COLS = N_HEADS * HEAD_DIM
KV_COLS = N_KV_HEADS * HEAD_DIM
LRU_WIDTH = 1024
LRU_BLOCKS = 8
LRU_BLOCK_W = LRU_WIDTH // LRU_BLOCKS
LRU_C = 8.0
IN_COLS = Q_COLS + 2 * KV_COLS + 2 * LRU_WIDTH
WINDOW = 128
BLOCK = 128
N_BUCKETS = 32
MAX_DISTANCE = 128
N_META = 16
N_EXPERTS = 32
TOP_K = 4
D_FF = D_MODEL
SWIGLU_LIMIT = 7.0
SWIGLU_ALPHA = 1.702
DEPTH = 1
DN_ALPHA = (2.0 * DEPTH) ** 0.25
LN_EPS = 1e-5
NEG_INF = -1e30

SUBLANES = 8
LANES = 128
VMEM_LIMIT_BYTES = 56 * 1024 * 1024

TOKEN_TILE = 512
LRU_CHUNK = 256
MOE_SUB = 256
MOE_SUBS_PER_TILE = 4
MOE_TILE = MOE_SUB * MOE_SUBS_PER_TILE
MOE_FF_CHUNK = 256
ROW_DMA_TOKENS = 256

_BF16 = jnp.bfloat16
_F32 = jnp.float32


def _layer_norm(x, g, b):
    mu = jnp.mean(x, axis=-1, keepdims=True)
    xc = x - mu
    var = jnp.mean(xc * xc, axis=-1, keepdims=True)
    return xc * lax.rsqrt(var + LN_EPS) * g + b


def _dot(a, b):
    return jnp.dot(a, b, preferred_element_type=_F32)


def _dot_nt(a, b):
    return lax.dot_general(a, b, (((1,), (1,)), ((), ())), preferred_element_type=_F32)


def _params(*semantics):
    return pltpu.CompilerParams(dimension_semantics=semantics, vmem_limit_bytes=VMEM_LIMIT_BYTES)


def _const_spec(shape):
    nd = len(shape)
    return pl.BlockSpec(shape, lambda *_: (0,) * nd, pipeline_mode=pl.Buffered(1))


def _in_proj_kernel(x_ref, g_ref, b_ref, w_ref, q_ref, k_ref, v_ref, xr_ref, yg_ref):
    h = _layer_norm(x_ref[...], g_ref[...], b_ref[...]).astype(_BF16)
    c0, c1, c2, c3 = Q_COLS, Q_COLS + KV_COLS, Q_COLS + 2 * KV_COLS, Q_COLS + 2 * KV_COLS + LRU_WIDTH
    q_ref[...] = _dot(h, w_ref[:, 0:c0]).astype(_BF16)
    k_ref[...] = _dot(h, w_ref[:, c0:c1]).astype(_BF16)
    v_ref[...] = _dot(h, w_ref[:, c1:c2]).astype(_BF16)
    xr_ref[...] = _dot(h, w_ref[:, c2:c3])
    yg_ref[...] = _dot(h, w_ref[:, c3:IN_COLS])


def _in_proj(x, g, b, w_bf16):
    t = x.shape[0]
    tm = min(TOKEN_TILE, t)
    assert t % tm == 0
    row = lambda n: pl.BlockSpec((tm, n), lambda i: (i, 0))
    return pl.pallas_call(
        _in_proj_kernel,
        grid=(t // tm,),
        in_specs=[row(D_MODEL), _const_spec((1, D_MODEL)), _const_spec((1, D_MODEL)),
                  _const_spec((D_MODEL, IN_COLS))],
        out_specs=[row(Q_COLS), row(KV_COLS), row(KV_COLS), row(LRU_WIDTH), row(LRU_WIDTH)],
        out_shape=[jax.ShapeDtypeStruct((t, Q_COLS), _BF16),
                   jax.ShapeDtypeStruct((t, KV_COLS), _BF16),
                   jax.ShapeDtypeStruct((t, KV_COLS), _BF16),
                   jax.ShapeDtypeStruct((t, LRU_WIDTH), _F32),
                   jax.ShapeDtypeStruct((t, LRU_WIDTH), _F32)],
        compiler_params=_params("arbitrary"),
        name="in_proj",
    )(x, g, b, w_bf16)


def _attn_kernel(q_ref, kp_ref, kc_ref, kn_ref, vp_ref, vc_ref, vn_ref, kt_ref, vt_ref,
                 bb_ref, tb_ref, o_ref, *, nblk):
    i = pl.program_id(2)
    q = q_ref[0]
    kb = jnp.concatenate([kp_ref[0], kc_ref[0], kn_ref[0]], axis=0)
    vb = jnp.concatenate([vp_ref[0], vc_ref[0], vn_ref[0]], axis=0)
    kt = kt_ref[0]
    vt = vt_ref[0]
    qi = lax.broadcasted_iota(jnp.int32, (BLOCK, 3 * BLOCK), 0)
    kj = lax.broadcasted_iota(jnp.int32, (BLOCK, 3 * BLOCK), 1)
    rel = kj - BLOCK - qi
    valid = (jnp.abs(rel) <= WINDOW) & ((kj >= BLOCK) | (i > 0)) & ((kj < 2 * BLOCK) | (i < nblk - 1))
    tail_valid = lax.broadcasted_iota(jnp.int32, (BLOCK, BLOCK), 1) <= N_META
    scale = HEAD_DIM ** -0.5
    qs = jnp.concatenate([q[:, g * HEAD_DIM:(g + 1) * HEAD_DIM] for g in range(GQA_GROUP)], axis=0)
    valid = jnp.concatenate([valid] * GQA_GROUP, axis=0)
    tail_valid = jnp.concatenate([tail_valid] * GQA_GROUP, axis=0)
    band_bias = bb_ref[...].reshape(GQA_GROUP * BLOCK, 3 * BLOCK)
    tail_bias = tb_ref[0].reshape(GQA_GROUP * BLOCK, BLOCK)
    s_b = jnp.where(valid, _dot_nt(qs, kb) * scale + band_bias, NEG_INF)
    s_t = jnp.where(tail_valid, _dot_nt(qs, kt) * scale + tail_bias, NEG_INF)
    m = jnp.maximum(jnp.max(s_b, axis=-1, keepdims=True), jnp.max(s_t, axis=-1, keepdims=True))
    e_b = jnp.exp(s_b - m)
    e_t = jnp.exp(s_t - m)
    den = jnp.sum(e_b, axis=-1, keepdims=True) + jnp.sum(e_t, axis=-1, keepdims=True)
    inv = 1.0 / den
    p_b = (e_b * inv).astype(_BF16)
    p_t = (e_t * inv).astype(_BF16)
    o = (_dot(p_b, vb) + _dot(p_t, vt)).astype(_BF16)
    for g in range(GQA_GROUP):
        o_ref[0, :, g * HEAD_DIM:(g + 1) * HEAD_DIM] = o[g * BLOCK:(g + 1) * BLOCK]


def _attention(q, k, v, k_tail, v_tail, band_bias, tail_bias):
    bsz, s = q.shape[0], q.shape[1]
    nblk = s // BLOCK
    gw = GQA_GROUP * HEAD_DIM
    kv_spec = lambda d: pl.BlockSpec(
        (1, BLOCK, HEAD_DIM), lambda b, h, i: (b, jnp.clip(i + d, 0, nblk - 1), h))
    tail_spec = pl.BlockSpec((1, BLOCK, HEAD_DIM), lambda b, h, i: (h, 0, 0))
    return pl.pallas_call(
        functools.partial(_attn_kernel, nblk=nblk),
        grid=(bsz, N_KV_HEADS, nblk),
        in_specs=[pl.BlockSpec((1, BLOCK, gw), lambda b, h, i: (b, i, h)),
                  kv_spec(-1), kv_spec(0), kv_spec(1), kv_spec(-1), kv_spec(0), kv_spec(1),
                  tail_spec, tail_spec,
                  pl.BlockSpec((GQA_GROUP, BLOCK, 3 * BLOCK), lambda b, h, i: (h, 0, 0)),
                  pl.BlockSpec((1, GQA_GROUP, BLOCK, BLOCK), lambda b, h, i: (i, h, 0, 0))],
        out_specs=pl.BlockSpec((1, BLOCK, gw), lambda b, h, i: (b, i, h)),
        out_shape=jax.ShapeDtypeStruct((bsz, s, Q_COLS), _BF16),
        compiler_params=_params("arbitrary", "arbitrary", "arbitrary"),
        name="attention",
    )(q, k, k, k, v, v, v, k_tail, v_tail, band_bias, tail_bias)


_X0 = 24


def _tile_scan(a, u, h_in, reverse):
    n = a.shape[0]
    nt = n // SUBLANES
    a3 = a.reshape(nt, SUBLANES, LANES)
    u3 = u.reshape(nt, SUBLANES, LANES)
    row = lax.broadcasted_iota(jnp.int32, (nt, SUBLANES, LANES), 1)
    for step in (1, 2, 4):
        shift = (SUBLANES - step) if reverse else step
        a_sh = pltpu.roll(a3, shift, 1)
        u_sh = pltpu.roll(u3, shift, 1)
        ok = (row < SUBLANES - step) if reverse else (row >= step)
        u3 = jnp.where(ok, a3 * u_sh + u3, u3)
        a3 = jnp.where(ok, a3 * a_sh, a3)
    out = [None] * nt
    h = h_in
    order = range(nt - 1, -1, -1) if reverse else range(nt)
    edge = 0 if reverse else SUBLANES - 1
    for j in order:
        ht = u3[j] + a3[j] * h
        out[j] = ht
        h = ht[edge:edge + 1]
    return jnp.concatenate(out, axis=0), h


def _lru_kernel(xr_ref, yg_ref, xm_ref, cw_ref, cb_ref, w_ref, b_ref, c_ref, o_ref,
                xext, hf, ab, ub, *, seq):
    nchunk = seq // LRU_CHUNK
    zeros8 = jnp.zeros((SUBLANES, LANES), _F32)
    xext[0:SUBLANES, :] = zeros8
    xext[SUBLANES:_X0, :] = xm_ref[...]
    xext[_X0:_X0 + seq, :] = xr_ref[0]
    xext[_X0 + seq:_X0 + seq + SUBLANES, :] = zeros8
    cw = cw_ref[...]
    cb = cb_ref[...]
    w = w_ref[0]
    bias = b_ref[0]
    c_f = c_ref[0:1, :]
    c_b = c_ref[1:2, :]

    def gates(r0, n):
        win = xext[pl.ds(r0 - SUBLANES, n + 2 * SUBLANES), :]
        span = n + 2 * SUBLANES
        tap = lambda d: pltpu.roll(win, (span - d) % span, 0)[SUBLANES:SUBLANES + n]
        xc = cb + cw[0:1] * tap(-2)
        xc = xc + cw[1:2] * tap(-1)
        xc = xc + cw[2:3] * win[SUBLANES:SUBLANES + n]
        xc = xc + cw[3:4] * tap(1)
        g = _dot(xc.astype(_BF16), w) + bias
        r_f = jax.nn.sigmoid(g[:, 0:LANES])
        r_b = jax.nn.sigmoid(g[:, LANES:2 * LANES])
        i_f = jax.nn.sigmoid(g[:, 2 * LANES:3 * LANES])
        i_b = jax.nn.sigmoid(g[:, 3 * LANES:4 * LANES])
        la_f = c_f * r_f
        la_b = c_b * r_b
        a_f = jnp.exp(la_f)
        a_b = jnp.exp(la_b)
        u_f = jnp.sqrt(1.0 - a_f * a_f) * (i_f * xc)
        u_b = jnp.sqrt(1.0 - a_b * a_b) * (i_b * xc)
        return a_f, u_f, a_b, u_b

    a_f, u_f, _, _ = gates(SUBLANES, N_META)
    _, h0 = _tile_scan(a_f, u_f, jnp.zeros((1, LANES), _F32), reverse=False)

    def fwd(c, h):
        t0 = pl.multiple_of(c * LRU_CHUNK, LRU_CHUNK)
        a_f, u_f, a_b, u_b = gates(_X0 + t0, LRU_CHUNK)
        hs, h = _tile_scan(a_f, u_f, h, reverse=False)
        hf[pl.ds(t0, LRU_CHUNK), :] = hs
        ab[pl.ds(t0, LRU_CHUNK), :] = a_b
        ub[pl.ds(t0, LRU_CHUNK), :] = u_b
        return h

    lax.fori_loop(0, nchunk, fwd, h0)

    def bwd(cc, h):
        c = nchunk - 1 - cc
        t0 = pl.multiple_of(c * LRU_CHUNK, LRU_CHUNK)
        hs, h = _tile_scan(ab[pl.ds(t0, LRU_CHUNK), :], ub[pl.ds(t0, LRU_CHUNK), :], h, reverse=True)
        tot = hf[pl.ds(t0, LRU_CHUNK), :] + hs
        o_ref[0, pl.ds(t0, LRU_CHUNK), :] = (tot * jax.nn.gelu(yg_ref[0, pl.ds(t0, LRU_CHUNK), :])).astype(_BF16)
        return h

    lax.fori_loop(0, nchunk, bwd, jnp.zeros((1, LANES), _F32))


def _rg_lru(xr, yg, xr_meta, conv_w, conv_b, w_cat, b_cat, c_decay):
    bsz, s = xr.shape[0], xr.shape[1]
    assert s % LRU_CHUNK == 0
    slab = pl.BlockSpec((1, s, LANES), lambda b, n: (b, 0, n))
    col = lambda r: pl.BlockSpec((r, LANES), lambda b, n: (0, n))
    return pl.pallas_call(
        functools.partial(_lru_kernel, seq=s),
        grid=(bsz, LRU_BLOCKS),
        in_specs=[slab, slab, col(N_META), col(4), col(1),
                  pl.BlockSpec((1, LANES, 4 * LANES), lambda b, n: (n, 0, 0)),
                  pl.BlockSpec((1, 1, 4 * LANES), lambda b, n: (n, 0, 0)),
                  col(2)],
        out_specs=slab,
        out_shape=jax.ShapeDtypeStruct((bsz, s, LRU_WIDTH), _BF16),
        scratch_shapes=[pltpu.VMEM((s + _X0 + SUBLANES, LANES), _F32),
                        pltpu.VMEM((s, LANES), _F32),
                        pltpu.VMEM((s, LANES), _F32),
                        pltpu.VMEM((s, LANES), _F32)],
        compiler_params=_params("arbitrary", "arbitrary"),
        name="rg_lru",
    )(xr, yg, xr_meta, conv_w, conv_b, w_cat, b_cat, c_decay)


def _mix_kernel(xp_ref, xs_ref, ap_ref, as_ref, rp_ref, rs_ref, g0_ref, b0_ref, wo_ref, g1_ref, b1_ref,
                rwt_ref, rb_ref, h1_ref, te_ref, gt_ref, rk_ref, cnt_ref, carry, *, n_prompt_tiles, tm):
    i = pl.program_id(0)

    @pl.when(i == 0)
    def _():
        carry[...] = jnp.zeros_like(carry)

    first = i < n_prompt_tiles
    x = jnp.where(first, xp_ref[...], xs_ref[...])
    attn = jnp.where(first, ap_ref[...], as_ref[...])
    rec = jnp.where(first, rp_ref[...], rs_ref[...])
    h0 = _layer_norm(x, g0_ref[...], b0_ref[...])
    mix = _dot(attn, wo_ref[0:Q_COLS, :]) + _dot(rec, wo_ref[Q_COLS:Q_COLS + LRU_WIDTH, :])
    h1 = _layer_norm(DN_ALPHA * h0 + mix, g1_ref[...], b1_ref[...])
    h1_ref[...] = h1

    logits = _dot_nt(rwt_ref[...], h1.astype(_BF16)) + rb_ref[:, 0:1]
    eidx = lax.broadcasted_iota(jnp.int32, (N_EXPERTS, tm), 0)
    work = logits
    vals, idxs, hots = [], [], []
    for _ in range(TOP_K):
        m = jnp.max(work, axis=0, keepdims=True)
        idx = jnp.min(jnp.where(work == m, eidx, N_EXPERTS), axis=0, keepdims=True)
        hot = eidx == idx
        vals.append(m)
        idxs.append(idx)
        hots.append(hot)
        work = jnp.where(hot, -jnp.inf, work)
    exps = [jnp.exp(v - vals[0]) for v in vals]
    den = exps[0] + exps[1] + exps[2] + exps[3]
    hot_all = (hots[0] | hots[1] | hots[2] | hots[3])
    hot_f = jnp.where(hot_all, 1.0, 0.0).astype(_F32)
    upper = (lax.broadcasted_iota(jnp.int32, (tm, tm), 0) < lax.broadcasted_iota(jnp.int32, (tm, tm), 1))
    before = _dot(hot_f.astype(_BF16), jnp.where(upper, 1.0, 0.0).astype(_BF16))
    base = carry[:, 0:1] + before
    zero_rows = jnp.zeros((SUBLANES - TOP_K, tm), _F32)
    ranks = [jnp.sum(jnp.where(h, base, 0.0), axis=0, keepdims=True) for h in hots]
    te_ref[...] = jnp.concatenate(idxs + [zero_rows.astype(jnp.int32)], axis=0)
    gt_ref[...] = jnp.concatenate([e / den for e in exps] + [zero_rows], axis=0)
    rk_ref[...] = jnp.concatenate(ranks + [zero_rows], axis=0).astype(jnp.int32)
    carry[...] = carry[...] + jnp.sum(hot_f, axis=1, keepdims=True)
    cnt_ref[...] = carry[...]


def _mix(x_p, x_s, attn_p, attn_s, rec_p, rec_s, g0, b0, w_out_bf16, g1, b1, rw_t, rb):
    tp, ts = x_p.shape[0], x_s.shape[0]
    t = tp + ts
    tm = min(TOKEN_TILE, tp, ts)
    assert tp % tm == 0 and ts % tm == 0
    npt = tp // tm
    first = lambda n: pl.BlockSpec((tm, n), lambda i: (jnp.minimum(i, npt - 1), 0))
    second = lambda n: pl.BlockSpec((tm, n), lambda i: (jnp.maximum(i - npt, 0), 0))
    lane_row = pl.BlockSpec((SUBLANES, tm), lambda i: (0, i))
    return pl.pallas_call(
        functools.partial(_mix_kernel, n_prompt_tiles=npt, tm=tm),
        grid=(t // tm,),
        in_specs=[first(D_MODEL), second(D_MODEL), first(Q_COLS), second(Q_COLS),
                  first(LRU_WIDTH), second(LRU_WIDTH),
                  _const_spec((1, D_MODEL)), _const_spec((1, D_MODEL)),
                  _const_spec((D_MODEL, D_MODEL)),
                  _const_spec((1, D_MODEL)), _const_spec((1, D_MODEL)),
                  _const_spec((N_EXPERTS, D_MODEL)), _const_spec((N_EXPERTS, LANES))],
        out_specs=[pl.BlockSpec((tm, D_MODEL), lambda i: (i, 0)), lane_row, lane_row, lane_row,
                   pl.BlockSpec((N_EXPERTS, LANES), lambda i: (0, 0))],
        out_shape=[jax.ShapeDtypeStruct((t, D_MODEL), _F32),
                   jax.ShapeDtypeStruct((SUBLANES, t), jnp.int32),
                   jax.ShapeDtypeStruct((SUBLANES, t), _F32),
                   jax.ShapeDtypeStruct((SUBLANES, t), jnp.int32),
                   jax.ShapeDtypeStruct((N_EXPERTS, LANES), _F32)],
        scratch_shapes=[pltpu.VMEM((N_EXPERTS, LANES), _F32)],
        compiler_params=_params("arbitrary"),
        name="mix_router",
    )(x_p, x_s, attn_p, attn_s, rec_p, rec_s, g0, b0, w_out_bf16, g1, b1, rw_t, rb)


def _dispatch_kernel(ps_ref, pn_ref, pos_ref, h1_ref, xs_hbm, zrow, sem, psem):
    i = pl.program_id(0)
    groups = h1_ref.shape[0]
    dt = groups * SUBLANES

    for k in range(TOP_K):
        def body(g, c):
            for u in range(SUBLANES):
                dst = pos_ref[0, 0, k * dt + g * SUBLANES + u]
                pltpu.make_async_copy(h1_ref.at[g, pl.ds(u, 1)], xs_hbm.at[pl.ds(dst, 1)], sem.at[0]).start()
            return c
        lax.fori_loop(0, groups, body, 0)

    @pl.when(i == 0)
    def _():
        zrow[...] = jnp.zeros_like(zrow)

        def per_expert(e, c):
            start = ps_ref[e]

            def pad_copy(j):
                return pltpu.make_async_copy(zrow.at[pl.ds(0, 1)], xs_hbm.at[pl.ds(start + j, 1)], psem.at[0])

            def go(j, c2):
                pad_copy(j).start()
                return c2

            def done(j, c2):
                pad_copy(j).wait()
                return c2

            lax.fori_loop(0, pn_ref[e], go, 0)
            lax.fori_loop(0, pn_ref[e], done, 0)
            return c

        lax.fori_loop(0, N_EXPERTS, per_expert, 0)

    for k in range(TOP_K):
        pltpu.make_async_copy(h1_ref, h1_ref, sem.at[0]).wait()


def _dispatch(pad_start, pad_n, pos3, h1, n_rows):
    t = h1.shape[0]
    dt = pos3.shape[2] // TOP_K
    grid_spec = pltpu.PrefetchScalarGridSpec(
        num_scalar_prefetch=2,
        grid=(t // dt,),
        in_specs=[pl.BlockSpec((1, 1, TOP_K * dt), lambda i, ps, pn: (i, 0, 0), memory_space=pltpu.SMEM),
                  pl.BlockSpec((dt // SUBLANES, SUBLANES, D_MODEL), lambda i, ps, pn: (i, 0, 0))],
        out_specs=pl.BlockSpec(memory_space=pl.ANY),
        scratch_shapes=[pltpu.VMEM((SUBLANES, D_MODEL), _F32),
                        pltpu.SemaphoreType.DMA((1,)), pltpu.SemaphoreType.DMA((1,))],
    )
    return pl.pallas_call(
        _dispatch_kernel,
        grid_spec=grid_spec,
        out_shape=jax.ShapeDtypeStruct((n_rows, D_MODEL), _F32),
        compiler_params=_params("arbitrary"),
        name="dispatch_rows",
    )(pad_start, pad_n, pos3, h1.reshape(t // SUBLANES, SUBLANES, D_MODEL))


def _moe_kernel(te_ref, tns_ref, src_ref, x_ref, w1g_ref, w1l_ref, b1g_ref, b1l_ref, w2_ref, b2_ref,
                o_ref, xb):
    i = pl.program_id(0)
    f = pl.program_id(1)
    nsub = tns_ref[i]

    @pl.when(nsub > 0)
    def _active():
        @pl.when(f == 0)
        def _():
            def sub(s, c):
                rows = pl.ds(pl.multiple_of(s * MOE_SUB, MOE_SUB), MOE_SUB)
                xb[rows, :] = x_ref[rows, :].astype(_BF16)
                return c
            lax.fori_loop(0, nsub, sub, 0)
            o_ref[...] = jnp.broadcast_to(b2_ref[0], o_ref.shape)

        b1g = b1g_ref[0]
        b1l = b1l_ref[0]

        def compute(r0, n):
            rows = pl.ds(r0, n)
            x = xb[rows, :]
            glu = jnp.minimum(_dot(x, w1g_ref[0].astype(_BF16)) + b1g, SWIGLU_LIMIT)
            lin = jnp.clip(_dot(x, w1l_ref[0].astype(_BF16)) + b1l, -SWIGLU_LIMIT, SWIGLU_LIMIT)
            act = glu * jax.nn.sigmoid(SWIGLU_ALPHA * glu) * (lin + 1.0)
            o_ref[rows, :] += _dot(act.astype(_BF16), w2_ref[0].astype(_BF16))

        @pl.when(nsub == MOE_SUBS_PER_TILE)
        def _():
            compute(0, MOE_TILE)

        @pl.when((nsub & 2) != 0)
        def _():
            compute(0, 2 * MOE_SUB)

        @pl.when((nsub & 5) == 1)
        def _():
            compute(pl.multiple_of((nsub & 2) * MOE_SUB, MOE_SUB), MOE_SUB)


def _moe(tile_e, tile_nsub, tile_src, xs, w1, b1, w2, b2):
    n_tiles = tile_e.shape[0]
    nf = D_FF // MOE_FF_CHUNK
    last = nf - 1

    def ff(i, f, tns):
        return jnp.where(tns[i] > 0, f, last)

    grid_spec = pltpu.PrefetchScalarGridSpec(
        num_scalar_prefetch=3,
        grid=(n_tiles, nf),
        in_specs=[
            pl.BlockSpec((MOE_TILE, D_MODEL), lambda i, f, te, tns, src: (src[i], 0)),
            pl.BlockSpec((1, D_MODEL, MOE_FF_CHUNK), lambda i, f, te, tns, src: (te[i], 0, ff(i, f, tns))),
            pl.BlockSpec((1, D_MODEL, MOE_FF_CHUNK), lambda i, f, te, tns, src: (te[i], 0, nf + ff(i, f, tns))),
            pl.BlockSpec((1, 1, MOE_FF_CHUNK), lambda i, f, te, tns, src: (te[i], 0, ff(i, f, tns))),
            pl.BlockSpec((1, 1, MOE_FF_CHUNK), lambda i, f, te, tns, src: (te[i], 0, nf + ff(i, f, tns))),
            pl.BlockSpec((1, MOE_FF_CHUNK, D_MODEL), lambda i, f, te, tns, src: (te[i], ff(i, f, tns), 0)),
            pl.BlockSpec((1, 1, D_MODEL), lambda i, f, te, tns, src: (te[i], 0, 0)),
        ],
        out_specs=pl.BlockSpec((MOE_TILE, D_MODEL), lambda i, f, te, tns, src: (src[i], 0)),
        scratch_shapes=[pltpu.VMEM((MOE_TILE, D_MODEL), _BF16)],
    )
    return pl.pallas_call(
        _moe_kernel,
        grid_spec=grid_spec,
        out_shape=jax.ShapeDtypeStruct(xs.shape, _F32),
        compiler_params=_params("arbitrary", "arbitrary"),
        name="moe_experts",
    )(tile_e, tile_nsub, tile_src, xs, w1, w1, b1, b1, w2, b2)


def _combine_kernel(pos_ref, posn_ref, h1_ref, gt_ref, g_ref, b_ref, ys_hbm, o_ref, rows, sem, *, n_steps):
    i = pl.program_id(0)
    ct = h1_ref.shape[0]
    cur = lax.rem(i, 2)

    def fetch(p_ref, buf):
        for k in range(TOP_K):
            def body(g, c):
                for u in range(SUBLANES):
                    src_row = p_ref[0, 0, k * ct + g * SUBLANES + u]
                    pltpu.make_async_copy(ys_hbm.at[pl.ds(src_row, 1)], rows.at[buf, k, g, pl.ds(u, 1)],
                                          sem.at[buf]).start()
                return c
            lax.fori_loop(0, ct // SUBLANES, body, 0)

    @pl.when(i == 0)
    def _():
        fetch(pos_ref, 0)

    @pl.when(i + 1 < n_steps)
    def _():
        fetch(posn_ref, 1 - cur)

    for k in range(TOP_K):
        pltpu.make_async_copy(rows.at[1 - cur, k], rows.at[cur, k], sem.at[cur]).wait()

    gates = jnp.transpose(gt_ref[...])
    expert_rows = lambda k: rows[cur, k].reshape(ct, D_MODEL)
    ffn = expert_rows(0) * gates[:, 0:1]
    for k in range(1, TOP_K):
        ffn = ffn + expert_rows(k) * gates[:, k:k + 1]
    o_ref[...] = _layer_norm(DN_ALPHA * h1_ref[...] + ffn, g_ref[...], b_ref[...])


def _combine(h1, ys, pos3, gates_t, g2, b2, tok0, ntok):
    ct = pos3.shape[2] // TOP_K
    assert tok0 % ct == 0 and ntok % ct == 0
    off = tok0 // ct
    n_steps = ntok // ct
    return pl.pallas_call(
        functools.partial(_combine_kernel, n_steps=n_steps),
        grid=(n_steps,),
        in_specs=[pl.BlockSpec((1, 1, TOP_K * ct), lambda i: (off + i, 0, 0), memory_space=pltpu.SMEM),
                  pl.BlockSpec((1, 1, TOP_K * ct), lambda i: (off + jnp.minimum(i + 1, n_steps - 1), 0, 0),
                               memory_space=pltpu.SMEM),
                  pl.BlockSpec((ct, D_MODEL), lambda i: (off + i, 0)),
                  pl.BlockSpec((SUBLANES, ct), lambda i: (0, off + i)),
                  _const_spec((1, D_MODEL)), _const_spec((1, D_MODEL)),
                  pl.BlockSpec(memory_space=pl.ANY)],
        out_specs=pl.BlockSpec((ct, D_MODEL), lambda i: (i, 0)),
        out_shape=jax.ShapeDtypeStruct((ntok, D_MODEL), _F32),
        scratch_shapes=[pltpu.VMEM((2, TOP_K, ct // SUBLANES, SUBLANES, D_MODEL), _F32),
                        pltpu.SemaphoreType.DMA((2,))],
        compiler_params=_params("arbitrary"),
        name="combine_ln2",
    )(pos3, pos3, h1, gates_t, g2, b2, ys)


def _t5_bucket(rel):
    half = N_BUCKETS // 2
    exact = half // 2
    n = jnp.abs(rel)
    large = exact + (jnp.log(jnp.maximum(n, 1).astype(_F32) / exact)
                     / math.log(MAX_DISTANCE / exact) * (half - exact)).astype(jnp.int32)
    large = jnp.minimum(large, half - 1)
    return jnp.where(rel > 0, half, 0) + jnp.where(n < exact, n, large)


def _lookup(table, idx):
    out = jnp.zeros(idx.shape + table.shape[1:], table.dtype)
    expand = (...,) + (None,) * (table.ndim - 1)
    for j in range(table.shape[0]):
        out = out + jnp.where((idx == j)[expand], table[j], jnp.zeros((), table.dtype))
    return out


def _bias_tables(rel_bias, sink, nblk):
    rb = rel_bias.astype(_F32)
    qi = jnp.arange(BLOCK)
    kj = jnp.arange(3 * BLOCK)
    rel_band = kj[None, :] - BLOCK - qi[:, None]
    band = _lookup(rb, _t5_bucket(rel_band)).transpose(2, 0, 1)
    dist = jnp.arange(1, nblk * BLOCK + N_META + 1)
    by_dist = _lookup(rb, _t5_bucket(-dist))
    s = nblk * BLOCK
    meta = jnp.stack([by_dist[N_META - m - 1:N_META - m - 1 + s] for m in range(N_META)], axis=1)
    meta = meta.reshape(nblk, BLOCK, N_META, N_HEADS).transpose(0, 3, 1, 2)
    sink_col = jnp.broadcast_to(sink.astype(_F32)[None, :, None, None], (nblk, N_HEADS, BLOCK, 1))
    pad = jnp.zeros((nblk, N_HEADS, BLOCK, BLOCK - N_META - 1), _F32)
    return band, jnp.concatenate([meta, sink_col, pad], axis=-1)


def _dispatch_tables(top_e, rank, counts, t, n_tiles):
    c = counts.astype(jnp.int32)
    nb = (c + MOE_SUB - 1) // MOE_SUB
    ns = (nb + MOE_SUBS_PER_TILE - 1) // MOE_SUBS_PER_TILE
    ends = jnp.cumsum(ns)
    first_tile = ends - ns

    pos = _lookup(first_tile, top_e[:TOP_K]) * MOE_TILE + rank[:TOP_K]
    pos3 = pos.reshape(TOP_K, t // ROW_DMA_TOKENS, ROW_DMA_TOKENS).transpose(1, 0, 2).reshape(
        t // ROW_DMA_TOKENS, 1, TOP_K * ROW_DMA_TOKENS)
    pad_start = first_tile * MOE_TILE + c
    pad_n = nb * MOE_SUB - c

    tid = jnp.arange(n_tiles, dtype=jnp.int32)
    n_used = ends[-1]
    used = tid < n_used
    te = jnp.minimum(jnp.sum((tid[:, None] >= ends[None, :]).astype(jnp.int32), axis=1), N_EXPERTS - 1)
    last_used = jnp.maximum(n_used - 1, 0)
    te = jnp.where(used, te, te[last_used])
    jt = tid - first_tile[te]
    nsub = jnp.where(used, jnp.clip(nb[te] - jt * MOE_SUBS_PER_TILE, 0, MOE_SUBS_PER_TILE), 0)
    src = jnp.where(used, tid, last_used)
    return te, nsub.astype(jnp.int32), src.astype(jnp.int32), pos3, pad_start, pad_n


def kernel(x_prompt, x_sample, meta_tokens, ln_in_g, ln_in_b, rel_bias, w_in, conv_w, conv_b, lru_wa, lru_ba,
           lru_wi, lru_bi, lru_lam, attn_sink, w_out, ln1_g, ln1_b, router_w, router_b, exp_w1, exp_b1,
           exp_w2, exp_b2, ln2_g, ln2_b):
    assert DEPTH == 1 and w_in.shape[0] == 1
    row = lambda a: a.reshape(1, -1).astype(_F32)
    g0, b0 = row(ln_in_g), row(ln_in_b)
    w_in_b = w_in[0].astype(_BF16)
    w_out_b = w_out[0].astype(_BF16)

    wa, wi = lru_wa[0], lru_wi[0]
    w_cat = jnp.concatenate([wa[0], wa[1], wi[0], wi[1]], axis=-1).astype(_BF16)
    blk = lambda v: v.reshape(LRU_BLOCKS, 1, LRU_BLOCK_W)
    b_cat = jnp.concatenate([blk(lru_ba[0, 0]), blk(lru_ba[0, 1]), blk(lru_bi[0, 0]), blk(lru_bi[0, 1])],
                            axis=-1).astype(_F32)
    c_decay = -LRU_C * jax.nn.softplus(-lru_lam[0].astype(_F32))

    groups = [x_prompt, x_sample]
    nblk_max = max(x.shape[1] for x in groups) // BLOCK
    band_bias, tail_bias = _bias_tables(rel_bias, attn_sink[0], nblk_max)

    _, k_m, v_m, xr_m, _ = _in_proj(meta_tokens.astype(_F32), g0, b0, w_in_b)
    tail = lambda a: jnp.zeros((N_KV_HEADS, BLOCK, HEAD_DIM), _BF16).at[:, :N_META].set(
        a.reshape(N_META, N_KV_HEADS, HEAD_DIM).transpose(1, 0, 2))
    k_tail, v_tail = tail(k_m), tail(v_m)

    xs, attns, recs = [], [], []
    for x in groups:
        bsz, s = x.shape[0], x.shape[1]
        x2 = x.reshape(bsz * s, D_MODEL)
        q, k, v, xr, yg = _in_proj(x2, g0, b0, w_in_b)
        attn = _attention(q.reshape(bsz, s, Q_COLS), k.reshape(bsz, s, KV_COLS), v.reshape(bsz, s, KV_COLS),
                          k_tail, v_tail, band_bias, tail_bias)
        rec = _rg_lru(xr.reshape(bsz, s, LRU_WIDTH), yg.reshape(bsz, s, LRU_WIDTH), xr_m,
                      conv_w[0].astype(_F32), row(conv_b[0]), w_cat, b_cat, c_decay)
        xs.append(x2)
        attns.append(attn.reshape(bsz * s, Q_COLS))
        recs.append(rec.reshape(bsz * s, LRU_WIDTH))

    rw_t = router_w[0].T.astype(_BF16)
    rb = jnp.broadcast_to(router_b[0].astype(_F32)[:, None], (N_EXPERTS, LANES))
    h1, top_e, gates_t, rank, counts = _mix(xs[0], xs[1], attns[0], attns[1], recs[0], recs[1], g0, b0,
                                            w_out_b, row(ln1_g[0]), row(ln1_b[0]), rw_t, rb)

    t = h1.shape[0]
    n_tiles = TOP_K * t // MOE_TILE + N_EXPERTS
    te, nsub, src_tile, pos3, pad_start, pad_n = _dispatch_tables(top_e, rank, counts[:, 0], t, n_tiles)
    xs = _dispatch(pad_start, pad_n, pos3, h1, n_tiles * MOE_TILE)
    ys = _moe(te, nsub, src_tile, xs,
              exp_w1[0], exp_b1[0].reshape(N_EXPERTS, 1, 2 * D_FF), exp_w2[0],
              exp_b2[0].reshape(N_EXPERTS, 1, D_MODEL))

    outs = []
    tok0 = 0
    for x in groups:
        bsz, s = x.shape[0], x.shape[1]
        y = _combine(h1, ys, pos3, gates_t, row(ln2_g[0]), row(ln2_b[0]), tok0, bsz * s)
        outs.append(y.reshape(bsz, s, D_MODEL))
        tok0 += bsz * s
    return tuple(outs)
```

```python
import functools
import math

import jax
import jax.numpy as jnp
from jax import lax
from jax.experimental import pallas as pl
from jax.experimental.pallas import tpu as pltpu

D_MODEL = 2048
HEAD_DIM = 128
N_HEADS = 8
N_KV_HEADS = 2
GQA_GROUP = N_HEADS // N_KV_HEADS
Q_COLS = N_HEADS * HEAD_DIM
KV_COLS = N_KV_HEADS * HEAD_DIM
LRU_WIDTH = 1024
LRU_BLOCKS = 8
LRU_BLOCK_W = LRU_WIDTH // LRU_BLOCKS
LRU_C = 8.0
IN_COLS = Q_COLS + 2 * KV_COLS + 2 * LRU_WIDTH
WINDOW = 128
BLOCK = 128
N_BUCKETS = 32
MAX_DISTANCE = 128
N_META = 16
N_EXPERTS = 32
TOP_K = 4
D_FF = D_MODEL
SWIGLU_LIMIT = 7.0
SWIGLU_ALPHA = 1.702
DEPTH = 1
DN_ALPHA = (2.0 * DEPTH) ** 0.25
LN_EPS = 1e-5
NEG_INF = -1e30

SUBLANES = 8
LANES = 128
VMEM_LIMIT_BYTES = 56 * 1024 * 1024

TOKEN_TILE = 512
LRU_CHUNK = 256
MOE_SUB = 256
MOE_SUBS_PER_TILE = 4
MOE_TILE = MOE_SUB * MOE_SUBS_PER_TILE
MOE_FF_CHUNK = 256
ROW_DMA_TOKENS = 256

_BF16 = jnp.bfloat16
_F32 = jnp.float32


def _layer_norm(x, g, b):
    mu = jnp.mean(x, axis=-1, keepdims=True)
    xc = x - mu
    var = jnp.mean(xc * xc, axis=-1, keepdims=True)
    return xc * lax.rsqrt(var + LN_EPS) * g + b


def _dot(a, b):
    return jnp.dot(a, b, preferred_element_type=_F32)


def _dot_nt(a, b):
    return lax.dot_general(a, b, (((1,), (1,)), ((), ())), preferred_element_type=_F32)


def _params(*semantics):
    return pltpu.CompilerParams(dimension_semantics=semantics, vmem_limit_bytes=VMEM_LIMIT_BYTES)


def _const_spec(shape):
    nd = len(shape)
    return pl.BlockSpec(shape, lambda *_: (0,) * nd, pipeline_mode=pl.Buffered(1))


def _in_proj_kernel(x_ref, g_ref, b_ref, w_ref, q_ref, k_ref, v_ref, xr_ref, yg_ref):
    h = _layer_norm(x_ref[...], g_ref[...], b_ref[...]).astype(_BF16)
    c0, c1, c2, c3 = Q_COLS, Q_COLS + KV_COLS, Q_COLS + 2 * KV_COLS, Q_COLS + 2 * KV_COLS + LRU_WIDTH
    q_ref[...] = _dot(h, w_ref[:, 0:c0]).astype(_BF16)
    k_ref[...] = _dot(h, w_ref[:, c0:c1]).astype(_BF16)
    v_ref[...] = _dot(h, w_ref[:, c1:c2]).astype(_BF16)
    xr_ref[...] = _dot(h, w_ref[:, c2:c3])
    yg_ref[...] = _dot(h, w_ref[:, c3:IN_COLS])


def _in_proj(x, g, b, w_bf16):
    t = x.shape[0]
    tm = min(TOKEN_TILE, t)
    assert t % tm == 0
    row = lambda n: pl.BlockSpec((tm, n), lambda i: (i, 0))
    return pl.pallas_call(
        _in_proj_kernel,
        grid=(t // tm,),
        in_specs=[row(D_MODEL), _const_spec((1, D_MODEL)), _const_spec((1, D_MODEL)),
                  _const_spec((D_MODEL, IN_COLS))],
        out_specs=[row(Q_COLS), row(KV_COLS), row(KV_COLS), row(LRU_WIDTH), row(LRU_WIDTH)],
        out_shape=[jax.ShapeDtypeStruct((t, Q_COLS), _BF16),
                   jax.ShapeDtypeStruct((t, KV_COLS), _BF16),
                   jax.ShapeDtypeStruct((t, KV_COLS), _BF16),
                   jax.ShapeDtypeStruct((t, LRU_WIDTH), _F32),
                   jax.ShapeDtypeStruct((t, LRU_WIDTH), _F32)],
        compiler_params=_params("arbitrary"),
        name="in_proj",
    )(x, g, b, w_bf16)


def _attn_kernel(q_ref, kp_ref, kc_ref, kn_ref, vp_ref, vc_ref, vn_ref, kt_ref, vt_ref,
                 bb_ref, tb_ref, o_ref, *, nblk):
    i = pl.program_id(2)
    q = q_ref[0]
    kb = jnp.concatenate([kp_ref[0], kc_ref[0], kn_ref[0]], axis=0)
    vb = jnp.concatenate([vp_ref[0], vc_ref[0], vn_ref[0]], axis=0)
    kt = kt_ref[0]
    vt = vt_ref[0]
    qi = lax.broadcasted_iota(jnp.int32, (BLOCK, 3 * BLOCK), 0)
    kj = lax.broadcasted_iota(jnp.int32, (BLOCK, 3 * BLOCK), 1)
    rel = kj - BLOCK - qi
    valid = (jnp.abs(rel) <= WINDOW) & ((kj >= BLOCK) | (i > 0)) & ((kj < 2 * BLOCK) | (i < nblk - 1))
    tail_valid = lax.broadcasted_iota(jnp.int32, (BLOCK, BLOCK), 1) <= N_META
    scale = HEAD_DIM ** -0.5
    qs = jnp.concatenate([q[:, g * HEAD_DIM:(g + 1) * HEAD_DIM] for g in range(GQA_GROUP)], axis=0)
    valid = jnp.concatenate([valid] * GQA_GROUP, axis=0)
    tail_valid = jnp.concatenate([tail_valid] * GQA_GROUP, axis=0)
    band_bias = bb_ref[...].reshape(GQA_GROUP * BLOCK, 3 * BLOCK)
    tail_bias = tb_ref[0].reshape(GQA_GROUP * BLOCK, BLOCK)
    s_b = jnp.where(valid, _dot_nt(qs, kb) * scale + band_bias, NEG_INF)
    s_t = jnp.where(tail_valid, _dot_nt(qs, kt) * scale + tail_bias, NEG_INF)
    m = jnp.maximum(jnp.max(s_b, axis=-1, keepdims=True), jnp.max(s_t, axis=-1, keepdims=True))
    e_b = jnp.exp(s_b - m)
    e_t = jnp.exp(s_t - m)
    den = jnp.sum(e_b, axis=-1, keepdims=True) + jnp.sum(e_t, axis=-1, keepdims=True)
    inv = 1.0 / den
    p_b = (e_b * inv).astype(_BF16)
    p_t = (e_t * inv).astype(_BF16)
    o = (_dot(p_b, vb) + _dot(p_t, vt)).astype(_BF16)
    for g in range(GQA_GROUP):
        o_ref[0, :, g * HEAD_DIM:(g + 1) * HEAD_DIM] = o[g * BLOCK:(g + 1) * BLOCK]


def _attention(q, k, v, k_tail, v_tail, band_bias, tail_bias):
    bsz, s = q.shape[0], q.shape[1]
    nblk = s // BLOCK
    gw = GQA_GROUP * HEAD_DIM
    kv_spec = lambda d: pl.BlockSpec(
        (1, BLOCK, HEAD_DIM), lambda b, h, i: (b, jnp.clip(i + d, 0, nblk - 1), h))
    tail_spec = pl.BlockSpec((1, BLOCK, HEAD_DIM), lambda b, h, i: (h, 0, 0))
    return pl.pallas_call(
        functools.partial(_attn_kernel, nblk=nblk),
        grid=(bsz, N_KV_HEADS, nblk),
        in_specs=[pl.BlockSpec((1, BLOCK, gw), lambda b, h, i: (b, i, h)),
                  kv_spec(-1), kv_spec(0), kv_spec(1), kv_spec(-1), kv_spec(0), kv_spec(1),
                  tail_spec, tail_spec,
                  pl.BlockSpec((GQA_GROUP, BLOCK, 3 * BLOCK), lambda b, h, i: (h, 0, 0)),
                  pl.BlockSpec((1, GQA_GROUP, BLOCK, BLOCK), lambda b, h, i: (i, h, 0, 0))],
        out_specs=pl.BlockSpec((1, BLOCK, gw), lambda b, h, i: (b, i, h)),
        out_shape=jax.ShapeDtypeStruct((bsz, s, Q_COLS), _BF16),
        compiler_params=_params("arbitrary", "arbitrary", "arbitrary"),
        name="attention",
    )(q, k, k, k, v, v, v, k_tail, v_tail, band_bias, tail_bias)


_X0 = 24


def _tile_scan(a, u, h_in, reverse):
    n = a.shape[0]
    nt = n // SUBLANES
    a3 = a.reshape(nt, SUBLANES, LANES)
    u3 = u.reshape(nt, SUBLANES, LANES)
    row = lax.broadcasted_iota(jnp.int32, (nt, SUBLANES, LANES), 1)
    for step in (1, 2, 4):
        shift = (SUBLANES - step) if reverse else step
        a_sh = pltpu.roll(a3, shift, 1)
        u_sh = pltpu.roll(u3, shift, 1)
        ok = (row < SUBLANES - step) if reverse else (row >= step)
        u3 = jnp.where(ok, a3 * u_sh + u3, u3)
        a3 = jnp.where(ok, a3 * a_sh, a3)
    out = [None] * nt
    h = h_in
    order = range(nt - 1, -1, -1) if reverse else range(nt)
    edge = 0 if reverse else SUBLANES - 1
    for j in order:
        ht = u3[j] + a3[j] * h
        out[j] = ht
        h = ht[edge:edge + 1]
    return jnp.concatenate(out, axis=0), h


def _lru_kernel(xr_ref, yg_ref, xm_ref, cw_ref, cb_ref, w_ref, b_ref, c_ref, o_ref,
                xext, hf, ab, ub, *, seq):
    nchunk = seq // LRU_CHUNK
    zeros8 = jnp.zeros((SUBLANES, LANES), _F32)
    xext[0:SUBLANES, :] = zeros8
    xext[SUBLANES:_X0, :] = xm_ref[...]
    xext[_X0:_X0 + seq, :] = xr_ref[0]
    xext[_X0 + seq:_X0 + seq + SUBLANES, :] = zeros8
    cw = cw_ref[...]
    cb = cb_ref[...]
    w = w_ref[0]
    bias = b_ref[0]
    c_f = c_ref[0:1, :]
    c_b = c_ref[1:2, :]

    def gates(r0, n):
        win = xext[pl.ds(r0 - SUBLANES, n + 2 * SUBLANES), :]
        span = n + 2 * SUBLANES
        tap = lambda d: pltpu.roll(win, (span - d) % span, 0)[SUBLANES:SUBLANES + n]
        xc = cb + cw[0:1] * tap(-2)
        xc = xc + cw[1:2] * tap(-1)
        xc = xc + cw[2:3] * win[SUBLANES:SUBLANES + n]
        xc = xc + cw[3:4] * tap(1)
        g = _dot(xc.astype(_BF16), w) + bias
        r_f = jax.nn.sigmoid(g[:, 0:LANES])
        r_b = jax.nn.sigmoid(g[:, LANES:2 * LANES])
        i_f = jax.nn.sigmoid(g[:, 2 * LANES:3 * LANES])
        i_b = jax.nn.sigmoid(g[:, 3 * LANES:4 * LANES])
        la_f = c_f * r_f
        la_b = c_b * r_b
        a_f = jnp.exp(la_f)
        a_b = jnp.exp(la_b)
        u_f = jnp.sqrt(1.0 - a_f * a_f) * (i_f * xc)
        u_b = jnp.sqrt(1.0 - a_b * a_b) * (i_b * xc)
        return a_f, u_f, a_b, u_b

    a_f, u_f, _, _ = gates(SUBLANES, N_META)
    _, h0 = _tile_scan(a_f, u_f, jnp.zeros((1, LANES), _F32), reverse=False)

    def fwd(c, h):
        t0 = pl.multiple_of(c * LRU_CHUNK, LRU_CHUNK)
        a_f, u_f, a_b, u_b = gates(_X0 + t0, LRU_CHUNK)
        hs, h = _tile_scan(a_f, u_f, h, reverse=False)
        hf[pl.ds(t0, LRU_CHUNK), :] = hs
        ab[pl.ds(t0, LRU_CHUNK), :] = a_b
        ub[pl.ds(t0, LRU_CHUNK), :] = u_b
        return h

    lax.fori_loop(0, nchunk, fwd, h0)

    def bwd(cc, h):
        c = nchunk - 1 - cc
        t0 = pl.multiple_of(c * LRU_CHUNK, LRU_CHUNK)
        hs, h = _tile_scan(ab[pl.ds(t0, LRU_CHUNK), :], ub[pl.ds(t0, LRU_CHUNK), :], h, reverse=True)
        tot = hf[pl.ds(t0, LRU_CHUNK), :] + hs
        o_ref[0, pl.ds(t0, LRU_CHUNK), :] = (tot * jax.nn.gelu(yg_ref[0, pl.ds(t0, LRU_CHUNK), :])).astype(_BF16)
        return h

    lax.fori_loop(0, nchunk, bwd, jnp.zeros((1, LANES), _F32))


def _rg_lru(xr, yg, xr_meta, conv_w, conv_b, w_cat, b_cat, c_decay):
    bsz, s = xr.shape[0], xr.shape[1]
    assert s % LRU_CHUNK == 0
    slab = pl.BlockSpec((1, s, LANES), lambda b, n: (b, 0, n))
    col = lambda r: pl.BlockSpec((r, LANES), lambda b, n: (0, n))
    return pl.pallas_call(
        functools.partial(_lru_kernel, seq=s),
        grid=(bsz, LRU_BLOCKS),
        in_specs=[slab, slab, col(N_META), col(4), col(1),
                  pl.BlockSpec((1, LANES, 4 * LANES), lambda b, n: (n, 0, 0)),
                  pl.BlockSpec((1, 1, 4 * LANES), lambda b, n: (n, 0, 0)),
                  col(2)],
        out_specs=slab,
        out_shape=jax.ShapeDtypeStruct((bsz, s, LRU_WIDTH), _BF16),
        scratch_shapes=[pltpu.VMEM((s + _X0 + SUBLANES, LANES), _F32),
                        pltpu.VMEM((s, LANES), _F32),
                        pltpu.VMEM((s, LANES), _F32),
                        pltpu.VMEM((s, LANES), _F32)],
        compiler_params=_params("arbitrary", "arbitrary"),
        name="rg_lru",
    )(xr, yg, xr_meta, conv_w, conv_b, w_cat, b_cat, c_decay)


def _mix_kernel(xp_ref, xs_ref, ap_ref, as_ref, rp_ref, rs_ref, g0_ref, b0_ref, wo_ref, g1_ref, b1_ref,
                rwt_ref, rb_ref, h1_ref, te_ref, gt_ref, rk_ref, cnt_ref, carry, *, n_prompt_tiles, tm):
    i = pl.program_id(0)

    @pl.when(i == 0)
    def _():
        carry[...] = jnp.zeros_like(carry)

    first = i < n_prompt_tiles
    x = jnp.where(first, xp_ref[...], xs_ref[...])
    attn = jnp.where(first, ap_ref[...], as_ref[...])
    rec = jnp.where(first, rp_ref[...], rs_ref[...])
    h0 = _layer_norm(x, g0_ref[...], b0_ref[...])
    mix = _dot(attn, wo_ref[0:Q_COLS, :]) + _dot(rec, wo_ref[Q_COLS:Q_COLS + LRU_WIDTH, :])
    h1 = _layer_norm(DN_ALPHA * h0 + mix, g1_ref[...], b1_ref[...])
    h1_ref[...] = h1

    logits = _dot_nt(rwt_ref[...], h1.astype(_BF16)) + rb_ref[:, 0:1]
    eidx = lax.broadcasted_iota(jnp.int32, (N_EXPERTS, tm), 0)
    work = logits
    vals, idxs, hots = [], [], []
    for _ in range(TOP_K):
        m = jnp.max(work, axis=0, keepdims=True)
        idx = jnp.min(jnp.where(work == m, eidx, N_EXPERTS), axis=0, keepdims=True)
        hot = eidx == idx
        vals.append(m)
        idxs.append(idx)
        hots.append(hot)
        work = jnp.where(hot, -jnp.inf, work)
    exps = [jnp.exp(v - vals[0]) for v in vals]
    den = exps[0] + exps[1] + exps[2] + exps[3]
    hot_all = (hots[0] | hots[1] | hots[2] | hots[3])
    hot_f = jnp.where(hot_all, 1.0, 0.0).astype(_F32)
    upper = (lax.broadcasted_iota(jnp.int32, (tm, tm), 0) < lax.broadcasted_iota(jnp.int32, (tm, tm), 1))
    before = _dot(hot_f.astype(_BF16), jnp.where(upper, 1.0, 0.0).astype(_BF16))
    base = carry[:, 0:1] + before
    zero_rows = jnp.zeros((SUBLANES - TOP_K, tm), _F32)
    ranks = [jnp.sum(jnp.where(h, base, 0.0), axis=0, keepdims=True) for h in hots]
    te_ref[...] = jnp.concatenate(idxs + [zero_rows.astype(jnp.int32)], axis=0)
    gt_ref[...] = jnp.concatenate([e / den for e in exps] + [zero_rows], axis=0)
    rk_ref[...] = jnp.concatenate(ranks + [zero_rows], axis=0).astype(jnp.int32)
    carry[...] = carry[...] + jnp.sum(hot_f, axis=1, keepdims=True)
    cnt_ref[...] = carry[...]


def _mix(x_p, x_s, attn_p, attn_s, rec_p, rec_s, g0, b0, w_out_bf16, g1, b1, rw_t, rb):
    tp, ts = x_p.shape[0], x_s.shape[0]
    t = tp + ts
    tm = min(TOKEN_TILE, tp, ts)
    assert tp % tm == 0 and ts % tm == 0
    npt = tp // tm
    first = lambda n: pl.BlockSpec((tm, n), lambda i: (jnp.minimum(i, npt - 1), 0))
    second = lambda n: pl.BlockSpec((tm, n), lambda i: (jnp.maximum(i - npt, 0), 0))
    lane_row = pl.BlockSpec((SUBLANES, tm), lambda i: (0, i))
    return pl.pallas_call(
        functools.partial(_mix_kernel, n_prompt_tiles=npt, tm=tm),
        grid=(t // tm,),
        in_specs=[first(D_MODEL), second(D_MODEL), first(Q_COLS), second(Q_COLS),
                  first(LRU_WIDTH), second(LRU_WIDTH),
                  _const_spec((1, D_MODEL)), _const_spec((1, D_MODEL)),
                  _const_spec((D_MODEL, D_MODEL)),
                  _const_spec((1, D_MODEL)), _const_spec((1, D_MODEL)),
                  _const_spec((N_EXPERTS, D_MODEL)), _const_spec((N_EXPERTS, LANES))],
        out_specs=[pl.BlockSpec((tm, D_MODEL), lambda i: (i, 0)), lane_row, lane_row, lane_row,
                   pl.BlockSpec((N_EXPERTS, LANES), lambda i: (0, 0))],
        out_shape=[jax.ShapeDtypeStruct((t, D_MODEL), _F32),
                   jax.ShapeDtypeStruct((SUBLANES, t), jnp.int32),
                   jax.ShapeDtypeStruct((SUBLANES, t), _F32),
                   jax.ShapeDtypeStruct((SUBLANES, t), jnp.int32),
                   jax.ShapeDtypeStruct((N_EXPERTS, LANES), _F32)],
        scratch_shapes=[pltpu.VMEM((N_EXPERTS, LANES), _F32)],
        compiler_params=_params("arbitrary"),
        name="mix_router",
    )(x_p, x_s, attn_p, attn_s, rec_p, rec_s, g0, b0, w_out_bf16, g1, b1, rw_t, rb)


def _dispatch_kernel(ps_ref, pn_ref, pos_ref, h1_ref, xs_hbm, zrow, sem, psem):
    i = pl.program_id(0)
    groups = h1_ref.shape[0]
    dt = groups * SUBLANES

    for k in range(TOP_K):
        def body(g, c):
            for u in range(SUBLANES):
                dst = pos_ref[0, 0, k * dt + g * SUBLANES + u]
                pltpu.make_async_copy(h1_ref.at[g, pl.ds(u, 1)], xs_hbm.at[pl.ds(dst, 1)], sem.at[0]).start()
            return c
        lax.fori_loop(0, groups, body, 0)

    @pl.when(i == 0)
    def _():
        zrow[...] = jnp.zeros_like(zrow)

        def per_expert(e, c):
            start = ps_ref[e]

            def pad_copy(j):
                return pltpu.make_async_copy(zrow.at[pl.ds(0, 1)], xs_hbm.at[pl.ds(start + j, 1)], psem.at[0])

            def go(j, c2):
                pad_copy(j).start()
                return c2

            def done(j, c2):
                pad_copy(j).wait()
                return c2

            lax.fori_loop(0, pn_ref[e], go, 0)
            lax.fori_loop(0, pn_ref[e], done, 0)
            return c

        lax.fori_loop(0, N_EXPERTS, per_expert, 0)

    for k in range(TOP_K):
        pltpu.make_async_copy(h1_ref, h1_ref, sem.at[0]).wait()


def _dispatch(pad_start, pad_n, pos3, h1, n_rows):
    t = h1.shape[0]
    dt = pos3.shape[2] // TOP_K
    grid_spec = pltpu.PrefetchScalarGridSpec(
        num_scalar_prefetch=2,
        grid=(t // dt,),
        in_specs=[pl.BlockSpec((1, 1, TOP_K * dt), lambda i, ps, pn: (i, 0, 0), memory_space=pltpu.SMEM),
                  pl.BlockSpec((dt // SUBLANES, SUBLANES, D_MODEL), lambda i, ps, pn: (i, 0, 0))],
        out_specs=pl.BlockSpec(memory_space=pl.ANY),
        scratch_shapes=[pltpu.VMEM((SUBLANES, D_MODEL), _F32),
                        pltpu.SemaphoreType.DMA((1,)), pltpu.SemaphoreType.DMA((1,))],
    )
    return pl.pallas_call(
        _dispatch_kernel,
        grid_spec=grid_spec,
        out_shape=jax.ShapeDtypeStruct((n_rows, D_MODEL), _F32),
        compiler_params=_params("arbitrary"),
        name="dispatch_rows",
    )(pad_start, pad_n, pos3, h1.reshape(t // SUBLANES, SUBLANES, D_MODEL))


def _moe_kernel(te_ref, tns_ref, src_ref, x_ref, w1g_ref, w1l_ref, b1g_ref, b1l_ref, w2_ref, b2_ref,
                o_ref, xb):
    i = pl.program_id(0)
    f = pl.program_id(1)
    nsub = tns_ref[i]

    @pl.when(nsub > 0)
    def _active():
        full = nsub == MOE_SUBS_PER_TILE
        b1g = b1g_ref[0]
        b1l = b1l_ref[0]

        def compute(r0, n, first):
            rows = pl.ds(r0, n)
            if first:
                x = x_ref[rows, :].astype(_BF16)
                xb[rows, :] = x
            else:
                x = xb[rows, :]
            glu = jnp.minimum(_dot(x, w1g_ref[0].astype(_BF16)) + b1g, SWIGLU_LIMIT)
            lin = jnp.clip(_dot(x, w1l_ref[0].astype(_BF16)) + b1l, -SWIGLU_LIMIT, SWIGLU_LIMIT)
            act = glu * jax.nn.sigmoid(SWIGLU_ALPHA * glu) * (lin + 1.0)
            part = _dot(act.astype(_BF16), w2_ref[0].astype(_BF16))
            if first:
                o_ref[rows, :] = part + b2_ref[0]
            else:
                o_ref[rows, :] += part

        @pl.when(full & (f == 0))
        def _():
            compute(0, MOE_TILE, True)

        @pl.when(full & (f > 0))
        def _():
            compute(0, MOE_TILE, False)

        @pl.when(jnp.logical_not(full) & (f == 0))
        def _():
            def sub(s, c):
                rows = pl.ds(pl.multiple_of(s * MOE_SUB, MOE_SUB), MOE_SUB)
                xb[rows, :] = x_ref[rows, :].astype(_BF16)
                return c
            lax.fori_loop(0, nsub, sub, 0)
            o_ref[...] = jnp.broadcast_to(b2_ref[0], o_ref.shape)

        @pl.when((nsub & 2) != 0)
        def _():
            compute(0, 2 * MOE_SUB, False)

        @pl.when((nsub & 5) == 1)
        def _():
            compute(pl.multiple_of((nsub & 2) * MOE_SUB, MOE_SUB), MOE_SUB, False)


def _moe(tile_e, tile_nsub, tile_src, xs, w1, b1, w2, b2):
    n_tiles = tile_e.shape[0]
    nf = D_FF // MOE_FF_CHUNK
    last = nf - 1

    def ff(i, f, tns):
        return jnp.where(tns[i] > 0, f, last)

    grid_spec = pltpu.PrefetchScalarGridSpec(
        num_scalar_prefetch=3,
        grid=(n_tiles, nf),
        in_specs=[
            pl.BlockSpec((MOE_TILE, D_MODEL), lambda i, f, te, tns, src: (src[i], 0)),
            pl.BlockSpec((1, D_MODEL, MOE_FF_CHUNK), lambda i, f, te, tns, src: (te[i], 0, ff(i, f, tns))),
            pl.BlockSpec((1, D_MODEL, MOE_FF_CHUNK), lambda i, f, te, tns, src: (te[i], 0, nf + ff(i, f, tns))),
            pl.BlockSpec((1, 1, MOE_FF_CHUNK), lambda i, f, te, tns, src: (te[i], 0, ff(i, f, tns))),
            pl.BlockSpec((1, 1, MOE_FF_CHUNK), lambda i, f, te, tns, src: (te[i], 0, nf + ff(i, f, tns))),
            pl.BlockSpec((1, MOE_FF_CHUNK, D_MODEL), lambda i, f, te, tns, src: (te[i], ff(i, f, tns), 0)),
            pl.BlockSpec((1, 1, D_MODEL), lambda i, f, te, tns, src: (te[i], 0, 0)),
        ],
        out_specs=pl.BlockSpec((MOE_TILE, D_MODEL), lambda i, f, te, tns, src: (src[i], 0)),
        scratch_shapes=[pltpu.VMEM((MOE_TILE, D_MODEL), _BF16)],
    )
    return pl.pallas_call(
        _moe_kernel,
        grid_spec=grid_spec,
        out_shape=jax.ShapeDtypeStruct(xs.shape, _F32),
        compiler_params=_params("arbitrary", "arbitrary"),
        name="moe_experts",
    )(tile_e, tile_nsub, tile_src, xs, w1, w1, b1, b1, w2, b2)


def _combine_kernel(pos_ref, posn_ref, h1_ref, gt_ref, g_ref, b_ref, ys_hbm, o_ref, rows, sem, *, n_steps):
    i = pl.program_id(0)
    ct = h1_ref.shape[0]
    cur = lax.rem(i, 2)

    def row_copy(p_ref, buf, k, g, u):
        src_row = p_ref[0, 0, k * ct + g * SUBLANES + u]
        return pltpu.make_async_copy(ys_hbm.at[pl.ds(src_row, 1)], rows.at[buf, k, g, pl.ds(u, 1)], sem.at[buf])

    def fetch(p_ref, buf, slots):
        for k in slots:
            def body(g, c):
                for u in range(SUBLANES):
                    row_copy(p_ref, buf, k, g, u).start()
                return c
            lax.fori_loop(0, ct // SUBLANES, body, 0)

    def wait_rows(buf):
        for k in range(TOP_K):
            pltpu.make_async_copy(rows.at[1 - buf, k], rows.at[buf, k], sem.at[buf]).wait()

    @pl.when(i == 0)
    def _():
        fetch(pos_ref, 0, range(TOP_K))

    fetch(posn_ref, 1 - cur, range(TOP_K - 1))
    wait_rows(cur)

    gates = jnp.transpose(gt_ref[...])
    expert_rows = lambda k: rows[cur, k].reshape(ct, D_MODEL)
    ffn = expert_rows(0) * gates[:, 0:1]
    for k in range(1, TOP_K):
        ffn = ffn + expert_rows(k) * gates[:, k:k + 1]
    for g in range(ct // SUBLANES):
        for u in range(SUBLANES):
            row_copy(posn_ref, 1 - cur, TOP_K - 1, g, u).start()
    o_ref[...] = _layer_norm(DN_ALPHA * h1_ref[...] + ffn, g_ref[...], b_ref[...])

    @pl.when(i == n_steps - 1)
    def _():
        wait_rows(1 - cur)


def _combine(h1, ys, pos3, gates_t, g2, b2, tok0, ntok):
    ct = pos3.shape[2] // TOP_K
    assert tok0 % ct == 0 and ntok % ct == 0
    off = tok0 // ct
    n_steps = ntok // ct
    return pl.pallas_call(
        functools.partial(_combine_kernel, n_steps=n_steps),
        grid=(n_steps,),
        in_specs=[pl.BlockSpec((1, 1, TOP_K * ct), lambda i: (off + i, 0, 0), memory_space=pltpu.SMEM),
                  pl.BlockSpec((1, 1, TOP_K * ct), lambda i: (off + jnp.minimum(i + 1, n_steps - 1), 0, 0),
                               memory_space=pltpu.SMEM),
                  pl.BlockSpec((ct, D_MODEL), lambda i: (off + i, 0)),
                  pl.BlockSpec((SUBLANES, ct), lambda i: (0, off + i)),
                  _const_spec((1, D_MODEL)), _const_spec((1, D_MODEL)),
                  pl.BlockSpec(memory_space=pl.ANY)],
        out_specs=pl.BlockSpec((ct, D_MODEL), lambda i: (i, 0)),
        out_shape=jax.ShapeDtypeStruct((ntok, D_MODEL), _F32),
        scratch_shapes=[pltpu.VMEM((2, TOP_K, ct // SUBLANES, SUBLANES, D_MODEL), _F32),
                        pltpu.SemaphoreType.DMA((2,))],
        compiler_params=_params("arbitrary"),
        name="combine_ln2",
    )(pos3, pos3, h1, gates_t, g2, b2, ys)


def _t5_bucket(rel):
    half = N_BUCKETS // 2
    exact = half // 2
    n = jnp.abs(rel)
    large = exact + (jnp.log(jnp.maximum(n, 1).astype(_F32) / exact)
                     / math.log(MAX_DISTANCE / exact) * (half - exact)).astype(jnp.int32)
    large = jnp.minimum(large, half - 1)
    return jnp.where(rel > 0, half, 0) + jnp.where(n < exact, n, large)


def _lookup(table, idx):
    out = jnp.zeros(idx.shape + table.shape[1:], table.dtype)
    expand = (...,) + (None,) * (table.ndim - 1)
    for j in range(table.shape[0]):
        out = out + jnp.where((idx == j)[expand], table[j], jnp.zeros((), table.dtype))
    return out


def _bias_tables(rel_bias, sink, nblk):
    rb = rel_bias.astype(_F32)
    qi = jnp.arange(BLOCK)
    kj = jnp.arange(3 * BLOCK)
    rel_band = kj[None, :] - BLOCK - qi[:, None]
    band = _lookup(rb, _t5_bucket(rel_band)).transpose(2, 0, 1)
    dist = jnp.arange(1, nblk * BLOCK + N_META + 1)
    by_dist = _lookup(rb, _t5_bucket(-dist))
    s = nblk * BLOCK
    meta = jnp.stack([by_dist[N_META - m - 1:N_META - m - 1 + s] for m in range(N_META)], axis=1)
    meta = meta.reshape(nblk, BLOCK, N_META, N_HEADS).transpose(0, 3, 1, 2)
    sink_col = jnp.broadcast_to(sink.astype(_F32)[None, :, None, None], (nblk, N_HEADS, BLOCK, 1))
    pad = jnp.zeros((nblk, N_HEADS, BLOCK, BLOCK - N_META - 1), _F32)
    return band, jnp.concatenate([meta, sink_col, pad], axis=-1)


def _dispatch_tables(top_e, rank, counts, t, n_tiles):
    c = counts.astype(jnp.int32)
    nb = (c + MOE_SUB - 1) // MOE_SUB
    ns = (nb + MOE_SUBS_PER_TILE - 1) // MOE_SUBS_PER_TILE
    ends = jnp.cumsum(ns)
    first_tile = ends - ns

    pos = _lookup(first_tile, top_e[:TOP_K]) * MOE_TILE + rank[:TOP_K]
    pos3 = pos.reshape(TOP_K, t // ROW_DMA_TOKENS, ROW_DMA_TOKENS).transpose(1, 0, 2).reshape(
        t // ROW_DMA_TOKENS, 1, TOP_K * ROW_DMA_TOKENS)
    pad_start = first_tile * MOE_TILE + c
    pad_n = nb * MOE_SUB - c

    tid = jnp.arange(n_tiles, dtype=jnp.int32)
    n_used = ends[-1]
    used = tid < n_used
    te = jnp.minimum(jnp.sum((tid[:, None] >= ends[None, :]).astype(jnp.int32), axis=1), N_EXPERTS - 1)
    last_used = jnp.maximum(n_used - 1, 0)
    te = jnp.where(used, te, te[last_used])
    jt = tid - first_tile[te]
    nsub = jnp.where(used, jnp.clip(nb[te] - jt * MOE_SUBS_PER_TILE, 0, MOE_SUBS_PER_TILE), 0)
    src = jnp.where(used, tid, last_used)
    return te, nsub.astype(jnp.int32), src.astype(jnp.int32), pos3, pad_start, pad_n


def kernel(x_prompt, x_sample, meta_tokens, ln_in_g, ln_in_b, rel_bias, w_in, conv_w, conv_b, lru_wa, lru_ba,
           lru_wi, lru_bi, lru_lam, attn_sink, w_out, ln1_g, ln1_b, router_w, router_b, exp_w1, exp_b1,
           exp_w2, exp_b2, ln2_g, ln2_b):
    assert DEPTH == 1 and w_in.shape[0] == 1
    row = lambda a: a.reshape(1, -1).astype(_F32)
    g0, b0 = row(ln_in_g), row(ln_in_b)
    w_in_b = w_in[0].astype(_BF16)
    w_out_b = w_out[0].astype(_BF16)

    wa, wi = lru_wa[0], lru_wi[0]
    w_cat = jnp.concatenate([wa[0], wa[1], wi[0], wi[1]], axis=-1).astype(_BF16)
    blk = lambda v: v.reshape(LRU_BLOCKS, 1, LRU_BLOCK_W)
    b_cat = jnp.concatenate([blk(lru_ba[0, 0]), blk(lru_ba[0, 1]), blk(lru_bi[0, 0]), blk(lru_bi[0, 1])],
                            axis=-1).astype(_F32)
    c_decay = -LRU_C * jax.nn.softplus(-lru_lam[0].astype(_F32))

    groups = [x_prompt, x_sample]
    nblk_max = max(x.shape[1] for x in groups) // BLOCK
    band_bias, tail_bias = _bias_tables(rel_bias, attn_sink[0], nblk_max)

    _, k_m, v_m, xr_m, _ = _in_proj(meta_tokens.astype(_F32), g0, b0, w_in_b)
    tail = lambda a: jnp.zeros((N_KV_HEADS, BLOCK, HEAD_DIM), _BF16).at[:, :N_META].set(
        a.reshape(N_META, N_KV_HEADS, HEAD_DIM).transpose(1, 0, 2))
    k_tail, v_tail = tail(k_m), tail(v_m)

    xs, attns, recs = [], [], []
    for x in groups:
        bsz, s = x.shape[0], x.shape[1]
        x2 = x.reshape(bsz * s, D_MODEL)
        q, k, v, xr, yg = _in_proj(x2, g0, b0, w_in_b)
        attn = _attention(q.reshape(bsz, s, Q_COLS), k.reshape(bsz, s, KV_COLS), v.reshape(bsz, s, KV_COLS),
                          k_tail, v_tail, band_bias, tail_bias)
        rec = _rg_lru(xr.reshape(bsz, s, LRU_WIDTH), yg.reshape(bsz, s, LRU_WIDTH), xr_m,
                      conv_w[0].astype(_F32), row(conv_b[0]), w_cat, b_cat, c_decay)
        xs.append(x2)
        attns.append(attn.reshape(bsz * s, Q_COLS))
        recs.append(rec.reshape(bsz * s, LRU_WIDTH))

    rw_t = router_w[0].T.astype(_BF16)
    rb = jnp.broadcast_to(router_b[0].astype(_F32)[:, None], (N_EXPERTS, LANES))
    h1, top_e, gates_t, rank, counts = _mix(xs[0], xs[1], attns[0], attns[1], recs[0], recs[1], g0, b0,
                                            w_out_b, row(ln1_g[0]), row(ln1_b[0]), rw_t, rb)

    t = h1.shape[0]
    n_tiles = TOP_K * t // MOE_TILE + N_EXPERTS
    te, nsub, src_tile, pos3, pad_start, pad_n = _dispatch_tables(top_e, rank, counts[:, 0], t, n_tiles)
    xs = _dispatch(pad_start, pad_n, pos3, h1, n_tiles * MOE_TILE)
    ys = _moe(te, nsub, src_tile, xs,
              exp_w1[0], exp_b1[0].reshape(N_EXPERTS, 1, 2 * D_FF), exp_w2[0],
              exp_b2[0].reshape(N_EXPERTS, 1, D_MODEL))

    outs = []
    tok0 = 0
    for x in groups:
        bsz, s = x.shape[0], x.shape[1]
        y = _combine(h1, ys, pos3, gates_t, row(ln2_g[0]), row(ln2_b[0]), tok0, bsz * s)
        outs.append(y.reshape(bsz, s, D_MODEL))
        tok0 += bsz * s
    return tuple(outs)
```

```python
import functools
import math

import jax
import jax.numpy as jnp
from jax import lax
from jax.experimental import pallas as pl
from jax.experimental.pallas import tpu as pltpu

D_MODEL = 2048
HEAD_DIM = 128
N_HEADS = 8
N_KV_HEADS = 2
GQA_GROUP = N_HEADS // N_KV_HEADS
Q_COLS = N_HEADS * HEAD_DIM
KV_COLS = N_KV_HEADS * HEAD_DIM
LRU_WIDTH = 1024
LRU_BLOCKS = 8
LRU_BLOCK_W = LRU_WIDTH // LRU_BLOCKS
LRU_C = 8.0
IN_COLS = Q_COLS + 2 * KV_COLS + 2 * LRU_WIDTH
WINDOW = 128
BLOCK = 128
N_BUCKETS = 32
MAX_DISTANCE = 128
N_META = 16
N_EXPERTS = 32
TOP_K = 4
D_FF = D_MODEL
SWIGLU_LIMIT = 7.0
SWIGLU_ALPHA = 1.702
DEPTH = 1
DN_ALPHA = (2.0 * DEPTH) ** 0.25
LN_EPS = 1e-5
NEG_INF = -1e30

SUBLANES = 8
LANES = 128
VMEM_LIMIT_BYTES = 56 * 1024 * 1024

TOKEN_TILE = 512
LRU_CHUNK = 256
MOE_SUB = 256
MOE_SUBS_PER_TILE = 4
MOE_TILE = MOE_SUB * MOE_SUBS_PER_TILE
MOE_FF_CHUNK = 256
ROW_DMA_TOKENS = 256

_BF16 = jnp.bfloat16
_F32 = jnp.float32


def _layer_norm(x, g, b):
    mu = jnp.mean(x, axis=-1, keepdims=True)
    xc = x - mu
    var = jnp.mean(xc * xc, axis=-1, keepdims=True)
    return xc * lax.rsqrt(var + LN_EPS) * g + b


def _dot(a, b):
    return jnp.dot(a, b, preferred_element_type=_F32)


def _dot_nt(a, b):
    return lax.dot_general(a, b, (((1,), (1,)), ((), ())), preferred_element_type=_F32)


def _params(*semantics):
    return pltpu.CompilerParams(dimension_semantics=semantics, vmem_limit_bytes=VMEM_LIMIT_BYTES)


def _const_spec(shape):
    nd = len(shape)
    return pl.BlockSpec(shape, lambda *_: (0,) * nd, pipeline_mode=pl.Buffered(1))


def _in_proj_kernel(x_ref, g_ref, b_ref, w_ref, q_ref, k_ref, v_ref, xr_ref, yg_ref):
    h = _layer_norm(x_ref[...], g_ref[...], b_ref[...]).astype(_BF16)
    c0, c1, c2, c3 = Q_COLS, Q_COLS + KV_COLS, Q_COLS + 2 * KV_COLS, Q_COLS + 2 * KV_COLS + LRU_WIDTH
    q_ref[...] = _dot(h, w_ref[:, 0:c0]).astype(_BF16)
    k_ref[...] = _dot(h, w_ref[:, c0:c1]).astype(_BF16)
    v_ref[...] = _dot(h, w_ref[:, c1:c2]).astype(_BF16)
    xr_ref[...] = _dot(h, w_ref[:, c2:c3])
    yg_ref[...] = _dot(h, w_ref[:, c3:IN_COLS])


def _in_proj(x, g, b, w_bf16):
    t = x.shape[0]
    tm = min(TOKEN_TILE, t)
    assert t % tm == 0
    row = lambda n: pl.BlockSpec((tm, n), lambda i: (i, 0))
    return pl.pallas_call(
        _in_proj_kernel,
        grid=(t // tm,),
        in_specs=[row(D_MODEL), _const_spec((1, D_MODEL)), _const_spec((1, D_MODEL)),
                  _const_spec((D_MODEL, IN_COLS))],
        out_specs=[row(Q_COLS), row(KV_COLS), row(KV_COLS), row(LRU_WIDTH), row(LRU_WIDTH)],
        out_shape=[jax.ShapeDtypeStruct((t, Q_COLS), _BF16),
                   jax.ShapeDtypeStruct((t, KV_COLS), _BF16),
                   jax.ShapeDtypeStruct((t, KV_COLS), _BF16),
                   jax.ShapeDtypeStruct((t, LRU_WIDTH), _F32),
                   jax.ShapeDtypeStruct((t, LRU_WIDTH), _F32)],
        compiler_params=_params("arbitrary"),
        name="in_proj",
    )(x, g, b, w_bf16)


def _attn_kernel(q_ref, kp_ref, kc_ref, kn_ref, vp_ref, vc_ref, vn_ref, kt_ref, vt_ref,
                 bb_ref, tb_ref, o_ref, *, nblk):
    i = pl.program_id(2)
    q = q_ref[0]
    kb = jnp.concatenate([kp_ref[0], kc_ref[0], kn_ref[0]], axis=0)
    vb = jnp.concatenate([vp_ref[0], vc_ref[0], vn_ref[0]], axis=0)
    kt = kt_ref[0]
    vt = vt_ref[0]
    kj = lax.broadcasted_iota(jnp.int32, (1, 3 * BLOCK), 1)
    valid = ((kj >= BLOCK) | (i > 0)) & ((kj < 2 * BLOCK) | (i < nblk - 1))
    tail_valid = lax.broadcasted_iota(jnp.int32, (1, BLOCK), 1) <= N_META
    scale = HEAD_DIM ** -0.5
    qs = jnp.concatenate([q[:, g * HEAD_DIM:(g + 1) * HEAD_DIM] for g in range(GQA_GROUP)], axis=0)
    band_bias = bb_ref[...].reshape(GQA_GROUP * BLOCK, 3 * BLOCK)
    tail_bias = tb_ref[0].reshape(GQA_GROUP * BLOCK, BLOCK)
    s_b = jnp.where(valid, _dot_nt(qs, kb) * scale + band_bias, NEG_INF)
    s_t = jnp.where(tail_valid, _dot_nt(qs, kt) * scale + tail_bias, NEG_INF)
    m = jnp.maximum(jnp.max(s_b, axis=-1, keepdims=True), jnp.max(s_t, axis=-1, keepdims=True))
    e_b = jnp.exp(s_b - m)
    e_t = jnp.exp(s_t - m)
    den = jnp.sum(e_b, axis=-1, keepdims=True) + jnp.sum(e_t, axis=-1, keepdims=True)
    inv = 1.0 / den
    p_b = (e_b * inv).astype(_BF16)
    p_t = (e_t * inv).astype(_BF16)
    o = (_dot(p_b, vb) + _dot(p_t, vt)).astype(_BF16)
    for g in range(GQA_GROUP):
        o_ref[0, :, g * HEAD_DIM:(g + 1) * HEAD_DIM] = o[g * BLOCK:(g + 1) * BLOCK]


def _attention(q, k, v, k_tail, v_tail, band_bias, tail_bias):
    bsz, s = q.shape[0], q.shape[1]
    nblk = s // BLOCK
    gw = GQA_GROUP * HEAD_DIM
    kv_spec = lambda d: pl.BlockSpec(
        (1, BLOCK, HEAD_DIM), lambda b, h, i: (b, jnp.clip(i + d, 0, nblk - 1), h))
    tail_spec = pl.BlockSpec((1, BLOCK, HEAD_DIM), lambda b, h, i: (h, 0, 0))
    return pl.pallas_call(
        functools.partial(_attn_kernel, nblk=nblk),
        grid=(bsz, N_KV_HEADS, nblk),
        in_specs=[pl.BlockSpec((1, BLOCK, gw), lambda b, h, i: (b, i, h)),
                  kv_spec(-1), kv_spec(0), kv_spec(1), kv_spec(-1), kv_spec(0), kv_spec(1),
                  tail_spec, tail_spec,
                  pl.BlockSpec((GQA_GROUP, BLOCK, 3 * BLOCK), lambda b, h, i: (h, 0, 0)),
                  pl.BlockSpec((1, GQA_GROUP, BLOCK, BLOCK), lambda b, h, i: (i, h, 0, 0))],
        out_specs=pl.BlockSpec((1, BLOCK, gw), lambda b, h, i: (b, i, h)),
        out_shape=jax.ShapeDtypeStruct((bsz, s, Q_COLS), _BF16),
        compiler_params=_params("arbitrary", "arbitrary", "arbitrary"),
        name="attention",
    )(q, k, k, k, v, v, v, k_tail, v_tail, band_bias, tail_bias)


_X0 = 24


def _tile_scan(a, u, h_in, reverse):
    n = a.shape[0]
    nt = n // SUBLANES
    a3 = a.reshape(nt, SUBLANES, LANES)
    u3 = u.reshape(nt, SUBLANES, LANES)
    row = lax.broadcasted_iota(jnp.int32, (nt, SUBLANES, LANES), 1)
    for step in (1, 2, 4):
        shift = (SUBLANES - step) if reverse else step
        a_sh = pltpu.roll(a3, shift, 1)
        u_sh = pltpu.roll(u3, shift, 1)
        ok = (row < SUBLANES - step) if reverse else (row >= step)
        u3 = jnp.where(ok, a3 * u_sh + u3, u3)
        a3 = jnp.where(ok, a3 * a_sh, a3)
    out = [None] * nt
    h = h_in
    order = range(nt - 1, -1, -1) if reverse else range(nt)
    edge = 0 if reverse else SUBLANES - 1
    for j in order:
        ht = u3[j] + a3[j] * h
        out[j] = ht
        h = ht[edge:edge + 1]
    return jnp.concatenate(out, axis=0), h


def _lru_kernel(xr_ref, yg_ref, xm_ref, cw_ref, cb_ref, w_ref, b_ref, c_ref, o_ref,
                xext, hf, ab, ub, *, seq):
    nchunk = seq // LRU_CHUNK
    zeros8 = jnp.zeros((SUBLANES, LANES), _F32)
    xext[0:SUBLANES, :] = zeros8
    xext[SUBLANES:_X0, :] = xm_ref[...]
    xext[_X0:_X0 + seq, :] = xr_ref[0]
    xext[_X0 + seq:_X0 + seq + SUBLANES, :] = zeros8
    cw = cw_ref[...]
    cb = cb_ref[...]
    w = w_ref[0]
    bias = b_ref[0]
    c_f = c_ref[0:1, :]
    c_b = c_ref[1:2, :]

    def gates(r0, n):
        win = xext[pl.ds(r0 - SUBLANES, n + 2 * SUBLANES), :]
        span = n + 2 * SUBLANES
        tap = lambda d: pltpu.roll(win, (span - d) % span, 0)[SUBLANES:SUBLANES + n]
        xc = cb + cw[0:1] * tap(-2)
        xc = xc + cw[1:2] * tap(-1)
        xc = xc + cw[2:3] * win[SUBLANES:SUBLANES + n]
        xc = xc + cw[3:4] * tap(1)
        g = _dot(xc.astype(_BF16), w) + bias
        r_f = jax.nn.sigmoid(g[:, 0:LANES])
        r_b = jax.nn.sigmoid(g[:, LANES:2 * LANES])
        i_f = jax.nn.sigmoid(g[:, 2 * LANES:3 * LANES])
        i_b = jax.nn.sigmoid(g[:, 3 * LANES:4 * LANES])
        la_f = c_f * r_f
        la_b = c_b * r_b
        a_f = jnp.exp(la_f)
        a_b = jnp.exp(la_b)
        root = lambda y: jnp.where(y > 0.0, y * lax.rsqrt(y), 0.0)
        u_f = root(1.0 - a_f * a_f) * (i_f * xc)
        u_b = root(1.0 - a_b * a_b) * (i_b * xc)
        return a_f, u_f, a_b, u_b

    a_f, u_f, _, _ = gates(SUBLANES, N_META)
    _, h0 = _tile_scan(a_f, u_f, jnp.zeros((1, LANES), _F32), reverse=False)

    def fwd(c, h):
        t0 = pl.multiple_of(c * LRU_CHUNK, LRU_CHUNK)
        a_f, u_f, a_b, u_b = gates(_X0 + t0, LRU_CHUNK)
        hs, h = _tile_scan(a_f, u_f, h, reverse=False)
        hf[pl.ds(t0, LRU_CHUNK), :] = hs
        ab[pl.ds(t0, LRU_CHUNK), :] = a_b
        ub[pl.ds(t0, LRU_CHUNK), :] = u_b
        return h

    lax.fori_loop(0, nchunk, fwd, h0)

    def bwd(cc, h):
        c = nchunk - 1 - cc
        t0 = pl.multiple_of(c * LRU_CHUNK, LRU_CHUNK)
        hs, h = _tile_scan(ab[pl.ds(t0, LRU_CHUNK), :], ub[pl.ds(t0, LRU_CHUNK), :], h, reverse=True)
        tot = hf[pl.ds(t0, LRU_CHUNK), :] + hs
        o_ref[0, pl.ds(t0, LRU_CHUNK), :] = (tot * jax.nn.gelu(yg_ref[0, pl.ds(t0, LRU_CHUNK), :])).astype(_BF16)
        return h

    lax.fori_loop(0, nchunk, bwd, jnp.zeros((1, LANES), _F32))


def _rg_lru(xr, yg, xr_meta, conv_w, conv_b, w_cat, b_cat, c_decay):
    bsz, s = xr.shape[0], xr.shape[1]
    assert s % LRU_CHUNK == 0
    slab = pl.BlockSpec((1, s, LANES), lambda b, n: (b, 0, n))
    col = lambda r: pl.BlockSpec((r, LANES), lambda b, n: (0, n))
    return pl.pallas_call(
        functools.partial(_lru_kernel, seq=s),
        grid=(bsz, LRU_BLOCKS),
        in_specs=[slab, slab, col(N_META), col(4), col(1),
                  pl.BlockSpec((1, LANES, 4 * LANES), lambda b, n: (n, 0, 0)),
                  pl.BlockSpec((1, 1, 4 * LANES), lambda b, n: (n, 0, 0)),
                  col(2)],
        out_specs=slab,
        out_shape=jax.ShapeDtypeStruct((bsz, s, LRU_WIDTH), _BF16),
        scratch_shapes=[pltpu.VMEM((s + _X0 + SUBLANES, LANES), _F32),
                        pltpu.VMEM((s, LANES), _F32),
                        pltpu.VMEM((s, LANES), _F32),
                        pltpu.VMEM((s, LANES), _F32)],
        compiler_params=_params("arbitrary", "arbitrary"),
        name="rg_lru",
    )(xr, yg, xr_meta, conv_w, conv_b, w_cat, b_cat, c_decay)


def _mix_kernel(xp_ref, xs_ref, ap_ref, as_ref, rp_ref, rs_ref, g0_ref, b0_ref, wo_ref, g1_ref, b1_ref,
                rwt_ref, rb_ref, h1_ref, te_ref, gt_ref, rk_ref, cnt_ref, carry, *, n_prompt_tiles, tm):
    i = pl.program_id(0)

    @pl.when(i == 0)
    def _():
        carry[...] = jnp.zeros_like(carry)

    first = i < n_prompt_tiles
    x = jnp.where(first, xp_ref[...], xs_ref[...])
    attn = jnp.where(first, ap_ref[...], as_ref[...])
    rec = jnp.where(first, rp_ref[...], rs_ref[...])
    h0 = _layer_norm(x, g0_ref[...], b0_ref[...])
    mix = _dot(attn, wo_ref[0:Q_COLS, :]) + _dot(rec, wo_ref[Q_COLS:Q_COLS + LRU_WIDTH, :])
    h1 = _layer_norm(DN_ALPHA * h0 + mix, g1_ref[...], b1_ref[...])
    h1_ref[...] = h1

    logits = _dot_nt(rwt_ref[...], h1.astype(_BF16)) + rb_ref[:, 0:1]
    eidx = lax.broadcasted_iota(jnp.int32, (N_EXPERTS, tm), 0)
    work = logits
    vals, idxs, hots = [], [], []
    for _ in range(TOP_K):
        m = jnp.max(work, axis=0, keepdims=True)
        idx = jnp.min(jnp.where(work == m, eidx, N_EXPERTS), axis=0, keepdims=True)
        hot = eidx == idx
        vals.append(m)
        idxs.append(idx)
        hots.append(hot)
        work = jnp.where(hot, -jnp.inf, work)
    exps = [jnp.exp(v - vals[0]) for v in vals]
    den = exps[0] + exps[1] + exps[2] + exps[3]
    hot_all = (hots[0] | hots[1] | hots[2] | hots[3])
    hot_f = jnp.where(hot_all, 1.0, 0.0).astype(_F32)
    upper = (lax.broadcasted_iota(jnp.int32, (tm, tm), 0) < lax.broadcasted_iota(jnp.int32, (tm, tm), 1))
    before = _dot(hot_f.astype(_BF16), jnp.where(upper, 1.0, 0.0).astype(_BF16))
    base = carry[:, 0:1] + before
    zero_rows = jnp.zeros((SUBLANES - TOP_K, tm), _F32)
    ranks = [jnp.sum(jnp.where(h, base, 0.0), axis=0, keepdims=True) for h in hots]
    te_ref[...] = jnp.concatenate(idxs + [zero_rows.astype(jnp.int32)], axis=0)
    gt_ref[...] = jnp.concatenate([e / den for e in exps] + [zero_rows], axis=0)
    rk_ref[...] = jnp.concatenate(ranks + [zero_rows], axis=0).astype(jnp.int32)
    carry[...] = carry[...] + jnp.sum(hot_f, axis=1, keepdims=True)
    cnt_ref[...] = carry[...]


def _mix(x_p, x_s, attn_p, attn_s, rec_p, rec_s, g0, b0, w_out_bf16, g1, b1, rw_t, rb):
    tp, ts = x_p.shape[0], x_s.shape[0]
    t = tp + ts
    tm = min(TOKEN_TILE, tp, ts)
    assert tp % tm == 0 and ts % tm == 0
    npt = tp // tm
    first = lambda n: pl.BlockSpec((tm, n), lambda i: (jnp.minimum(i, npt - 1), 0))
    second = lambda n: pl.BlockSpec((tm, n), lambda i: (jnp.maximum(i - npt, 0), 0))
    lane_row = pl.BlockSpec((SUBLANES, tm), lambda i: (0, i))
    return pl.pallas_call(
        functools.partial(_mix_kernel, n_prompt_tiles=npt, tm=tm),
        grid=(t // tm,),
        in_specs=[first(D_MODEL), second(D_MODEL), first(Q_COLS), second(Q_COLS),
                  first(LRU_WIDTH), second(LRU_WIDTH),
                  _const_spec((1, D_MODEL)), _const_spec((1, D_MODEL)),
                  _const_spec((D_MODEL, D_MODEL)),
                  _const_spec((1, D_MODEL)), _const_spec((1, D_MODEL)),
                  _const_spec((N_EXPERTS, D_MODEL)), _const_spec((N_EXPERTS, LANES))],
        out_specs=[pl.BlockSpec((tm, D_MODEL), lambda i: (i, 0)), lane_row, lane_row, lane_row,
                   pl.BlockSpec((N_EXPERTS, LANES), lambda i: (0, 0))],
        out_shape=[jax.ShapeDtypeStruct((t, D_MODEL), _F32),
                   jax.ShapeDtypeStruct((SUBLANES, t), jnp.int32),
                   jax.ShapeDtypeStruct((SUBLANES, t), _F32),
                   jax.ShapeDtypeStruct((SUBLANES, t), jnp.int32),
                   jax.ShapeDtypeStruct((N_EXPERTS, LANES), _F32)],
        scratch_shapes=[pltpu.VMEM((N_EXPERTS, LANES), _F32)],
        compiler_params=_params("arbitrary"),
        name="mix_router",
    )(x_p, x_s, attn_p, attn_s, rec_p, rec_s, g0, b0, w_out_bf16, g1, b1, rw_t, rb)


def _dispatch_kernel(ps_ref, pn_ref, pos_ref, h1_ref, xs_hbm, zrow, sem, psem):
    i = pl.program_id(0)
    groups = h1_ref.shape[0]
    dt = groups * SUBLANES

    for k in range(TOP_K):
        def body(g, c):
            for u in range(SUBLANES):
                dst = pos_ref[0, 0, k * dt + g * SUBLANES + u]
                pltpu.make_async_copy(h1_ref.at[g, pl.ds(u, 1)], xs_hbm.at[pl.ds(dst, 1)], sem.at[0]).start()
            return c
        lax.fori_loop(0, groups, body, 0)

    @pl.when(i == 0)
    def _():
        zrow[...] = jnp.zeros_like(zrow)

        def per_expert(e, c):
            start = ps_ref[e]

            def pad_copy(j):
                return pltpu.make_async_copy(zrow.at[pl.ds(0, 1)], xs_hbm.at[pl.ds(start + j, 1)], psem.at[0])

            def go(j, c2):
                pad_copy(j).start()
                return c2

            def done(j, c2):
                pad_copy(j).wait()
                return c2

            lax.fori_loop(0, pn_ref[e], go, 0)
            lax.fori_loop(0, pn_ref[e], done, 0)
            return c

        lax.fori_loop(0, N_EXPERTS, per_expert, 0)

    for k in range(TOP_K):
        pltpu.make_async_copy(h1_ref, h1_ref, sem.at[0]).wait()


def _dispatch(pad_start, pad_n, pos3, h1, n_rows):
    t = h1.shape[0]
    dt = pos3.shape[2] // TOP_K
    grid_spec = pltpu.PrefetchScalarGridSpec(
        num_scalar_prefetch=2,
        grid=(t // dt,),
        in_specs=[pl.BlockSpec((1, 1, TOP_K * dt), lambda i, ps, pn: (i, 0, 0), memory_space=pltpu.SMEM),
                  pl.BlockSpec((dt // SUBLANES, SUBLANES, D_MODEL), lambda i, ps, pn: (i, 0, 0))],
        out_specs=pl.BlockSpec(memory_space=pl.ANY),
        scratch_shapes=[pltpu.VMEM((SUBLANES, D_MODEL), _F32),
                        pltpu.SemaphoreType.DMA((1,)), pltpu.SemaphoreType.DMA((1,))],
    )
    return pl.pallas_call(
        _dispatch_kernel,
        grid_spec=grid_spec,
        out_shape=jax.ShapeDtypeStruct((n_rows, D_MODEL), _F32),
        compiler_params=_params("arbitrary"),
        name="dispatch_rows",
    )(pad_start, pad_n, pos3, h1.reshape(t // SUBLANES, SUBLANES, D_MODEL))


def _moe_kernel(te_ref, tns_ref, src_ref, x_ref, w1g_ref, w1l_ref, b1g_ref, b1l_ref, w2_ref, b2_ref,
                o_ref, xb):
    i = pl.program_id(0)
    f = pl.program_id(1)
    nsub = tns_ref[i]

    @pl.when(nsub > 0)
    def _active():
        full = nsub == MOE_SUBS_PER_TILE
        b1g = b1g_ref[0]
        b1l = b1l_ref[0]

        def compute(r0, n, first):
            rows = pl.ds(r0, n)
            if first:
                x = x_ref[rows, :].astype(_BF16)
                xb[rows, :] = x
            else:
                x = xb[rows, :]
            glu = jnp.minimum(_dot(x, w1g_ref[0].astype(_BF16)) + b1g, SWIGLU_LIMIT)
            lin = jnp.clip(_dot(x, w1l_ref[0].astype(_BF16)) + b1l, -SWIGLU_LIMIT, SWIGLU_LIMIT)
            act = glu * jax.nn.sigmoid(SWIGLU_ALPHA * glu) * (lin + 1.0)
            part = _dot(act.astype(_BF16), w2_ref[0].astype(_BF16))
            if first:
                o_ref[rows, :] = part + b2_ref[0]
            else:
                o_ref[rows, :] += part

        @pl.when(full & (f == 0))
        def _():
            compute(0, MOE_TILE, True)

        @pl.when(full & (f > 0))
        def _():
            compute(0, MOE_TILE, False)

        @pl.when(jnp.logical_not(full) & (f == 0))
        def _():
            def sub(s, c):
                rows = pl.ds(pl.multiple_of(s * MOE_SUB, MOE_SUB), MOE_SUB)
                xb[rows, :] = x_ref[rows, :].astype(_BF16)
                return c
            lax.fori_loop(0, nsub, sub, 0)
            o_ref[...] = jnp.broadcast_to(b2_ref[0], o_ref.shape)

        @pl.when((nsub & 2) != 0)
        def _():
            compute(0, 2 * MOE_SUB, False)

        @pl.when((nsub & 5) == 1)
        def _():
            compute(pl.multiple_of((nsub & 2) * MOE_SUB, MOE_SUB), MOE_SUB, False)


def _moe(tile_e, tile_nsub, tile_src, xs, w1, b1, w2, b2):
    n_tiles = tile_e.shape[0]
    nf = D_FF // MOE_FF_CHUNK
    last = nf - 1

    def ff(i, f, tns):
        return jnp.where(tns[i] > 0, f, last)

    grid_spec = pltpu.PrefetchScalarGridSpec(
        num_scalar_prefetch=3,
        grid=(n_tiles, nf),
        in_specs=[
            pl.BlockSpec((MOE_TILE, D_MODEL), lambda i, f, te, tns, src: (src[i], 0)),
            pl.BlockSpec((1, D_MODEL, MOE_FF_CHUNK), lambda i, f, te, tns, src: (te[i], 0, ff(i, f, tns))),
            pl.BlockSpec((1, D_MODEL, MOE_FF_CHUNK), lambda i, f, te, tns, src: (te[i], 0, nf + ff(i, f, tns))),
            pl.BlockSpec((1, 1, MOE_FF_CHUNK), lambda i, f, te, tns, src: (te[i], 0, ff(i, f, tns))),
            pl.BlockSpec((1, 1, MOE_FF_CHUNK), lambda i, f, te, tns, src: (te[i], 0, nf + ff(i, f, tns))),
            pl.BlockSpec((1, MOE_FF_CHUNK, D_MODEL), lambda i, f, te, tns, src: (te[i], ff(i, f, tns), 0)),
            pl.BlockSpec((1, 1, D_MODEL), lambda i, f, te, tns, src: (te[i], 0, 0)),
        ],
        out_specs=pl.BlockSpec((MOE_TILE, D_MODEL), lambda i, f, te, tns, src: (src[i], 0)),
        scratch_shapes=[pltpu.VMEM((MOE_TILE, D_MODEL), _BF16)],
    )
    return pl.pallas_call(
        _moe_kernel,
        grid_spec=grid_spec,
        out_shape=jax.ShapeDtypeStruct(xs.shape, _F32),
        compiler_params=_params("arbitrary", "arbitrary"),
        name="moe_experts",
    )(tile_e, tile_nsub, tile_src, xs, w1, w1, b1, b1, w2, b2)


def _combine_kernel(pos_ref, posn_ref, h1_ref, gt_ref, g_ref, b_ref, ys_hbm, o_ref, rows, sem, *, n_steps):
    i = pl.program_id(0)
    ct = h1_ref.shape[0]
    cur = lax.rem(i, 2)

    def row_copy(p_ref, buf, k, g, u):
        src_row = p_ref[0, 0, k * ct + g * SUBLANES + u]
        return pltpu.make_async_copy(ys_hbm.at[pl.ds(src_row, 1)], rows.at[buf, k, g, pl.ds(u, 1)], sem.at[buf])

    def fetch(p_ref, buf, slots):
        for k in slots:
            def body(g, c):
                for u in range(SUBLANES):
                    row_copy(p_ref, buf, k, g, u).start()
                return c
            lax.fori_loop(0, ct // SUBLANES, body, 0)

    def wait_rows(buf):
        for k in range(TOP_K):
            pltpu.make_async_copy(rows.at[1 - buf, k], rows.at[buf, k], sem.at[buf]).wait()

    @pl.when(i == 0)
    def _():
        fetch(pos_ref, 0, range(TOP_K))

    fetch(posn_ref, 1 - cur, range(TOP_K - 1))
    wait_rows(cur)

    gates = jnp.transpose(gt_ref[...])
    expert_rows = lambda k: rows[cur, k].reshape(ct, D_MODEL)
    ffn = expert_rows(0) * gates[:, 0:1]
    for k in range(1, TOP_K):
        ffn = ffn + expert_rows(k) * gates[:, k:k + 1]
    for g in range(ct // SUBLANES):
        for u in range(SUBLANES):
            row_copy(posn_ref, 1 - cur, TOP_K - 1, g, u).start()
    o_ref[...] = _layer_norm(DN_ALPHA * h1_ref[...] + ffn, g_ref[...], b_ref[...])

    @pl.when(i == n_steps - 1)
    def _():
        wait_rows(1 - cur)


def _combine(h1, ys, pos3, gates_t, g2, b2, tok0, ntok):
    ct = pos3.shape[2] // TOP_K
    assert tok0 % ct == 0 and ntok % ct == 0
    off = tok0 // ct
    n_steps = ntok // ct
    return pl.pallas_call(
        functools.partial(_combine_kernel, n_steps=n_steps),
        grid=(n_steps,),
        in_specs=[pl.BlockSpec((1, 1, TOP_K * ct), lambda i: (off + i, 0, 0), memory_space=pltpu.SMEM),
                  pl.BlockSpec((1, 1, TOP_K * ct), lambda i: (off + jnp.minimum(i + 1, n_steps - 1), 0, 0),
                               memory_space=pltpu.SMEM),
                  pl.BlockSpec((ct, D_MODEL), lambda i: (off + i, 0)),
                  pl.BlockSpec((SUBLANES, ct), lambda i: (0, off + i)),
                  _const_spec((1, D_MODEL)), _const_spec((1, D_MODEL)),
                  pl.BlockSpec(memory_space=pl.ANY)],
        out_specs=pl.BlockSpec((ct, D_MODEL), lambda i: (i, 0)),
        out_shape=jax.ShapeDtypeStruct((ntok, D_MODEL), _F32),
        scratch_shapes=[pltpu.VMEM((2, TOP_K, ct // SUBLANES, SUBLANES, D_MODEL), _F32),
                        pltpu.SemaphoreType.DMA((2,))],
        compiler_params=_params("arbitrary"),
        name="combine_ln2",
    )(pos3, pos3, h1, gates_t, g2, b2, ys)


def _t5_bucket(rel):
    half = N_BUCKETS // 2
    exact = half // 2
    n = jnp.abs(rel)
    large = exact + (jnp.log(jnp.maximum(n, 1).astype(_F32) / exact)
                     / math.log(MAX_DISTANCE / exact) * (half - exact)).astype(jnp.int32)
    large = jnp.minimum(large, half - 1)
    return jnp.where(rel > 0, half, 0) + jnp.where(n < exact, n, large)


def _lookup(table, idx):
    out = jnp.zeros(idx.shape + table.shape[1:], table.dtype)
    expand = (...,) + (None,) * (table.ndim - 1)
    for j in range(table.shape[0]):
        out = out + jnp.where((idx == j)[expand], table[j], jnp.zeros((), table.dtype))
    return out


def _bias_tables(rel_bias, sink, nblk):
    rb = rel_bias.astype(_F32)
    qi = jnp.arange(BLOCK)
    kj = jnp.arange(3 * BLOCK)
    rel_band = kj[None, :] - BLOCK - qi[:, None]
    band = _lookup(rb, _t5_bucket(rel_band)).transpose(2, 0, 1)
    band = jnp.where((jnp.abs(rel_band) <= WINDOW)[None], band, NEG_INF)
    dist = jnp.arange(1, nblk * BLOCK + N_META + 1)
    by_dist = _lookup(rb, _t5_bucket(-dist))
    s = nblk * BLOCK
    meta = jnp.stack([by_dist[N_META - m - 1:N_META - m - 1 + s] for m in range(N_META)], axis=1)
    meta = meta.reshape(nblk, BLOCK, N_META, N_HEADS).transpose(0, 3, 1, 2)
    sink_col = jnp.broadcast_to(sink.astype(_F32)[None, :, None, None], (nblk, N_HEADS, BLOCK, 1))
    pad = jnp.zeros((nblk, N_HEADS, BLOCK, BLOCK - N_META - 1), _F32)
    return band, jnp.concatenate([meta, sink_col, pad], axis=-1)


def _dispatch_tables(top_e, rank, counts, t, n_tiles):
    c = counts.astype(jnp.int32)
    nb = (c + MOE_SUB - 1) // MOE_SUB
    ns = (nb + MOE_SUBS_PER_TILE - 1) // MOE_SUBS_PER_TILE
    ends = jnp.cumsum(ns)
    first_tile = ends - ns

    pos = _lookup(first_tile, top_e[:TOP_K]) * MOE_TILE + rank[:TOP_K]
    pos3 = pos.reshape(TOP_K, t // ROW_DMA_TOKENS, ROW_DMA_TOKENS).transpose(1, 0, 2).reshape(
        t // ROW_DMA_TOKENS, 1, TOP_K * ROW_DMA_TOKENS)
    pad_start = first_tile * MOE_TILE + c
    pad_n = nb * MOE_SUB - c

    tid = jnp.arange(n_tiles, dtype=jnp.int32)
    n_used = ends[-1]
    used = tid < n_used
    te = jnp.minimum(jnp.sum((tid[:, None] >= ends[None, :]).astype(jnp.int32), axis=1), N_EXPERTS - 1)
    last_used = jnp.maximum(n_used - 1, 0)
    te = jnp.where(used, te, te[last_used])
    jt = tid - first_tile[te]
    nsub = jnp.where(used, jnp.clip(nb[te] - jt * MOE_SUBS_PER_TILE, 0, MOE_SUBS_PER_TILE), 0)
    src = jnp.where(used, tid, last_used)
    return te, nsub.astype(jnp.int32), src.astype(jnp.int32), pos3, pad_start, pad_n


def kernel(x_prompt, x_sample, meta_tokens, ln_in_g, ln_in_b, rel_bias, w_in, conv_w, conv_b, lru_wa, lru_ba,
           lru_wi, lru_bi, lru_lam, attn_sink, w_out, ln1_g, ln1_b, router_w, router_b, exp_w1, exp_b1,
           exp_w2, exp_b2, ln2_g, ln2_b):
    assert DEPTH == 1 and w_in.shape[0] == 1
    row = lambda a: a.reshape(1, -1).astype(_F32)
    g0, b0 = row(ln_in_g), row(ln_in_b)
    w_in_b = w_in[0].astype(_BF16)
    w_out_b = w_out[0].astype(_BF16)

    wa, wi = lru_wa[0], lru_wi[0]
    w_cat = jnp.concatenate([wa[0], wa[1], wi[0], wi[1]], axis=-1).astype(_BF16)
    blk = lambda v: v.reshape(LRU_BLOCKS, 1, LRU_BLOCK_W)
    b_cat = jnp.concatenate([blk(lru_ba[0, 0]), blk(lru_ba[0, 1]), blk(lru_bi[0, 0]), blk(lru_bi[0, 1])],
                            axis=-1).astype(_F32)
    c_decay = -LRU_C * jax.nn.softplus(-lru_lam[0].astype(_F32))

    groups = [x_prompt, x_sample]
    nblk_max = max(x.shape[1] for x in groups) // BLOCK
    band_bias, tail_bias = _bias_tables(rel_bias, attn_sink[0], nblk_max)

    _, k_m, v_m, xr_m, _ = _in_proj(meta_tokens.astype(_F32), g0, b0, w_in_b)
    tail = lambda a: jnp.zeros((N_KV_HEADS, BLOCK, HEAD_DIM), _BF16).at[:, :N_META].set(
        a.reshape(N_META, N_KV_HEADS, HEAD_DIM).transpose(1, 0, 2))
    k_tail, v_tail = tail(k_m), tail(v_m)

    xs, attns, recs = [], [], []
    for x in groups:
        bsz, s = x.shape[0], x.shape[1]
        x2 = x.reshape(bsz * s, D_MODEL)
        q, k, v, xr, yg = _in_proj(x2, g0, b0, w_in_b)
        attn = _attention(q.reshape(bsz, s, Q_COLS), k.reshape(bsz, s, KV_COLS), v.reshape(bsz, s, KV_COLS),
                          k_tail, v_tail, band_bias, tail_bias)
        rec = _rg_lru(xr.reshape(bsz, s, LRU_WIDTH), yg.reshape(bsz, s, LRU_WIDTH), xr_m,
                      conv_w[0].astype(_F32), row(conv_b[0]), w_cat, b_cat, c_decay)
        xs.append(x2)
        attns.append(attn.reshape(bsz * s, Q_COLS))
        recs.append(rec.reshape(bsz * s, LRU_WIDTH))

    rw_t = router_w[0].T.astype(_BF16)
    rb = jnp.broadcast_to(router_b[0].astype(_F32)[:, None], (N_EXPERTS, LANES))
    h1, top_e, gates_t, rank, counts = _mix(xs[0], xs[1], attns[0], attns[1], recs[0], recs[1], g0, b0,
                                            w_out_b, row(ln1_g[0]), row(ln1_b[0]), rw_t, rb)

    t = h1.shape[0]
    n_tiles = TOP_K * t // MOE_TILE + N_EXPERTS
    te, nsub, src_tile, pos3, pad_start, pad_n = _dispatch_tables(top_e, rank, counts[:, 0], t, n_tiles)
    xs = _dispatch(pad_start, pad_n, pos3, h1, n_tiles * MOE_TILE)
    ys = _moe(te, nsub, src_tile, xs,
              exp_w1[0], exp_b1[0].reshape(N_EXPERTS, 1, 2 * D_FF), exp_w2[0],
              exp_b2[0].reshape(N_EXPERTS, 1, D_MODEL))

    outs = []
    tok0 = 0
    for x in groups:
        bsz, s = x.shape[0], x.shape[1]
        y = _combine(h1, ys, pos3, gates_t, row(ln2_g[0]), row(ln2_b[0]), tok0, bsz * s)
        outs.append(y.reshape(bsz, s, D_MODEL))
        tok0 += bsz * s
    return tuple(outs)
```

```python
import functools
import math

import jax
import jax.numpy as jnp
from jax import lax
from jax.experimental import pallas as pl
from jax.experimental.pallas import tpu as pltpu

D_MODEL = 2048
HEAD_DIM = 128
N_HEADS = 8
N_KV_HEADS = 2
GQA_GROUP = N_HEADS // N_KV_HEADS
Q_COLS = N_HEADS * HEAD_DIM
KV_COLS = N_KV_HEADS * HEAD_DIM
LRU_WIDTH = 1024
LRU_BLOCKS = 8
LRU_BLOCK_W = LRU_WIDTH // LRU_BLOCKS
LRU_C = 8.0
IN_COLS = Q_COLS + 2 * KV_COLS + 2 * LRU_WIDTH
WINDOW = 128
BLOCK = 128
N_BUCKETS = 32
MAX_DISTANCE = 128
N_META = 16
N_EXPERTS = 32
TOP_K = 4
D_FF = D_MODEL
SWIGLU_LIMIT = 7.0
SWIGLU_ALPHA = 1.702
DEPTH = 1
DN_ALPHA = (2.0 * DEPTH) ** 0.25
LN_EPS = 1e-5
NEG_INF = -1e30

SUBLANES = 8
LANES = 128
VMEM_LIMIT_BYTES = 56 * 1024 * 1024

TOKEN_TILE = 512
LRU_CHUNK = 256
MOE_SUB = 256
MOE_SUBS_PER_TILE = 4
MOE_TILE = MOE_SUB * MOE_SUBS_PER_TILE
MOE_FF_CHUNK = 256
ROW_DMA_TOKENS = 256

_BF16 = jnp.bfloat16
_F32 = jnp.float32


def _layer_norm(x, g, b):
    mu = jnp.mean(x, axis=-1, keepdims=True)
    xc = x - mu
    var = jnp.mean(xc * xc, axis=-1, keepdims=True)
    return xc * lax.rsqrt(var + LN_EPS) * g + b


def _dot(a, b):
    return jnp.dot(a, b, preferred_element_type=_F32)


def _dot_nt(a, b):
    return lax.dot_general(a, b, (((1,), (1,)), ((), ())), preferred_element_type=_F32)


def _params(*semantics):
    return pltpu.CompilerParams(dimension_semantics=semantics, vmem_limit_bytes=VMEM_LIMIT_BYTES)


def _const_spec(shape):
    nd = len(shape)
    return pl.BlockSpec(shape, lambda *_: (0,) * nd, pipeline_mode=pl.Buffered(1))


def _in_proj_kernel(x_ref, g_ref, b_ref, w_ref, q_ref, k_ref, v_ref, xr_ref, yg_ref):
    h = _layer_norm(x_ref[...], g_ref[...], b_ref[...]).astype(_BF16)
    c0, c1, c2, c3 = Q_COLS, Q_COLS + KV_COLS, Q_COLS + 2 * KV_COLS, Q_COLS + 2 * KV_COLS + LRU_WIDTH
    q_ref[...] = _dot(h, w_ref[:, 0:c0]).astype(_BF16)
    k_ref[...] = _dot(h, w_ref[:, c0:c1]).astype(_BF16)
    v_ref[...] = _dot(h, w_ref[:, c1:c2]).astype(_BF16)
    xr_ref[...] = _dot(h, w_ref[:, c2:c3])
    yg_ref[...] = _dot(h, w_ref[:, c3:IN_COLS])


def _in_proj(x, g, b, w_bf16):
    t = x.shape[0]
    tm = min(TOKEN_TILE, t)
    assert t % tm == 0
    row = lambda n: pl.BlockSpec((tm, n), lambda i: (i, 0))
    return pl.pallas_call(
        _in_proj_kernel,
        grid=(t // tm,),
        in_specs=[row(D_MODEL), _const_spec((1, D_MODEL)), _const_spec((1, D_MODEL)),
                  _const_spec((D_MODEL, IN_COLS))],
        out_specs=[row(Q_COLS), row(KV_COLS), row(KV_COLS), row(LRU_WIDTH), row(LRU_WIDTH)],
        out_shape=[jax.ShapeDtypeStruct((t, Q_COLS), _BF16),
                   jax.ShapeDtypeStruct((t, KV_COLS), _BF16),
                   jax.ShapeDtypeStruct((t, KV_COLS), _BF16),
                   jax.ShapeDtypeStruct((t, LRU_WIDTH), _F32),
                   jax.ShapeDtypeStruct((t, LRU_WIDTH), _F32)],
        compiler_params=_params("arbitrary"),
        name="in_proj",
    )(x, g, b, w_bf16)


def _attn_kernel(q_ref, kp_ref, kc_ref, kn_ref, vp_ref, vc_ref, vn_ref, kt_ref, vt_ref,
                 bb_ref, tb_ref, o_ref, *, nblk):
    i = pl.program_id(1)
    kj = lax.broadcasted_iota(jnp.int32, (1, 3 * BLOCK), 1)
    valid = ((kj >= BLOCK) | (i > 0)) & ((kj < 2 * BLOCK) | (i < nblk - 1))
    tail_valid = lax.broadcasted_iota(jnp.int32, (1, BLOCK), 1) <= N_META
    scale = HEAD_DIM ** -0.5
    gw = GQA_GROUP * HEAD_DIM
    for h in range(N_KV_HEADS):
        cols = slice(h * HEAD_DIM, (h + 1) * HEAD_DIM)
        kb = jnp.concatenate([kp_ref[0, :, cols], kc_ref[0, :, cols], kn_ref[0, :, cols]], axis=0)
        vb = jnp.concatenate([vp_ref[0, :, cols], vc_ref[0, :, cols], vn_ref[0, :, cols]], axis=0)
        kt = kt_ref[h]
        vt = vt_ref[h]
        qs = jnp.concatenate([q_ref[0, :, h * gw + g * HEAD_DIM:h * gw + (g + 1) * HEAD_DIM]
                              for g in range(GQA_GROUP)], axis=0)
        band_bias = bb_ref[h * GQA_GROUP:(h + 1) * GQA_GROUP].reshape(GQA_GROUP * BLOCK, 3 * BLOCK)
        tail_bias = tb_ref[0, h * GQA_GROUP:(h + 1) * GQA_GROUP].reshape(GQA_GROUP * BLOCK, BLOCK)
        s_b = jnp.where(valid, _dot_nt(qs, kb) * scale + band_bias, NEG_INF)
        s_t = jnp.where(tail_valid, _dot_nt(qs, kt) * scale + tail_bias, NEG_INF)
        m = jnp.maximum(jnp.max(s_b, axis=-1, keepdims=True), jnp.max(s_t, axis=-1, keepdims=True))
        e_b = jnp.exp(s_b - m)
        e_t = jnp.exp(s_t - m)
        den = jnp.sum(e_b, axis=-1, keepdims=True) + jnp.sum(e_t, axis=-1, keepdims=True)
        inv = 1.0 / den
        p_b = (e_b * inv).astype(_BF16)
        p_t = (e_t * inv).astype(_BF16)
        o = (_dot(p_b, vb) + _dot(p_t, vt)).astype(_BF16)
        for g in range(GQA_GROUP):
            o_ref[0, :, h * gw + g * HEAD_DIM:h * gw + (g + 1) * HEAD_DIM] = o[g * BLOCK:(g + 1) * BLOCK]


def _attention(q, k, v, k_tail, v_tail, band_bias, tail_bias):
    bsz, s = q.shape[0], q.shape[1]
    nblk = s // BLOCK
    kv_spec = lambda d: pl.BlockSpec(
        (1, BLOCK, KV_COLS), lambda b, i: (b, jnp.clip(i + d, 0, nblk - 1), 0))
    tail_spec = pl.BlockSpec((N_KV_HEADS, BLOCK, HEAD_DIM), lambda b, i: (0, 0, 0))
    return pl.pallas_call(
        functools.partial(_attn_kernel, nblk=nblk),
        grid=(bsz, nblk),
        in_specs=[pl.BlockSpec((1, BLOCK, Q_COLS), lambda b, i: (b, i, 0)),
                  kv_spec(-1), kv_spec(0), kv_spec(1), kv_spec(-1), kv_spec(0), kv_spec(1),
                  tail_spec, tail_spec,
                  pl.BlockSpec((N_HEADS, BLOCK, 3 * BLOCK), lambda b, i: (0, 0, 0)),
                  pl.BlockSpec((1, N_HEADS, BLOCK, BLOCK), lambda b, i: (i, 0, 0, 0))],
        out_specs=pl.BlockSpec((1, BLOCK, Q_COLS), lambda b, i: (b, i, 0)),
        out_shape=jax.ShapeDtypeStruct((bsz, s, Q_COLS), _BF16),
        compiler_params=_params("arbitrary", "arbitrary"),
        name="attention",
    )(q, k, k, k, v, v, v, k_tail, v_tail, band_bias, tail_bias)


_X0 = 24


def _tile_scan(a, u, h_in, reverse):
    n = a.shape[0]
    nt = n // SUBLANES
    a3 = a.reshape(nt, SUBLANES, LANES)
    u3 = u.reshape(nt, SUBLANES, LANES)
    row = lax.broadcasted_iota(jnp.int32, (nt, SUBLANES, LANES), 1)
    for step in (1, 2, 4):
        shift = (SUBLANES - step) if reverse else step
        a_sh = pltpu.roll(a3, shift, 1)
        u_sh = pltpu.roll(u3, shift, 1)
        ok = (row < SUBLANES - step) if reverse else (row >= step)
        u3 = jnp.where(ok, a3 * u_sh + u3, u3)
        a3 = jnp.where(ok, a3 * a_sh, a3)
    out = [None] * nt
    h = h_in
    order = range(nt - 1, -1, -1) if reverse else range(nt)
    edge = 0 if reverse else SUBLANES - 1
    for j in order:
        ht = u3[j] + a3[j] * h
        out[j] = ht
        h = ht[edge:edge + 1]
    return jnp.concatenate(out, axis=0), h


def _lru_kernel(xr_ref, yg_ref, xm_ref, cw_ref, cb_ref, w_ref, b_ref, c_ref, o_ref,
                xext, hf, ab, ub, *, seq):
    nchunk = seq // LRU_CHUNK
    zeros8 = jnp.zeros((SUBLANES, LANES), _F32)
    xext[0:SUBLANES, :] = zeros8
    xext[SUBLANES:_X0, :] = xm_ref[...]
    xext[_X0:_X0 + seq, :] = xr_ref[0]
    xext[_X0 + seq:_X0 + seq + SUBLANES, :] = zeros8
    cw = cw_ref[...]
    cb = cb_ref[...]
    w = w_ref[0]
    bias = b_ref[0]
    c_f = c_ref[0:1, :]
    c_b = c_ref[1:2, :]

    def gates(r0, n):
        win = xext[pl.ds(r0 - SUBLANES, n + 2 * SUBLANES), :]
        span = n + 2 * SUBLANES
        tap = lambda d: pltpu.roll(win, (span - d) % span, 0)[SUBLANES:SUBLANES + n]
        xc = cb + cw[0:1] * tap(-2)
        xc = xc + cw[1:2] * tap(-1)
        xc = xc + cw[2:3] * win[SUBLANES:SUBLANES + n]
        xc = xc + cw[3:4] * tap(1)
        g = _dot(xc.astype(_BF16), w) + bias
        r_f = jax.nn.sigmoid(g[:, 0:LANES])
        r_b = jax.nn.sigmoid(g[:, LANES:2 * LANES])
        i_f = jax.nn.sigmoid(g[:, 2 * LANES:3 * LANES])
        i_b = jax.nn.sigmoid(g[:, 3 * LANES:4 * LANES])
        la_f = c_f * r_f
        la_b = c_b * r_b
        a_f = jnp.exp(la_f)
        a_b = jnp.exp(la_b)
        root = lambda y: jnp.where(y > 0.0, y * lax.rsqrt(y), 0.0)
        u_f = root(1.0 - a_f * a_f) * (i_f * xc)
        u_b = root(1.0 - a_b * a_b) * (i_b * xc)
        return a_f, u_f, a_b, u_b

    a_f, u_f, _, _ = gates(SUBLANES, N_META)
    _, h0 = _tile_scan(a_f, u_f, jnp.zeros((1, LANES), _F32), reverse=False)

    def fwd(c, h):
        t0 = pl.multiple_of(c * LRU_CHUNK, LRU_CHUNK)
        a_f, u_f, a_b, u_b = gates(_X0 + t0, LRU_CHUNK)
        hs, h = _tile_scan(a_f, u_f, h, reverse=False)
        hf[pl.ds(t0, LRU_CHUNK), :] = hs
        ab[pl.ds(t0, LRU_CHUNK), :] = a_b
        ub[pl.ds(t0, LRU_CHUNK), :] = u_b
        return h

    lax.fori_loop(0, nchunk, fwd, h0)

    def bwd(cc, h):
        c = nchunk - 1 - cc
        t0 = pl.multiple_of(c * LRU_CHUNK, LRU_CHUNK)
        hs, h = _tile_scan(ab[pl.ds(t0, LRU_CHUNK), :], ub[pl.ds(t0, LRU_CHUNK), :], h, reverse=True)
        tot = hf[pl.ds(t0, LRU_CHUNK), :] + hs
        o_ref[0, pl.ds(t0, LRU_CHUNK), :] = (tot * jax.nn.gelu(yg_ref[0, pl.ds(t0, LRU_CHUNK), :])).astype(_BF16)
        return h

    lax.fori_loop(0, nchunk, bwd, jnp.zeros((1, LANES), _F32))


def _rg_lru(xr, yg, xr_meta, conv_w, conv_b, w_cat, b_cat, c_decay):
    bsz, s = xr.shape[0], xr.shape[1]
    assert s % LRU_CHUNK == 0
    slab = pl.BlockSpec((1, s, LANES), lambda b, n: (b, 0, n))
    col = lambda r: pl.BlockSpec((r, LANES), lambda b, n: (0, n))
    return pl.pallas_call(
        functools.partial(_lru_kernel, seq=s),
        grid=(bsz, LRU_BLOCKS),
        in_specs=[slab, slab, col(N_META), col(4), col(1),
                  pl.BlockSpec((1, LANES, 4 * LANES), lambda b, n: (n, 0, 0)),
                  pl.BlockSpec((1, 1, 4 * LANES), lambda b, n: (n, 0, 0)),
                  col(2)],
        out_specs=slab,
        out_shape=jax.ShapeDtypeStruct((bsz, s, LRU_WIDTH), _BF16),
        scratch_shapes=[pltpu.VMEM((s + _X0 + SUBLANES, LANES), _F32),
                        pltpu.VMEM((s, LANES), _F32),
                        pltpu.VMEM((s, LANES), _F32),
                        pltpu.VMEM((s, LANES), _F32)],
        compiler_params=_params("arbitrary", "arbitrary"),
        name="rg_lru",
    )(xr, yg, xr_meta, conv_w, conv_b, w_cat, b_cat, c_decay)


def _mix_kernel(xp_ref, xs_ref, ap_ref, as_ref, rp_ref, rs_ref, g0_ref, b0_ref, wo_ref, g1_ref, b1_ref,
                rwt_ref, rb_ref, h1_ref, te_ref, gt_ref, rk_ref, cnt_ref, carry, *, n_prompt_tiles, tm):
    i = pl.program_id(0)

    @pl.when(i == 0)
    def _():
        carry[...] = jnp.zeros_like(carry)

    first = i < n_prompt_tiles
    x = jnp.where(first, xp_ref[...], xs_ref[...])
    attn = jnp.where(first, ap_ref[...], as_ref[...])
    rec = jnp.where(first, rp_ref[...], rs_ref[...])
    h0 = _layer_norm(x, g0_ref[...], b0_ref[...])
    mix = _dot(attn, wo_ref[0:Q_COLS, :]) + _dot(rec, wo_ref[Q_COLS:Q_COLS + LRU_WIDTH, :])
    h1 = _layer_norm(DN_ALPHA * h0 + mix, g1_ref[...], b1_ref[...])
    h1_ref[...] = h1

    logits = _dot_nt(rwt_ref[...], h1.astype(_BF16)) + rb_ref[:, 0:1]
    eidx = lax.broadcasted_iota(jnp.int32, (N_EXPERTS, tm), 0)
    work = logits
    vals, idxs, hots = [], [], []
    for _ in range(TOP_K):
        m = jnp.max(work, axis=0, keepdims=True)
        idx = jnp.min(jnp.where(work == m, eidx, N_EXPERTS), axis=0, keepdims=True)
        hot = eidx == idx
        vals.append(m)
        idxs.append(idx)
        hots.append(hot)
        work = jnp.where(hot, -jnp.inf, work)
    exps = [jnp.exp(v - vals[0]) for v in vals]
    den = exps[0] + exps[1] + exps[2] + exps[3]
    hot_all = (hots[0] | hots[1] | hots[2] | hots[3])
    hot_f = jnp.where(hot_all, 1.0, 0.0).astype(_F32)
    upper = (lax.broadcasted_iota(jnp.int32, (tm, tm), 0) < lax.broadcasted_iota(jnp.int32, (tm, tm), 1))
    before = _dot(hot_f.astype(_BF16), jnp.where(upper, 1.0, 0.0).astype(_BF16))
    base = carry[:, 0:1] + before
    zero_rows = jnp.zeros((SUBLANES - TOP_K, tm), _F32)
    ranks = [jnp.sum(jnp.where(h, base, 0.0), axis=0, keepdims=True) for h in hots]
    te_ref[...] = jnp.concatenate(idxs + [zero_rows.astype(jnp.int32)], axis=0)
    gt_ref[...] = jnp.concatenate([e / den for e in exps] + [zero_rows], axis=0)
    rk_ref[...] = jnp.concatenate(ranks + [zero_rows], axis=0).astype(jnp.int32)
    carry[...] = carry[...] + jnp.sum(hot_f, axis=1, keepdims=True)
    cnt_ref[...] = carry[...]


def _mix(x_p, x_s, attn_p, attn_s, rec_p, rec_s, g0, b0, w_out_bf16, g1, b1, rw_t, rb):
    tp, ts = x_p.shape[0], x_s.shape[0]
    t = tp + ts
    tm = min(TOKEN_TILE, tp, ts)
    assert tp % tm == 0 and ts % tm == 0
    npt = tp // tm
    first = lambda n: pl.BlockSpec((tm, n), lambda i: (jnp.minimum(i, npt - 1), 0))
    second = lambda n: pl.BlockSpec((tm, n), lambda i: (jnp.maximum(i - npt, 0), 0))
    lane_row = pl.BlockSpec((SUBLANES, tm), lambda i: (0, i))
    return pl.pallas_call(
        functools.partial(_mix_kernel, n_prompt_tiles=npt, tm=tm),
        grid=(t // tm,),
        in_specs=[first(D_MODEL), second(D_MODEL), first(Q_COLS), second(Q_COLS),
                  first(LRU_WIDTH), second(LRU_WIDTH),
                  _const_spec((1, D_MODEL)), _const_spec((1, D_MODEL)),
                  _const_spec((D_MODEL, D_MODEL)),
                  _const_spec((1, D_MODEL)), _const_spec((1, D_MODEL)),
                  _const_spec((N_EXPERTS, D_MODEL)), _const_spec((N_EXPERTS, LANES))],
        out_specs=[pl.BlockSpec((tm, D_MODEL), lambda i: (i, 0)), lane_row, lane_row, lane_row,
                   pl.BlockSpec((N_EXPERTS, LANES), lambda i: (0, 0))],
        out_shape=[jax.ShapeDtypeStruct((t, D_MODEL), _F32),
                   jax.ShapeDtypeStruct((SUBLANES, t), jnp.int32),
                   jax.ShapeDtypeStruct((SUBLANES, t), _F32),
                   jax.ShapeDtypeStruct((SUBLANES, t), jnp.int32),
                   jax.ShapeDtypeStruct((N_EXPERTS, LANES), _F32)],
        scratch_shapes=[pltpu.VMEM((N_EXPERTS, LANES), _F32)],
        compiler_params=_params("arbitrary"),
        name="mix_router",
    )(x_p, x_s, attn_p, attn_s, rec_p, rec_s, g0, b0, w_out_bf16, g1, b1, rw_t, rb)


def _dispatch_kernel(ps_ref, pn_ref, pos_ref, h1_ref, xs_hbm, zrow, sem, psem):
    i = pl.program_id(0)
    groups = h1_ref.shape[0]
    dt = groups * SUBLANES

    for k in range(TOP_K):
        def body(g, c):
            for u in range(SUBLANES):
                dst = pos_ref[0, 0, k * dt + g * SUBLANES + u]
                pltpu.make_async_copy(h1_ref.at[g, pl.ds(u, 1)], xs_hbm.at[pl.ds(dst, 1)], sem.at[0]).start()
            return c
        lax.fori_loop(0, groups, body, 0)

    @pl.when(i == 0)
    def _():
        zrow[...] = jnp.zeros_like(zrow)

        def per_expert(e, c):
            start = ps_ref[e]

            def pad_copy(j):
                return pltpu.make_async_copy(zrow.at[pl.ds(0, 1)], xs_hbm.at[pl.ds(start + j, 1)], psem.at[0])

            def go(j, c2):
                pad_copy(j).start()
                return c2

            def done(j, c2):
                pad_copy(j).wait()
                return c2

            lax.fori_loop(0, pn_ref[e], go, 0)
            lax.fori_loop(0, pn_ref[e], done, 0)
            return c

        lax.fori_loop(0, N_EXPERTS, per_expert, 0)

    for k in range(TOP_K):
        pltpu.make_async_copy(h1_ref, h1_ref, sem.at[0]).wait()


def _dispatch(pad_start, pad_n, pos3, h1, n_rows):
    t = h1.shape[0]
    dt = pos3.shape[2] // TOP_K
    grid_spec = pltpu.PrefetchScalarGridSpec(
        num_scalar_prefetch=2,
        grid=(t // dt,),
        in_specs=[pl.BlockSpec((1, 1, TOP_K * dt), lambda i, ps, pn: (i, 0, 0), memory_space=pltpu.SMEM),
                  pl.BlockSpec((dt // SUBLANES, SUBLANES, D_MODEL), lambda i, ps, pn: (i, 0, 0))],
        out_specs=pl.BlockSpec(memory_space=pl.ANY),
        scratch_shapes=[pltpu.VMEM((SUBLANES, D_MODEL), _F32),
                        pltpu.SemaphoreType.DMA((1,)), pltpu.SemaphoreType.DMA((1,))],
    )
    return pl.pallas_call(
        _dispatch_kernel,
        grid_spec=grid_spec,
        out_shape=jax.ShapeDtypeStruct((n_rows, D_MODEL), _F32),
        compiler_params=_params("arbitrary"),
        name="dispatch_rows",
    )(pad_start, pad_n, pos3, h1.reshape(t // SUBLANES, SUBLANES, D_MODEL))


def _moe_kernel(te_ref, tns_ref, src_ref, x_ref, w1g_ref, w1l_ref, b1g_ref, b1l_ref, w2_ref, b2_ref,
                o_ref, xb):
    i = pl.program_id(0)
    f = pl.program_id(1)
    nsub = tns_ref[i]

    @pl.when(nsub > 0)
    def _active():
        full = nsub == MOE_SUBS_PER_TILE
        b1g = b1g_ref[0]
        b1l = b1l_ref[0]

        def compute(r0, n, first):
            rows = pl.ds(r0, n)
            if first:
                x = x_ref[rows, :].astype(_BF16)
                xb[rows, :] = x
            else:
                x = xb[rows, :]
            glu = jnp.minimum(_dot(x, w1g_ref[0].astype(_BF16)) + b1g, SWIGLU_LIMIT)
            lin = jnp.clip(_dot(x, w1l_ref[0].astype(_BF16)) + b1l, -SWIGLU_LIMIT, SWIGLU_LIMIT)
            act = glu * jax.nn.sigmoid(SWIGLU_ALPHA * glu) * (lin + 1.0)
            part = _dot(act.astype(_BF16), w2_ref[0].astype(_BF16))
            if first:
                o_ref[rows, :] = part + b2_ref[0]
            else:
                o_ref[rows, :] += part

        @pl.when(full & (f == 0))
        def _():
            compute(0, MOE_TILE, True)

        @pl.when(full & (f > 0))
        def _():
            compute(0, MOE_TILE, False)

        @pl.when(jnp.logical_not(full) & (f == 0))
        def _():
            def sub(s, c):
                rows = pl.ds(pl.multiple_of(s * MOE_SUB, MOE_SUB), MOE_SUB)
                xb[rows, :] = x_ref[rows, :].astype(_BF16)
                return c
            lax.fori_loop(0, nsub, sub, 0)
            o_ref[...] = jnp.broadcast_to(b2_ref[0], o_ref.shape)

        @pl.when((nsub & 2) != 0)
        def _():
            compute(0, 2 * MOE_SUB, False)

        @pl.when((nsub & 5) == 1)
        def _():
            compute(pl.multiple_of((nsub & 2) * MOE_SUB, MOE_SUB), MOE_SUB, False)


def _moe(tile_e, tile_nsub, tile_src, xs, w1, b1, w2, b2):
    n_tiles = tile_e.shape[0]
    nf = D_FF // MOE_FF_CHUNK
    last = nf - 1

    def ff(i, f, tns):
        return jnp.where(tns[i] > 0, f, last)

    grid_spec = pltpu.PrefetchScalarGridSpec(
        num_scalar_prefetch=3,
        grid=(n_tiles, nf),
        in_specs=[
            pl.BlockSpec((MOE_TILE, D_MODEL), lambda i, f, te, tns, src: (src[i], 0)),
            pl.BlockSpec((1, D_MODEL, MOE_FF_CHUNK), lambda i, f, te, tns, src: (te[i], 0, ff(i, f, tns))),
            pl.BlockSpec((1, D_MODEL, MOE_FF_CHUNK), lambda i, f, te, tns, src: (te[i], 0, nf + ff(i, f, tns))),
            pl.BlockSpec((1, 1, MOE_FF_CHUNK), lambda i, f, te, tns, src: (te[i], 0, ff(i, f, tns))),
            pl.BlockSpec((1, 1, MOE_FF_CHUNK), lambda i, f, te, tns, src: (te[i], 0, nf + ff(i, f, tns))),
            pl.BlockSpec((1, MOE_FF_CHUNK, D_MODEL), lambda i, f, te, tns, src: (te[i], ff(i, f, tns), 0)),
            pl.BlockSpec((1, 1, D_MODEL), lambda i, f, te, tns, src: (te[i], 0, 0)),
        ],
        out_specs=pl.BlockSpec((MOE_TILE, D_MODEL), lambda i, f, te, tns, src: (src[i], 0)),
        scratch_shapes=[pltpu.VMEM((MOE_TILE, D_MODEL), _BF16)],
    )
    return pl.pallas_call(
        _moe_kernel,
        grid_spec=grid_spec,
        out_shape=jax.ShapeDtypeStruct(xs.shape, _F32),
        compiler_params=_params("arbitrary", "arbitrary"),
        name="moe_experts",
    )(tile_e, tile_nsub, tile_src, xs, w1, w1, b1, b1, w2, b2)


def _combine_kernel(pos_ref, posn_ref, h1_ref, gt_ref, g_ref, b_ref, ys_hbm, o_ref, rows, sem, *, n_steps):
    i = pl.program_id(0)
    ct = h1_ref.shape[0]
    cur = lax.rem(i, 2)

    def row_copy(p_ref, buf, k, g, u):
        src_row = p_ref[0, 0, k * ct + g * SUBLANES + u]
        return pltpu.make_async_copy(ys_hbm.at[pl.ds(src_row, 1)], rows.at[buf, k, g, pl.ds(u, 1)], sem.at[buf])

    def fetch(p_ref, buf, slots):
        for k in slots:
            def body(g, c):
                for u in range(SUBLANES):
                    row_copy(p_ref, buf, k, g, u).start()
                return c
            lax.fori_loop(0, ct // SUBLANES, body, 0)

    def wait_rows(buf):
        for k in range(TOP_K):
            pltpu.make_async_copy(rows.at[1 - buf, k], rows.at[buf, k], sem.at[buf]).wait()

    @pl.when(i == 0)
    def _():
        fetch(pos_ref, 0, range(TOP_K))

    fetch(posn_ref, 1 - cur, range(TOP_K - 1))
    wait_rows(cur)

    gates = jnp.transpose(gt_ref[...])
    expert_rows = lambda k: rows[cur, k].reshape(ct, D_MODEL)
    ffn = expert_rows(0) * gates[:, 0:1]
    for k in range(1, TOP_K):
        ffn = ffn + expert_rows(k) * gates[:, k:k + 1]
    for g in range(ct // SUBLANES):
        for u in range(SUBLANES):
            row_copy(posn_ref, 1 - cur, TOP_K - 1, g, u).start()
    o_ref[...] = _layer_norm(DN_ALPHA * h1_ref[...] + ffn, g_ref[...], b_ref[...])

    @pl.when(i == n_steps - 1)
    def _():
        wait_rows(1 - cur)


def _combine(h1, ys, pos3, gates_t, g2, b2, tok0, ntok):
    ct = pos3.shape[2] // TOP_K
    assert tok0 % ct == 0 and ntok % ct == 0
    off = tok0 // ct
    n_steps = ntok // ct
    return pl.pallas_call(
        functools.partial(_combine_kernel, n_steps=n_steps),
        grid=(n_steps,),
        in_specs=[pl.BlockSpec((1, 1, TOP_K * ct), lambda i: (off + i, 0, 0), memory_space=pltpu.SMEM),
                  pl.BlockSpec((1, 1, TOP_K * ct), lambda i: (off + jnp.minimum(i + 1, n_steps - 1), 0, 0),
                               memory_space=pltpu.SMEM),
                  pl.BlockSpec((ct, D_MODEL), lambda i: (off + i, 0)),
                  pl.BlockSpec((SUBLANES, ct), lambda i: (0, off + i)),
                  _const_spec((1, D_MODEL)), _const_spec((1, D_MODEL)),
                  pl.BlockSpec(memory_space=pl.ANY)],
        out_specs=pl.BlockSpec((ct, D_MODEL), lambda i: (i, 0)),
        out_shape=jax.ShapeDtypeStruct((ntok, D_MODEL), _F32),
        scratch_shapes=[pltpu.VMEM((2, TOP_K, ct // SUBLANES, SUBLANES, D_MODEL), _F32),
                        pltpu.SemaphoreType.DMA((2,))],
        compiler_params=_params("arbitrary"),
        name="combine_ln2",
    )(pos3, pos3, h1, gates_t, g2, b2, ys)


def _t5_bucket(rel):
    half = N_BUCKETS // 2
    exact = half // 2
    n = jnp.abs(rel)
    large = exact + (jnp.log(jnp.maximum(n, 1).astype(_F32) / exact)
                     / math.log(MAX_DISTANCE / exact) * (half - exact)).astype(jnp.int32)
    large = jnp.minimum(large, half - 1)
    return jnp.where(rel > 0, half, 0) + jnp.where(n < exact, n, large)


def _lookup(table, idx):
    out = jnp.zeros(idx.shape + table.shape[1:], table.dtype)
    expand = (...,) + (None,) * (table.ndim - 1)
    for j in range(table.shape[0]):
        out = out + jnp.where((idx == j)[expand], table[j], jnp.zeros((), table.dtype))
    return out


def _bias_tables(rel_bias, sink, nblk):
    rb = rel_bias.astype(_F32)
    qi = jnp.arange(BLOCK)
    kj = jnp.arange(3 * BLOCK)
    rel_band = kj[None, :] - BLOCK - qi[:, None]
    band = _lookup(rb, _t5_bucket(rel_band)).transpose(2, 0, 1)
    band = jnp.where((jnp.abs(rel_band) <= WINDOW)[None], band, NEG_INF)
    dist = jnp.arange(1, nblk * BLOCK + N_META + 1)
    by_dist = _lookup(rb, _t5_bucket(-dist))
    s = nblk * BLOCK
    meta = jnp.stack([by_dist[N_META - m - 1:N_META - m - 1 + s] for m in range(N_META)], axis=1)
    meta = meta.reshape(nblk, BLOCK, N_META, N_HEADS).transpose(0, 3, 1, 2)
    sink_col = jnp.broadcast_to(sink.astype(_F32)[None, :, None, None], (nblk, N_HEADS, BLOCK, 1))
    pad = jnp.zeros((nblk, N_HEADS, BLOCK, BLOCK - N_META - 1), _F32)
    return band, jnp.concatenate([meta, sink_col, pad], axis=-1)


def _dispatch_tables(top_e, rank, counts, t, n_tiles):
    c = counts.astype(jnp.int32)
    nb = (c + MOE_SUB - 1) // MOE_SUB
    ns = (nb + MOE_SUBS_PER_TILE - 1) // MOE_SUBS_PER_TILE
    ends = jnp.cumsum(ns)
    first_tile = ends - ns

    pos = _lookup(first_tile, top_e[:TOP_K]) * MOE_TILE + rank[:TOP_K]
    pos3 = pos.reshape(TOP_K, t // ROW_DMA_TOKENS, ROW_DMA_TOKENS).transpose(1, 0, 2).reshape(
        t // ROW_DMA_TOKENS, 1, TOP_K * ROW_DMA_TOKENS)
    pad_start = first_tile * MOE_TILE + c
    pad_n = nb * MOE_SUB - c

    tid = jnp.arange(n_tiles, dtype=jnp.int32)
    n_used = ends[-1]
    used = tid < n_used
    te = jnp.minimum(jnp.sum((tid[:, None] >= ends[None, :]).astype(jnp.int32), axis=1), N_EXPERTS - 1)
    last_used = jnp.maximum(n_used - 1, 0)
    te = jnp.where(used, te, te[last_used])
    jt = tid - first_tile[te]
    nsub = jnp.where(used, jnp.clip(nb[te] - jt * MOE_SUBS_PER_TILE, 0, MOE_SUBS_PER_TILE), 0)
    src = jnp.where(used, tid, last_used)
    return te, nsub.astype(jnp.int32), src.astype(jnp.int32), pos3, pad_start, pad_n


def kernel(x_prompt, x_sample, meta_tokens, ln_in_g, ln_in_b, rel_bias, w_in, conv_w, conv_b, lru_wa, lru_ba,
           lru_wi, lru_bi, lru_lam, attn_sink, w_out, ln1_g, ln1_b, router_w, router_b, exp_w1, exp_b1,
           exp_w2, exp_b2, ln2_g, ln2_b):
    assert DEPTH == 1 and w_in.shape[0] == 1
    row = lambda a: a.reshape(1, -1).astype(_F32)
    g0, b0 = row(ln_in_g), row(ln_in_b)
    w_in_b = w_in[0].astype(_BF16)
    w_out_b = w_out[0].astype(_BF16)

    wa, wi = lru_wa[0], lru_wi[0]
    w_cat = jnp.concatenate([wa[0], wa[1], wi[0], wi[1]], axis=-1).astype(_BF16)
    blk = lambda v: v.reshape(LRU_BLOCKS, 1, LRU_BLOCK_W)
    b_cat = jnp.concatenate([blk(lru_ba[0, 0]), blk(lru_ba[0, 1]), blk(lru_bi[0, 0]), blk(lru_bi[0, 1])],
                            axis=-1).astype(_F32)
    c_decay = -LRU_C * jax.nn.softplus(-lru_lam[0].astype(_F32))

    groups = [x_prompt, x_sample]
    nblk_max = max(x.shape[1] for x in groups) // BLOCK
    band_bias, tail_bias = _bias_tables(rel_bias, attn_sink[0], nblk_max)

    _, k_m, v_m, xr_m, _ = _in_proj(meta_tokens.astype(_F32), g0, b0, w_in_b)
    tail = lambda a: jnp.zeros((N_KV_HEADS, BLOCK, HEAD_DIM), _BF16).at[:, :N_META].set(
        a.reshape(N_META, N_KV_HEADS, HEAD_DIM).transpose(1, 0, 2))
    k_tail, v_tail = tail(k_m), tail(v_m)

    xs, attns, recs = [], [], []
    for x in groups:
        bsz, s = x.shape[0], x.shape[1]
        x2 = x.reshape(bsz * s, D_MODEL)
        q, k, v, xr, yg = _in_proj(x2, g0, b0, w_in_b)
        attn = _attention(q.reshape(bsz, s, Q_COLS), k.reshape(bsz, s, KV_COLS), v.reshape(bsz, s, KV_COLS),
                          k_tail, v_tail, band_bias, tail_bias)
        rec = _rg_lru(xr.reshape(bsz, s, LRU_WIDTH), yg.reshape(bsz, s, LRU_WIDTH), xr_m,
                      conv_w[0].astype(_F32), row(conv_b[0]), w_cat, b_cat, c_decay)
        xs.append(x2)
        attns.append(attn.reshape(bsz * s, Q_COLS))
        recs.append(rec.reshape(bsz * s, LRU_WIDTH))

    rw_t = router_w[0].T.astype(_BF16)
    rb = jnp.broadcast_to(router_b[0].astype(_F32)[:, None], (N_EXPERTS, LANES))
    h1, top_e, gates_t, rank, counts = _mix(xs[0], xs[1], attns[0], attns[1], recs[0], recs[1], g0, b0,
                                            w_out_b, row(ln1_g[0]), row(ln1_b[0]), rw_t, rb)

    t = h1.shape[0]
    n_tiles = TOP_K * t // MOE_TILE + N_EXPERTS
    te, nsub, src_tile, pos3, pad_start, pad_n = _dispatch_tables(top_e, rank, counts[:, 0], t, n_tiles)
    xs = _dispatch(pad_start, pad_n, pos3, h1, n_tiles * MOE_TILE)
    ys = _moe(te, nsub, src_tile, xs,
              exp_w1[0], exp_b1[0].reshape(N_EXPERTS, 1, 2 * D_FF), exp_w2[0],
              exp_b2[0].reshape(N_EXPERTS, 1, D_MODEL))

    outs = []
    tok0 = 0
    for x in groups:
        bsz, s = x.shape[0], x.shape[1]
        y = _combine(h1, ys, pos3, gates_t, row(ln2_g[0]), row(ln2_b[0]), tok0, bsz * s)
        outs.append(y.reshape(bsz, s, D_MODEL))
        tok0 += bsz * s
    return tuple(outs)
```

```python
import functools
import math

import jax
import jax.numpy as jnp
from jax import lax
from jax.experimental import pallas as pl
from jax.experimental.pallas import tpu as pltpu

D_MODEL = 2048
HEAD_DIM = 128
N_HEADS = 8
N_KV_HEADS = 2
GQA_GROUP = N_HEADS // N_KV_HEADS
Q_COLS = N_HEADS * HEAD_DIM
KV_COLS = N_KV_HEADS * HEAD_DIM
LRU_WIDTH = 1024
LRU_BLOCKS = 8
LRU_BLOCK_W = LRU_WIDTH // LRU_BLOCKS
LRU_C = 8.0
IN_COLS = Q_COLS + 2 * KV_COLS + 2 * LRU_WIDTH
WINDOW = 128
BLOCK = 128
N_BUCKETS = 32
MAX_DISTANCE = 128
N_META = 16
N_EXPERTS = 32
TOP_K = 4
D_FF = D_MODEL
SWIGLU_LIMIT = 7.0
SWIGLU_ALPHA = 1.702
DEPTH = 1
DN_ALPHA = (2.0 * DEPTH) ** 0.25
LN_EPS = 1e-5
NEG_INF = -1e30

SUBLANES = 8
LANES = 128
VMEM_LIMIT_BYTES = 56 * 1024 * 1024

TOKEN_TILE = 512
LRU_CHUNK = 512
MOE_SUB = 256
MOE_SUBS_PER_TILE = 4
MOE_TILE = MOE_SUB * MOE_SUBS_PER_TILE
MOE_FF_CHUNK = 256
ROW_DMA_TOKENS = 256

_BF16 = jnp.bfloat16
_F32 = jnp.float32


def _layer_norm(x, g, b):
    mu = jnp.mean(x, axis=-1, keepdims=True)
    xc = x - mu
    var = jnp.mean(xc * xc, axis=-1, keepdims=True)
    return xc * lax.rsqrt(var + LN_EPS) * g + b


def _dot(a, b):
    return jnp.dot(a, b, preferred_element_type=_F32)


def _dot_nt(a, b):
    return lax.dot_general(a, b, (((1,), (1,)), ((), ())), preferred_element_type=_F32)


def _params(*semantics):
    return pltpu.CompilerParams(dimension_semantics=semantics, vmem_limit_bytes=VMEM_LIMIT_BYTES)


def _const_spec(shape):
    nd = len(shape)
    return pl.BlockSpec(shape, lambda *_: (0,) * nd, pipeline_mode=pl.Buffered(1))


def _in_proj_kernel(x_ref, g_ref, b_ref, w_ref, q_ref, k_ref, v_ref, xr_ref, yg_ref):
    h = _layer_norm(x_ref[...], g_ref[...], b_ref[...]).astype(_BF16)
    c0, c1, c2, c3 = Q_COLS, Q_COLS + KV_COLS, Q_COLS + 2 * KV_COLS, Q_COLS + 2 * KV_COLS + LRU_WIDTH
    q_ref[...] = _dot(h, w_ref[:, 0:c0]).astype(_BF16)
    k_ref[...] = _dot(h, w_ref[:, c0:c1]).astype(_BF16)
    v_ref[...] = _dot(h, w_ref[:, c1:c2]).astype(_BF16)
    xr_ref[...] = _dot(h, w_ref[:, c2:c3])
    yg_ref[...] = _dot(h, w_ref[:, c3:IN_COLS])


def _in_proj(x, g, b, w_bf16):
    t = x.shape[0]
    tm = min(TOKEN_TILE, t)
    assert t % tm == 0
    row = lambda n: pl.BlockSpec((tm, n), lambda i: (i, 0))
    return pl.pallas_call(
        _in_proj_kernel,
        grid=(t // tm,),
        in_specs=[row(D_MODEL), _const_spec((1, D_MODEL)), _const_spec((1, D_MODEL)),
                  _const_spec((D_MODEL, IN_COLS))],
        out_specs=[row(Q_COLS), row(KV_COLS), row(KV_COLS), row(LRU_WIDTH), row(LRU_WIDTH)],
        out_shape=[jax.ShapeDtypeStruct((t, Q_COLS), _BF16),
                   jax.ShapeDtypeStruct((t, KV_COLS), _BF16),
                   jax.ShapeDtypeStruct((t, KV_COLS), _BF16),
                   jax.ShapeDtypeStruct((t, LRU_WIDTH), _F32),
                   jax.ShapeDtypeStruct((t, LRU_WIDTH), _F32)],
        compiler_params=_params("arbitrary"),
        name="in_proj",
    )(x, g, b, w_bf16)


def _attn_kernel(q_ref, kp_ref, kc_ref, kn_ref, vp_ref, vc_ref, vn_ref, kt_ref, vt_ref,
                 bb_ref, tb_ref, o_ref, *, nblk):
    i = pl.program_id(1)
    kj = lax.broadcasted_iota(jnp.int32, (1, 3 * BLOCK), 1)
    valid = ((kj >= BLOCK) | (i > 0)) & ((kj < 2 * BLOCK) | (i < nblk - 1))
    tail_valid = lax.broadcasted_iota(jnp.int32, (1, BLOCK), 1) <= N_META
    scale = HEAD_DIM ** -0.5
    gw = GQA_GROUP * HEAD_DIM
    for h in range(N_KV_HEADS):
        cols = slice(h * HEAD_DIM, (h + 1) * HEAD_DIM)
        kb = jnp.concatenate([kp_ref[0, :, cols], kc_ref[0, :, cols], kn_ref[0, :, cols]], axis=0)
        vb = jnp.concatenate([vp_ref[0, :, cols], vc_ref[0, :, cols], vn_ref[0, :, cols]], axis=0)
        kt = kt_ref[h]
        vt = vt_ref[h]
        qs = jnp.concatenate([q_ref[0, :, h * gw + g * HEAD_DIM:h * gw + (g + 1) * HEAD_DIM]
                              for g in range(GQA_GROUP)], axis=0)
        band_bias = bb_ref[h * GQA_GROUP:(h + 1) * GQA_GROUP].reshape(GQA_GROUP * BLOCK, 3 * BLOCK)
        tail_bias = tb_ref[0, h * GQA_GROUP:(h + 1) * GQA_GROUP].reshape(GQA_GROUP * BLOCK, BLOCK)
        s_b = jnp.where(valid, _dot_nt(qs, kb) * scale + band_bias, NEG_INF)
        s_t = jnp.where(tail_valid, _dot_nt(qs, kt) * scale + tail_bias, NEG_INF)
        m = jnp.maximum(jnp.max(s_b, axis=-1, keepdims=True), jnp.max(s_t, axis=-1, keepdims=True))
        e_b = jnp.exp(s_b - m)
        e_t = jnp.exp(s_t - m)
        den = jnp.sum(e_b, axis=-1, keepdims=True) + jnp.sum(e_t, axis=-1, keepdims=True)
        inv = 1.0 / den
        p_b = (e_b * inv).astype(_BF16)
        p_t = (e_t * inv).astype(_BF16)
        o = (_dot(p_b, vb) + _dot(p_t, vt)).astype(_BF16)
        for g in range(GQA_GROUP):
            o_ref[0, :, h * gw + g * HEAD_DIM:h * gw + (g + 1) * HEAD_DIM] = o[g * BLOCK:(g + 1) * BLOCK]


def _attention(q, k, v, k_tail, v_tail, band_bias, tail_bias):
    bsz, s = q.shape[0], q.shape[1]
    nblk = s // BLOCK
    kv_spec = lambda d: pl.BlockSpec(
        (1, BLOCK, KV_COLS), lambda b, i: (b, jnp.clip(i + d, 0, nblk - 1), 0))
    tail_spec = pl.BlockSpec((N_KV_HEADS, BLOCK, HEAD_DIM), lambda b, i: (0, 0, 0))
    return pl.pallas_call(
        functools.partial(_attn_kernel, nblk=nblk),
        grid=(bsz, nblk),
        in_specs=[pl.BlockSpec((1, BLOCK, Q_COLS), lambda b, i: (b, i, 0)),
                  kv_spec(-1), kv_spec(0), kv_spec(1), kv_spec(-1), kv_spec(0), kv_spec(1),
                  tail_spec, tail_spec,
                  pl.BlockSpec((N_HEADS, BLOCK, 3 * BLOCK), lambda b, i: (0, 0, 0)),
                  pl.BlockSpec((1, N_HEADS, BLOCK, BLOCK), lambda b, i: (i, 0, 0, 0))],
        out_specs=pl.BlockSpec((1, BLOCK, Q_COLS), lambda b, i: (b, i, 0)),
        out_shape=jax.ShapeDtypeStruct((bsz, s, Q_COLS), _BF16),
        compiler_params=_params("arbitrary", "arbitrary"),
        name="attention",
    )(q, k, k, k, v, v, v, k_tail, v_tail, band_bias, tail_bias)


_X0 = 24


def _tile_scan(a, u, h_in, reverse):
    n = a.shape[0]
    nt = n // SUBLANES
    a3 = a.reshape(nt, SUBLANES, LANES)
    u3 = u.reshape(nt, SUBLANES, LANES)
    row = lax.broadcasted_iota(jnp.int32, (nt, SUBLANES, LANES), 1)
    for step in (1, 2, 4):
        shift = (SUBLANES - step) if reverse else step
        a_sh = pltpu.roll(a3, shift, 1)
        u_sh = pltpu.roll(u3, shift, 1)
        ok = (row < SUBLANES - step) if reverse else (row >= step)
        u3 = jnp.where(ok, a3 * u_sh + u3, u3)
        a3 = jnp.where(ok, a3 * a_sh, a3)
    out = [None] * nt
    h = h_in
    order = range(nt - 1, -1, -1) if reverse else range(nt)
    edge = 0 if reverse else SUBLANES - 1
    for j in order:
        ht = u3[j] + a3[j] * h
        out[j] = ht
        h = ht[edge:edge + 1]
    return jnp.concatenate(out, axis=0), h


def _chunk_scan(a_ref, u_ref, o_ref, a0, o0, h_in, reverse):
    nt = LRU_CHUNK // SUBLANES
    order = range(SUBLANES - 1, -1, -1) if reverse else range(SUBLANES)
    rows = lambda ref, base, s: ref[pl.ds(base + s, nt, stride=SUBLANES), :]
    hs, ps = {}, {}
    prev = None
    for s in order:
        a_s = rows(a_ref, a0, s)
        u_s = rows(u_ref, a0, s)
        hs[s] = u_s if prev is None else a_s * hs[prev] + u_s
        ps[s] = a_s if prev is None else a_s * ps[prev]
        prev = s
    ends, h_out = _tile_scan(ps[prev], hs[prev], h_in, reverse)
    tile = lax.broadcasted_iota(jnp.int32, (nt, LANES), 0)
    if reverse:
        carry = jnp.where(tile == nt - 1, h_in, pltpu.roll(ends, nt - 1, 0))
    else:
        carry = jnp.where(tile == 0, h_in, pltpu.roll(ends, 1, 0))
    for s in order:
        o_ref[pl.ds(o0 + s, nt, stride=SUBLANES), :] = hs[s] + ps[s] * carry
    return h_out


def _lru_kernel(xr_ref, yg_ref, xm_ref, cw_ref, cb_ref, w_ref, b_ref, c_ref, o_ref,
                xext, hf, ab, ub, sa, su, *, seq):
    nchunk = seq // LRU_CHUNK
    zeros8 = jnp.zeros((SUBLANES, LANES), _F32)
    xext[0:SUBLANES, :] = zeros8
    xext[SUBLANES:_X0, :] = xm_ref[...]
    xext[_X0:_X0 + seq, :] = xr_ref[0]
    xext[_X0 + seq:_X0 + seq + SUBLANES, :] = zeros8
    cw = cw_ref[...]
    cb = cb_ref[...]
    w = w_ref[0]
    bias = b_ref[0]
    c_f = c_ref[0:1, :]
    c_b = c_ref[1:2, :]

    def gates(r0, n):
        win = xext[pl.ds(r0 - SUBLANES, n + 2 * SUBLANES), :]
        span = n + 2 * SUBLANES
        tap = lambda d: pltpu.roll(win, (span - d) % span, 0)[SUBLANES:SUBLANES + n]
        xc = cb + cw[0:1] * tap(-2)
        xc = xc + cw[1:2] * tap(-1)
        xc = xc + cw[2:3] * win[SUBLANES:SUBLANES + n]
        xc = xc + cw[3:4] * tap(1)
        g = _dot(xc.astype(_BF16), w) + bias
        r_f = jax.nn.sigmoid(g[:, 0:LANES])
        r_b = jax.nn.sigmoid(g[:, LANES:2 * LANES])
        i_f = jax.nn.sigmoid(g[:, 2 * LANES:3 * LANES])
        i_b = jax.nn.sigmoid(g[:, 3 * LANES:4 * LANES])
        la_f = c_f * r_f
        la_b = c_b * r_b
        a_f = jnp.exp(la_f)
        a_b = jnp.exp(la_b)
        root = lambda y: jnp.where(y > 0.0, y * lax.rsqrt(y), 0.0)
        u_f = root(1.0 - a_f * a_f) * (i_f * xc)
        u_b = root(1.0 - a_b * a_b) * (i_b * xc)
        return a_f, u_f, a_b, u_b

    a_f, u_f, _, _ = gates(SUBLANES, N_META)
    _, h0 = _tile_scan(a_f, u_f, jnp.zeros((1, LANES), _F32), reverse=False)

    def fwd(c, h):
        t0 = pl.multiple_of(c * LRU_CHUNK, LRU_CHUNK)
        a_f, u_f, a_b, u_b = gates(_X0 + t0, LRU_CHUNK)
        sa[...] = a_f
        su[...] = u_f
        ab[pl.ds(t0, LRU_CHUNK), :] = a_b
        ub[pl.ds(t0, LRU_CHUNK), :] = u_b
        return _chunk_scan(sa, su, hf, 0, t0, h, reverse=False)

    lax.fori_loop(0, nchunk, fwd, h0)

    def bwd(cc, h):
        c = nchunk - 1 - cc
        t0 = pl.multiple_of(c * LRU_CHUNK, LRU_CHUNK)
        h = _chunk_scan(ab, ub, sa, t0, 0, h, reverse=True)
        tot = hf[pl.ds(t0, LRU_CHUNK), :] + sa[...]
        o_ref[0, pl.ds(t0, LRU_CHUNK), :] = (tot * jax.nn.gelu(yg_ref[0, pl.ds(t0, LRU_CHUNK), :])).astype(_BF16)
        return h

    lax.fori_loop(0, nchunk, bwd, jnp.zeros((1, LANES), _F32))


def _rg_lru(xr, yg, xr_meta, conv_w, conv_b, w_cat, b_cat, c_decay):
    bsz, s = xr.shape[0], xr.shape[1]
    assert s % LRU_CHUNK == 0
    slab = pl.BlockSpec((1, s, LANES), lambda b, n: (b, 0, n))
    col = lambda r: pl.BlockSpec((r, LANES), lambda b, n: (0, n))
    return pl.pallas_call(
        functools.partial(_lru_kernel, seq=s),
        grid=(bsz, LRU_BLOCKS),
        in_specs=[slab, slab, col(N_META), col(4), col(1),
                  pl.BlockSpec((1, LANES, 4 * LANES), lambda b, n: (n, 0, 0)),
                  pl.BlockSpec((1, 1, 4 * LANES), lambda b, n: (n, 0, 0)),
                  col(2)],
        out_specs=slab,
        out_shape=jax.ShapeDtypeStruct((bsz, s, LRU_WIDTH), _BF16),
        scratch_shapes=[pltpu.VMEM((s + _X0 + SUBLANES, LANES), _F32),
                        pltpu.VMEM((s, LANES), _F32),
                        pltpu.VMEM((s, LANES), _F32),
                        pltpu.VMEM((s, LANES), _F32),
                        pltpu.VMEM((LRU_CHUNK, LANES), _F32),
                        pltpu.VMEM((LRU_CHUNK, LANES), _F32)],
        compiler_params=_params("arbitrary", "arbitrary"),
        name="rg_lru",
    )(xr, yg, xr_meta, conv_w, conv_b, w_cat, b_cat, c_decay)


def _mix_kernel(xp_ref, xs_ref, ap_ref, as_ref, rp_ref, rs_ref, g0_ref, b0_ref, wo_ref, g1_ref, b1_ref,
                rwt_ref, rb_ref, h1_ref, te_ref, gt_ref, rk_ref, cnt_ref, carry, *, n_prompt_tiles, tm):
    i = pl.program_id(0)

    @pl.when(i == 0)
    def _():
        carry[...] = jnp.zeros_like(carry)

    first = i < n_prompt_tiles
    x = jnp.where(first, xp_ref[...], xs_ref[...])
    attn = jnp.where(first, ap_ref[...], as_ref[...])
    rec = jnp.where(first, rp_ref[...], rs_ref[...])
    h0 = _layer_norm(x, g0_ref[...], b0_ref[...])
    mix = _dot(attn, wo_ref[0:Q_COLS, :]) + _dot(rec, wo_ref[Q_COLS:Q_COLS + LRU_WIDTH, :])
    h1 = _layer_norm(DN_ALPHA * h0 + mix, g1_ref[...], b1_ref[...])
    h1_ref[...] = h1

    logits = _dot_nt(rwt_ref[...], h1.astype(_BF16)) + rb_ref[:, 0:1]
    eidx = lax.broadcasted_iota(jnp.int32, (N_EXPERTS, tm), 0)
    work = logits
    vals, idxs, hots = [], [], []
    for _ in range(TOP_K):
        m = jnp.max(work, axis=0, keepdims=True)
        idx = jnp.min(jnp.where(work == m, eidx, N_EXPERTS), axis=0, keepdims=True)
        hot = eidx == idx
        vals.append(m)
        idxs.append(idx)
        hots.append(hot)
        work = jnp.where(hot, -jnp.inf, work)
    exps = [jnp.exp(v - vals[0]) for v in vals]
    den = exps[0] + exps[1] + exps[2] + exps[3]
    hot_all = (hots[0] | hots[1] | hots[2] | hots[3])
    hot_f = jnp.where(hot_all, 1.0, 0.0).astype(_F32)
    upper = (lax.broadcasted_iota(jnp.int32, (tm, tm), 0) < lax.broadcasted_iota(jnp.int32, (tm, tm), 1))
    before = _dot(hot_f.astype(_BF16), jnp.where(upper, 1.0, 0.0).astype(_BF16))
    base = carry[:, 0:1] + before
    zero_rows = jnp.zeros((SUBLANES - TOP_K, tm), _F32)
    ranks = [jnp.sum(jnp.where(h, base, 0.0), axis=0, keepdims=True) for h in hots]
    te_ref[...] = jnp.concatenate(idxs + [zero_rows.astype(jnp.int32)], axis=0)
    gt_ref[...] = jnp.concatenate([e / den for e in exps] + [zero_rows], axis=0)
    rk_ref[...] = jnp.concatenate(ranks + [zero_rows], axis=0).astype(jnp.int32)
    carry[...] = carry[...] + jnp.sum(hot_f, axis=1, keepdims=True)
    cnt_ref[...] = carry[...]


def _mix(x_p, x_s, attn_p, attn_s, rec_p, rec_s, g0, b0, w_out_bf16, g1, b1, rw_t, rb):
    tp, ts = x_p.shape[0], x_s.shape[0]
    t = tp + ts
    tm = min(TOKEN_TILE, tp, ts)
    assert tp % tm == 0 and ts % tm == 0
    npt = tp // tm
    first = lambda n: pl.BlockSpec((tm, n), lambda i: (jnp.minimum(i, npt - 1), 0))
    second = lambda n: pl.BlockSpec((tm, n), lambda i: (jnp.maximum(i - npt, 0), 0))
    lane_row = pl.BlockSpec((SUBLANES, tm), lambda i: (0, i))
    return pl.pallas_call(
        functools.partial(_mix_kernel, n_prompt_tiles=npt, tm=tm),
        grid=(t // tm,),
        in_specs=[first(D_MODEL), second(D_MODEL), first(Q_COLS), second(Q_COLS),
                  first(LRU_WIDTH), second(LRU_WIDTH),
                  _const_spec((1, D_MODEL)), _const_spec((1, D_MODEL)),
                  _const_spec((D_MODEL, D_MODEL)),
                  _const_spec((1, D_MODEL)), _const_spec((1, D_MODEL)),
                  _const_spec((N_EXPERTS, D_MODEL)), _const_spec((N_EXPERTS, LANES))],
        out_specs=[pl.BlockSpec((tm, D_MODEL), lambda i: (i, 0)), lane_row, lane_row, lane_row,
                   pl.BlockSpec((N_EXPERTS, LANES), lambda i: (0, 0))],
        out_shape=[jax.ShapeDtypeStruct((t, D_MODEL), _F32),
                   jax.ShapeDtypeStruct((SUBLANES, t), jnp.int32),
                   jax.ShapeDtypeStruct((SUBLANES, t), _F32),
                   jax.ShapeDtypeStruct((SUBLANES, t), jnp.int32),
                   jax.ShapeDtypeStruct((N_EXPERTS, LANES), _F32)],
        scratch_shapes=[pltpu.VMEM((N_EXPERTS, LANES), _F32)],
        compiler_params=_params("arbitrary"),
        name="mix_router",
    )(x_p, x_s, attn_p, attn_s, rec_p, rec_s, g0, b0, w_out_bf16, g1, b1, rw_t, rb)


def _dispatch_kernel(ps_ref, pn_ref, pos_ref, h1_ref, xs_hbm, zrow, sem, psem):
    i = pl.program_id(0)
    groups = h1_ref.shape[0]
    dt = groups * SUBLANES

    for k in range(TOP_K):
        def body(g, c):
            for u in range(SUBLANES):
                dst = pos_ref[0, 0, k * dt + g * SUBLANES + u]
                pltpu.make_async_copy(h1_ref.at[g, pl.ds(u, 1)], xs_hbm.at[pl.ds(dst, 1)], sem.at[0]).start()
            return c
        lax.fori_loop(0, groups, body, 0)

    @pl.when(i == 0)
    def _():
        zrow[...] = jnp.zeros_like(zrow)

        def per_expert(e, c):
            start = ps_ref[e]

            def pad_copy(j):
                return pltpu.make_async_copy(zrow.at[pl.ds(0, 1)], xs_hbm.at[pl.ds(start + j, 1)], psem.at[0])

            def go(j, c2):
                pad_copy(j).start()
                return c2

            def done(j, c2):
                pad_copy(j).wait()
                return c2

            lax.fori_loop(0, pn_ref[e], go, 0)
            lax.fori_loop(0, pn_ref[e], done, 0)
            return c

        lax.fori_loop(0, N_EXPERTS, per_expert, 0)

    for k in range(TOP_K):
        pltpu.make_async_copy(h1_ref, h1_ref, sem.at[0]).wait()


def _dispatch(pad_start, pad_n, pos3, h1, n_rows):
    t = h1.shape[0]
    dt = pos3.shape[2] // TOP_K
    grid_spec = pltpu.PrefetchScalarGridSpec(
        num_scalar_prefetch=2,
        grid=(t // dt,),
        in_specs=[pl.BlockSpec((1, 1, TOP_K * dt), lambda i, ps, pn: (i, 0, 0), memory_space=pltpu.SMEM),
                  pl.BlockSpec((dt // SUBLANES, SUBLANES, D_MODEL), lambda i, ps, pn: (i, 0, 0))],
        out_specs=pl.BlockSpec(memory_space=pl.ANY),
        scratch_shapes=[pltpu.VMEM((SUBLANES, D_MODEL), _F32),
                        pltpu.SemaphoreType.DMA((1,)), pltpu.SemaphoreType.DMA((1,))],
    )
    return pl.pallas_call(
        _dispatch_kernel,
        grid_spec=grid_spec,
        out_shape=jax.ShapeDtypeStruct((n_rows, D_MODEL), _F32),
        compiler_params=_params("arbitrary"),
        name="dispatch_rows",
    )(pad_start, pad_n, pos3, h1.reshape(t // SUBLANES, SUBLANES, D_MODEL))


def _moe_kernel(te_ref, tns_ref, src_ref, x_ref, w1g_ref, w1l_ref, b1g_ref, b1l_ref, w2_ref, b2_ref,
                o_ref, xb):
    i = pl.program_id(0)
    f = pl.program_id(1)
    nsub = tns_ref[i]

    @pl.when(nsub > 0)
    def _active():
        full = nsub == MOE_SUBS_PER_TILE
        b1g = b1g_ref[0]
        b1l = b1l_ref[0]

        def compute(r0, n, first):
            rows = pl.ds(r0, n)
            if first:
                x = x_ref[rows, :].astype(_BF16)
                xb[rows, :] = x
            else:
                x = xb[rows, :]
            glu = jnp.minimum(_dot(x, w1g_ref[0].astype(_BF16)) + b1g, SWIGLU_LIMIT)
            lin = jnp.clip(_dot(x, w1l_ref[0].astype(_BF16)) + b1l, -SWIGLU_LIMIT, SWIGLU_LIMIT)
            act = glu * jax.nn.sigmoid(SWIGLU_ALPHA * glu) * (lin + 1.0)
            part = _dot(act.astype(_BF16), w2_ref[0].astype(_BF16))
            if first:
                o_ref[rows, :] = part + b2_ref[0]
            else:
                o_ref[rows, :] += part

        @pl.when(full & (f == 0))
        def _():
            compute(0, MOE_TILE, True)

        @pl.when(full & (f > 0))
        def _():
            compute(0, MOE_TILE, False)

        @pl.when(jnp.logical_not(full) & (f == 0))
        def _():
            def sub(s, c):
                rows = pl.ds(pl.multiple_of(s * MOE_SUB, MOE_SUB), MOE_SUB)
                xb[rows, :] = x_ref[rows, :].astype(_BF16)
                return c
            lax.fori_loop(0, nsub, sub, 0)
            o_ref[...] = jnp.broadcast_to(b2_ref[0], o_ref.shape)

        @pl.when((nsub & 2) != 0)
        def _():
            compute(0, 2 * MOE_SUB, False)

        @pl.when((nsub & 5) == 1)
        def _():
            compute(pl.multiple_of((nsub & 2) * MOE_SUB, MOE_SUB), MOE_SUB, False)


def _moe(tile_e, tile_nsub, tile_src, xs, w1, b1, w2, b2):
    n_tiles = tile_e.shape[0]
    nf = D_FF // MOE_FF_CHUNK
    last = nf - 1

    def ff(i, f, tns):
        return jnp.where(tns[i] > 0, f, last)

    grid_spec = pltpu.PrefetchScalarGridSpec(
        num_scalar_prefetch=3,
        grid=(n_tiles, nf),
        in_specs=[
            pl.BlockSpec((MOE_TILE, D_MODEL), lambda i, f, te, tns, src: (src[i], 0)),
            pl.BlockSpec((1, D_MODEL, MOE_FF_CHUNK), lambda i, f, te, tns, src: (te[i], 0, ff(i, f, tns))),
            pl.BlockSpec((1, D_MODEL, MOE_FF_CHUNK), lambda i, f, te, tns, src: (te[i], 0, nf + ff(i, f, tns))),
            pl.BlockSpec((1, 1, MOE_FF_CHUNK), lambda i, f, te, tns, src: (te[i], 0, ff(i, f, tns))),
            pl.BlockSpec((1, 1, MOE_FF_CHUNK), lambda i, f, te, tns, src: (te[i], 0, nf + ff(i, f, tns))),
            pl.BlockSpec((1, MOE_FF_CHUNK, D_MODEL), lambda i, f, te, tns, src: (te[i], ff(i, f, tns), 0)),
            pl.BlockSpec((1, 1, D_MODEL), lambda i, f, te, tns, src: (te[i], 0, 0)),
        ],
        out_specs=pl.BlockSpec((MOE_TILE, D_MODEL), lambda i, f, te, tns, src: (src[i], 0)),
        scratch_shapes=[pltpu.VMEM((MOE_TILE, D_MODEL), _BF16)],
    )
    return pl.pallas_call(
        _moe_kernel,
        grid_spec=grid_spec,
        out_shape=jax.ShapeDtypeStruct(xs.shape, _F32),
        compiler_params=_params("arbitrary", "arbitrary"),
        name="moe_experts",
    )(tile_e, tile_nsub, tile_src, xs, w1, w1, b1, b1, w2, b2)


def _combine_kernel(pos_ref, posn_ref, h1_ref, gt_ref, g_ref, b_ref, ys_hbm, o_ref, rows, sem, *, n_steps):
    i = pl.program_id(0)
    ct = h1_ref.shape[0]
    cur = lax.rem(i, 2)

    def row_copy(p_ref, buf, k, g, u):
        src_row = p_ref[0, 0, k * ct + g * SUBLANES + u]
        return pltpu.make_async_copy(ys_hbm.at[pl.ds(src_row, 1)], rows.at[buf, k, g, pl.ds(u, 1)], sem.at[buf])

    def fetch(p_ref, buf, slots):
        for k in slots:
            def body(g, c):
                for u in range(SUBLANES):
                    row_copy(p_ref, buf, k, g, u).start()
                return c
            lax.fori_loop(0, ct // SUBLANES, body, 0)

    def wait_rows(buf):
        for k in range(TOP_K):
            pltpu.make_async_copy(rows.at[1 - buf, k], rows.at[buf, k], sem.at[buf]).wait()

    @pl.when(i == 0)
    def _():
        fetch(pos_ref, 0, range(TOP_K))

    fetch(posn_ref, 1 - cur, range(TOP_K - 1))
    wait_rows(cur)

    gates = jnp.transpose(gt_ref[...])
    expert_rows = lambda k: rows[cur, k].reshape(ct, D_MODEL)
    ffn = expert_rows(0) * gates[:, 0:1]
    for k in range(1, TOP_K):
        ffn = ffn + expert_rows(k) * gates[:, k:k + 1]
    for g in range(ct // SUBLANES):
        for u in range(SUBLANES):
            row_copy(posn_ref, 1 - cur, TOP_K - 1, g, u).start()
    o_ref[...] = _layer_norm(DN_ALPHA * h1_ref[...] + ffn, g_ref[...], b_ref[...])

    @pl.when(i == n_steps - 1)
    def _():
        wait_rows(1 - cur)


def _combine(h1, ys, pos3, gates_t, g2, b2, tok0, ntok):
    ct = pos3.shape[2] // TOP_K
    assert tok0 % ct == 0 and ntok % ct == 0
    off = tok0 // ct
    n_steps = ntok // ct
    return pl.pallas_call(
        functools.partial(_combine_kernel, n_steps=n_steps),
        grid=(n_steps,),
        in_specs=[pl.BlockSpec((1, 1, TOP_K * ct), lambda i: (off + i, 0, 0), memory_space=pltpu.SMEM),
                  pl.BlockSpec((1, 1, TOP_K * ct), lambda i: (off + jnp.minimum(i + 1, n_steps - 1), 0, 0),
                               memory_space=pltpu.SMEM),
                  pl.BlockSpec((ct, D_MODEL), lambda i: (off + i, 0)),
                  pl.BlockSpec((SUBLANES, ct), lambda i: (0, off + i)),
                  _const_spec((1, D_MODEL)), _const_spec((1, D_MODEL)),
                  pl.BlockSpec(memory_space=pl.ANY)],
        out_specs=pl.BlockSpec((ct, D_MODEL), lambda i: (i, 0)),
        out_shape=jax.ShapeDtypeStruct((ntok, D_MODEL), _F32),
        scratch_shapes=[pltpu.VMEM((2, TOP_K, ct // SUBLANES, SUBLANES, D_MODEL), _F32),
                        pltpu.SemaphoreType.DMA((2,))],
        compiler_params=_params("arbitrary"),
        name="combine_ln2",
    )(pos3, pos3, h1, gates_t, g2, b2, ys)


def _t5_bucket(rel):
    half = N_BUCKETS // 2
    exact = half // 2
    n = jnp.abs(rel)
    large = exact + (jnp.log(jnp.maximum(n, 1).astype(_F32) / exact)
                     / math.log(MAX_DISTANCE / exact) * (half - exact)).astype(jnp.int32)
    large = jnp.minimum(large, half - 1)
    return jnp.where(rel > 0, half, 0) + jnp.where(n < exact, n, large)


def _lookup(table, idx):
    out = jnp.zeros(idx.shape + table.shape[1:], table.dtype)
    expand = (...,) + (None,) * (table.ndim - 1)
    for j in range(table.shape[0]):
        out = out + jnp.where((idx == j)[expand], table[j], jnp.zeros((), table.dtype))
    return out


def _bias_tables(rel_bias, sink, nblk):
    rb = rel_bias.astype(_F32)
    qi = jnp.arange(BLOCK)
    kj = jnp.arange(3 * BLOCK)
    rel_band = kj[None, :] - BLOCK - qi[:, None]
    band = _lookup(rb, _t5_bucket(rel_band)).transpose(2, 0, 1)
    band = jnp.where((jnp.abs(rel_band) <= WINDOW)[None], band, NEG_INF)
    dist = jnp.arange(1, nblk * BLOCK + N_META + 1)
    by_dist = _lookup(rb, _t5_bucket(-dist))
    s = nblk * BLOCK
    meta = jnp.stack([by_dist[N_META - m - 1:N_META - m - 1 + s] for m in range(N_META)], axis=1)
    meta = meta.reshape(nblk, BLOCK, N_META, N_HEADS).transpose(0, 3, 1, 2)
    sink_col = jnp.broadcast_to(sink.astype(_F32)[None, :, None, None], (nblk, N_HEADS, BLOCK, 1))
    pad = jnp.zeros((nblk, N_HEADS, BLOCK, BLOCK - N_META - 1), _F32)
    return band, jnp.concatenate([meta, sink_col, pad], axis=-1)


def _dispatch_tables(top_e, rank, counts, t, n_tiles):
    c = counts.astype(jnp.int32)
    nb = (c + MOE_SUB - 1) // MOE_SUB
    ns = (nb + MOE_SUBS_PER_TILE - 1) // MOE_SUBS_PER_TILE
    ends = jnp.cumsum(ns)
    first_tile = ends - ns

    pos = _lookup(first_tile, top_e[:TOP_K]) * MOE_TILE + rank[:TOP_K]
    pos3 = pos.reshape(TOP_K, t // ROW_DMA_TOKENS, ROW_DMA_TOKENS).transpose(1, 0, 2).reshape(
        t // ROW_DMA_TOKENS, 1, TOP_K * ROW_DMA_TOKENS)
    pad_start = first_tile * MOE_TILE + c
    pad_n = nb * MOE_SUB - c

    tid = jnp.arange(n_tiles, dtype=jnp.int32)
    n_used = ends[-1]
    used = tid < n_used
    te = jnp.minimum(jnp.sum((tid[:, None] >= ends[None, :]).astype(jnp.int32), axis=1), N_EXPERTS - 1)
    last_used = jnp.maximum(n_used - 1, 0)
    te = jnp.where(used, te, te[last_used])
    jt = tid - first_tile[te]
    nsub = jnp.where(used, jnp.clip(nb[te] - jt * MOE_SUBS_PER_TILE, 0, MOE_SUBS_PER_TILE), 0)
    src = jnp.where(used, tid, last_used)
    return te, nsub.astype(jnp.int32), src.astype(jnp.int32), pos3, pad_start, pad_n


def kernel(x_prompt, x_sample, meta_tokens, ln_in_g, ln_in_b, rel_bias, w_in, conv_w, conv_b, lru_wa, lru_ba,
           lru_wi, lru_bi, lru_lam, attn_sink, w_out, ln1_g, ln1_b, router_w, router_b, exp_w1, exp_b1,
           exp_w2, exp_b2, ln2_g, ln2_b):
    assert DEPTH == 1 and w_in.shape[0] == 1
    row = lambda a: a.reshape(1, -1).astype(_F32)
    g0, b0 = row(ln_in_g), row(ln_in_b)
    w_in_b = w_in[0].astype(_BF16)
    w_out_b = w_out[0].astype(_BF16)

    wa, wi = lru_wa[0], lru_wi[0]
    w_cat = jnp.concatenate([wa[0], wa[1], wi[0], wi[1]], axis=-1).astype(_BF16)
    blk = lambda v: v.reshape(LRU_BLOCKS, 1, LRU_BLOCK_W)
    b_cat = jnp.concatenate([blk(lru_ba[0, 0]), blk(lru_ba[0, 1]), blk(lru_bi[0, 0]), blk(lru_bi[0, 1])],
                            axis=-1).astype(_F32)
    c_decay = -LRU_C * jax.nn.softplus(-lru_lam[0].astype(_F32))

    groups = [x_prompt, x_sample]
    nblk_max = max(x.shape[1] for x in groups) // BLOCK
    band_bias, tail_bias = _bias_tables(rel_bias, attn_sink[0], nblk_max)

    _, k_m, v_m, xr_m, _ = _in_proj(meta_tokens.astype(_F32), g0, b0, w_in_b)
    tail = lambda a: jnp.zeros((N_KV_HEADS, BLOCK, HEAD_DIM), _BF16).at[:, :N_META].set(
        a.reshape(N_META, N_KV_HEADS, HEAD_DIM).transpose(1, 0, 2))
    k_tail, v_tail = tail(k_m), tail(v_m)

    xs, attns, recs = [], [], []
    for x in groups:
        bsz, s = x.shape[0], x.shape[1]
        x2 = x.reshape(bsz * s, D_MODEL)
        q, k, v, xr, yg = _in_proj(x2, g0, b0, w_in_b)
        attn = _attention(q.reshape(bsz, s, Q_COLS), k.reshape(bsz, s, KV_COLS), v.reshape(bsz, s, KV_COLS),
                          k_tail, v_tail, band_bias, tail_bias)
        rec = _rg_lru(xr.reshape(bsz, s, LRU_WIDTH), yg.reshape(bsz, s, LRU_WIDTH), xr_m,
                      conv_w[0].astype(_F32), row(conv_b[0]), w_cat, b_cat, c_decay)
        xs.append(x2)
        attns.append(attn.reshape(bsz * s, Q_COLS))
        recs.append(rec.reshape(bsz * s, LRU_WIDTH))

    rw_t = router_w[0].T.astype(_BF16)
    rb = jnp.broadcast_to(router_b[0].astype(_F32)[:, None], (N_EXPERTS, LANES))
    h1, top_e, gates_t, rank, counts = _mix(xs[0], xs[1], attns[0], attns[1], recs[0], recs[1], g0, b0,
                                            w_out_b, row(ln1_g[0]), row(ln1_b[0]), rw_t, rb)

    t = h1.shape[0]
    n_tiles = TOP_K * t // MOE_TILE + N_EXPERTS
    te, nsub, src_tile, pos3, pad_start, pad_n = _dispatch_tables(top_e, rank, counts[:, 0], t, n_tiles)
    xs = _dispatch(pad_start, pad_n, pos3, h1, n_tiles * MOE_TILE)
    ys = _moe(te, nsub, src_tile, xs,
              exp_w1[0], exp_b1[0].reshape(N_EXPERTS, 1, 2 * D_FF), exp_w2[0],
              exp_b2[0].reshape(N_EXPERTS, 1, D_MODEL))

    outs = []
    tok0 = 0
    for x in groups:
        bsz, s = x.shape[0], x.shape[1]
        y = _combine(h1, ys, pos3, gates_t, row(ln2_g[0]), row(ln2_b[0]), tok0, bsz * s)
        outs.append(y.reshape(bsz, s, D_MODEL))
        tok0 += bsz * s
    return tuple(outs)
```

```python
import functools
import math

import jax
import jax.numpy as jnp
from jax import lax
from jax.experimental import pallas as pl
from jax.experimental.pallas import tpu as pltpu

D_MODEL = 2048
HEAD_DIM = 128
N_HEADS = 8
N_KV_HEADS = 2
GQA_GROUP = N_HEADS // N_KV_HEADS
Q_COLS = N_HEADS * HEAD_DIM
KV_COLS = N_KV_HEADS * HEAD_DIM
LRU_WIDTH = 1024
LRU_BLOCKS = 8
LRU_BLOCK_W = LRU_WIDTH // LRU_BLOCKS
LRU_C = 8.0
IN_COLS = Q_COLS + 2 * KV_COLS + 2 * LRU_WIDTH
WINDOW = 128
BLOCK = 128
N_BUCKETS = 32
MAX_DISTANCE = 128
N_META = 16
N_EXPERTS = 32
TOP_K = 4
D_FF = D_MODEL
SWIGLU_LIMIT = 7.0
SWIGLU_ALPHA = 1.702
DEPTH = 1
DN_ALPHA = (2.0 * DEPTH) ** 0.25
LN_EPS = 1e-5
NEG_INF = -1e30

SUBLANES = 8
LANES = 128
VMEM_LIMIT_BYTES = 56 * 1024 * 1024

TOKEN_TILE = 512
LRU_CHUNK = 512
MOE_SUB = 256
MOE_SUBS_PER_TILE = 4
MOE_TILE = MOE_SUB * MOE_SUBS_PER_TILE
MOE_FF_CHUNK = 256
DISPATCH_TOKENS = 512
COMBINE_TOKENS = 256

_BF16 = jnp.bfloat16
_F32 = jnp.float32


def _layer_norm(x, g, b):
    mu = jnp.mean(x, axis=-1, keepdims=True)
    xc = x - mu
    var = jnp.mean(xc * xc, axis=-1, keepdims=True)
    return xc * lax.rsqrt(var + LN_EPS) * g + b


def _dot(a, b):
    return jnp.dot(a, b, preferred_element_type=_F32)


def _dot_nt(a, b):
    return lax.dot_general(a, b, (((1,), (1,)), ((), ())), preferred_element_type=_F32)


def _params(*semantics):
    return pltpu.CompilerParams(dimension_semantics=semantics, vmem_limit_bytes=VMEM_LIMIT_BYTES)


def _const_spec(shape):
    nd = len(shape)
    return pl.BlockSpec(shape, lambda *_: (0,) * nd, pipeline_mode=pl.Buffered(1))


def _in_proj_kernel(x_ref, g_ref, b_ref, w_ref, q_ref, k_ref, v_ref, xr_ref, yg_ref):
    h = _layer_norm(x_ref[...], g_ref[...], b_ref[...]).astype(_BF16)
    c0, c1, c2, c3 = Q_COLS, Q_COLS + KV_COLS, Q_COLS + 2 * KV_COLS, Q_COLS + 2 * KV_COLS + LRU_WIDTH
    q_ref[...] = _dot(h, w_ref[:, 0:c0]).astype(_BF16)
    k_ref[...] = _dot(h, w_ref[:, c0:c1]).astype(_BF16)
    v_ref[...] = _dot(h, w_ref[:, c1:c2]).astype(_BF16)
    xr_ref[...] = _dot(h, w_ref[:, c2:c3])
    yg_ref[...] = _dot(h, w_ref[:, c3:IN_COLS])


def _in_proj(x, g, b, w_bf16):
    t = x.shape[0]
    tm = min(TOKEN_TILE, t)
    assert t % tm == 0
    row = lambda n: pl.BlockSpec((tm, n), lambda i: (i, 0))
    return pl.pallas_call(
        _in_proj_kernel,
        grid=(t // tm,),
        in_specs=[row(D_MODEL), _const_spec((1, D_MODEL)), _const_spec((1, D_MODEL)),
                  _const_spec((D_MODEL, IN_COLS))],
        out_specs=[row(Q_COLS), row(KV_COLS), row(KV_COLS), row(LRU_WIDTH), row(LRU_WIDTH)],
        out_shape=[jax.ShapeDtypeStruct((t, Q_COLS), _BF16),
                   jax.ShapeDtypeStruct((t, KV_COLS), _BF16),
                   jax.ShapeDtypeStruct((t, KV_COLS), _BF16),
                   jax.ShapeDtypeStruct((t, LRU_WIDTH), _F32),
                   jax.ShapeDtypeStruct((t, LRU_WIDTH), _F32)],
        compiler_params=_params("arbitrary"),
        name="in_proj",
    )(x, g, b, w_bf16)


def _attn_kernel(q_ref, kp_ref, kc_ref, kn_ref, vp_ref, vc_ref, vn_ref, kt_ref, vt_ref,
                 bb_ref, tb_ref, o_ref, *, nblk):
    i = pl.program_id(1)
    kj = lax.broadcasted_iota(jnp.int32, (1, 3 * BLOCK), 1)
    valid = ((kj >= BLOCK) | (i > 0)) & ((kj < 2 * BLOCK) | (i < nblk - 1))
    tail_valid = lax.broadcasted_iota(jnp.int32, (1, BLOCK), 1) <= N_META
    scale = HEAD_DIM ** -0.5
    gw = GQA_GROUP * HEAD_DIM
    for h in range(N_KV_HEADS):
        cols = slice(h * HEAD_DIM, (h + 1) * HEAD_DIM)
        kb = jnp.concatenate([kp_ref[0, :, cols], kc_ref[0, :, cols], kn_ref[0, :, cols]], axis=0)
        vb = jnp.concatenate([vp_ref[0, :, cols], vc_ref[0, :, cols], vn_ref[0, :, cols]], axis=0)
        kt = kt_ref[h]
        vt = vt_ref[h]
        qs = jnp.concatenate([q_ref[0, :, h * gw + g * HEAD_DIM:h * gw + (g + 1) * HEAD_DIM]
                              for g in range(GQA_GROUP)], axis=0)
        band_bias = bb_ref[h * GQA_GROUP:(h + 1) * GQA_GROUP].reshape(GQA_GROUP * BLOCK, 3 * BLOCK)
        tail_bias = tb_ref[0, h * GQA_GROUP:(h + 1) * GQA_GROUP].reshape(GQA_GROUP * BLOCK, BLOCK)
        s_b = jnp.where(valid, _dot_nt(qs, kb) * scale + band_bias, NEG_INF)
        s_t = jnp.where(tail_valid, _dot_nt(qs, kt) * scale + tail_bias, NEG_INF)
        m = jnp.maximum(jnp.max(s_b, axis=-1, keepdims=True), jnp.max(s_t, axis=-1, keepdims=True))
        e_b = jnp.exp(s_b - m)
        e_t = jnp.exp(s_t - m)
        den = jnp.sum(e_b, axis=-1, keepdims=True) + jnp.sum(e_t, axis=-1, keepdims=True)
        inv = 1.0 / den
        p_b = (e_b * inv).astype(_BF16)
        p_t = (e_t * inv).astype(_BF16)
        o = (_dot(p_b, vb) + _dot(p_t, vt)).astype(_BF16)
        for g in range(GQA_GROUP):
            o_ref[0, :, h * gw + g * HEAD_DIM:h * gw + (g + 1) * HEAD_DIM] = o[g * BLOCK:(g + 1) * BLOCK]


def _attention(q, k, v, k_tail, v_tail, band_bias, tail_bias):
    bsz, s = q.shape[0], q.shape[1]
    nblk = s // BLOCK
    kv_spec = lambda d: pl.BlockSpec(
        (1, BLOCK, KV_COLS), lambda b, i: (b, jnp.clip(i + d, 0, nblk - 1), 0))
    tail_spec = pl.BlockSpec((N_KV_HEADS, BLOCK, HEAD_DIM), lambda b, i: (0, 0, 0))
    return pl.pallas_call(
        functools.partial(_attn_kernel, nblk=nblk),
        grid=(bsz, nblk),
        in_specs=[pl.BlockSpec((1, BLOCK, Q_COLS), lambda b, i: (b, i, 0)),
                  kv_spec(-1), kv_spec(0), kv_spec(1), kv_spec(-1), kv_spec(0), kv_spec(1),
                  tail_spec, tail_spec,
                  pl.BlockSpec((N_HEADS, BLOCK, 3 * BLOCK), lambda b, i: (0, 0, 0)),
                  pl.BlockSpec((1, N_HEADS, BLOCK, BLOCK), lambda b, i: (i, 0, 0, 0))],
        out_specs=pl.BlockSpec((1, BLOCK, Q_COLS), lambda b, i: (b, i, 0)),
        out_shape=jax.ShapeDtypeStruct((bsz, s, Q_COLS), _BF16),
        compiler_params=_params("arbitrary", "arbitrary"),
        name="attention",
    )(q, k, k, k, v, v, v, k_tail, v_tail, band_bias, tail_bias)


_X0 = 24


def _tile_scan(a, u, h_in, reverse):
    n = a.shape[0]
    nt = n // SUBLANES
    a3 = a.reshape(nt, SUBLANES, LANES)
    u3 = u.reshape(nt, SUBLANES, LANES)
    row = lax.broadcasted_iota(jnp.int32, (nt, SUBLANES, LANES), 1)
    for step in (1, 2, 4):
        shift = (SUBLANES - step) if reverse else step
        a_sh = pltpu.roll(a3, shift, 1)
        u_sh = pltpu.roll(u3, shift, 1)
        ok = (row < SUBLANES - step) if reverse else (row >= step)
        u3 = jnp.where(ok, a3 * u_sh + u3, u3)
        a3 = jnp.where(ok, a3 * a_sh, a3)
    out = [None] * nt
    h = h_in
    order = range(nt - 1, -1, -1) if reverse else range(nt)
    edge = 0 if reverse else SUBLANES - 1
    for j in order:
        ht = u3[j] + a3[j] * h
        out[j] = ht
        h = ht[edge:edge + 1]
    return jnp.concatenate(out, axis=0), h


def _chunk_scan(a_ref, u_ref, o_ref, a0, o0, h_in, reverse):
    nt = LRU_CHUNK // SUBLANES
    order = range(SUBLANES - 1, -1, -1) if reverse else range(SUBLANES)
    rows = lambda ref, base, s: ref[pl.ds(base + s, nt, stride=SUBLANES), :]
    hs, ps = {}, {}
    prev = None
    for s in order:
        a_s = rows(a_ref, a0, s)
        u_s = rows(u_ref, a0, s)
        hs[s] = u_s if prev is None else a_s * hs[prev] + u_s
        ps[s] = a_s if prev is None else a_s * ps[prev]
        prev = s
    ends, h_out = _tile_scan(ps[prev], hs[prev], h_in, reverse)
    tile = lax.broadcasted_iota(jnp.int32, (nt, LANES), 0)
    if reverse:
        carry = jnp.where(tile == nt - 1, h_in, pltpu.roll(ends, nt - 1, 0))
    else:
        carry = jnp.where(tile == 0, h_in, pltpu.roll(ends, 1, 0))
    for s in order:
        o_ref[pl.ds(o0 + s, nt, stride=SUBLANES), :] = hs[s] + ps[s] * carry
    return h_out


def _lru_kernel(xr_ref, yg_ref, xm_ref, cw_ref, cb_ref, w_ref, b_ref, c_ref, o_ref,
                xext, hf, ab, ub, sa, su, *, seq):
    nchunk = seq // LRU_CHUNK
    zeros8 = jnp.zeros((SUBLANES, LANES), _F32)
    xext[0:SUBLANES, :] = zeros8
    xext[SUBLANES:_X0, :] = xm_ref[...]
    xext[_X0:_X0 + seq, :] = xr_ref[0]
    xext[_X0 + seq:_X0 + seq + SUBLANES, :] = zeros8
    cw = cw_ref[...]
    cb = cb_ref[...]
    w = w_ref[0]
    bias = b_ref[0]
    c_f = c_ref[0:1, :]
    c_b = c_ref[1:2, :]

    def gates(r0, n):
        win = xext[pl.ds(r0 - SUBLANES, n + 2 * SUBLANES), :]
        span = n + 2 * SUBLANES
        tap = lambda d: pltpu.roll(win, (span - d) % span, 0)[SUBLANES:SUBLANES + n]
        xc = cb + cw[0:1] * tap(-2)
        xc = xc + cw[1:2] * tap(-1)
        xc = xc + cw[2:3] * win[SUBLANES:SUBLANES + n]
        xc = xc + cw[3:4] * tap(1)
        g = _dot(xc.astype(_BF16), w) + bias
        r_f = jax.nn.sigmoid(g[:, 0:LANES])
        r_b = jax.nn.sigmoid(g[:, LANES:2 * LANES])
        i_f = jax.nn.sigmoid(g[:, 2 * LANES:3 * LANES])
        i_b = jax.nn.sigmoid(g[:, 3 * LANES:4 * LANES])
        la_f = c_f * r_f
        la_b = c_b * r_b
        a_f = jnp.exp(la_f)
        a_b = jnp.exp(la_b)
        root = lambda y: jnp.where(y > 0.0, y * lax.rsqrt(y), 0.0)
        u_f = root(1.0 - a_f * a_f) * (i_f * xc)
        u_b = root(1.0 - a_b * a_b) * (i_b * xc)
        return a_f, u_f, a_b, u_b

    a_f, u_f, _, _ = gates(SUBLANES, N_META)
    _, h0 = _tile_scan(a_f, u_f, jnp.zeros((1, LANES), _F32), reverse=False)

    def fwd(c, h):
        t0 = pl.multiple_of(c * LRU_CHUNK, LRU_CHUNK)
        a_f, u_f, a_b, u_b = gates(_X0 + t0, LRU_CHUNK)
        sa[...] = a_f
        su[...] = u_f
        ab[pl.ds(t0, LRU_CHUNK), :] = a_b
        ub[pl.ds(t0, LRU_CHUNK), :] = u_b
        return _chunk_scan(sa, su, hf, 0, t0, h, reverse=False)

    lax.fori_loop(0, nchunk, fwd, h0)

    def bwd(cc, h):
        c = nchunk - 1 - cc
        t0 = pl.multiple_of(c * LRU_CHUNK, LRU_CHUNK)
        h = _chunk_scan(ab, ub, sa, t0, 0, h, reverse=True)
        tot = hf[pl.ds(t0, LRU_CHUNK), :] + sa[...]
        o_ref[0, pl.ds(t0, LRU_CHUNK), :] = (tot * jax.nn.gelu(yg_ref[0, pl.ds(t0, LRU_CHUNK), :])).astype(_BF16)
        return h

    lax.fori_loop(0, nchunk, bwd, jnp.zeros((1, LANES), _F32))


def _rg_lru(xr, yg, xr_meta, conv_w, conv_b, w_cat, b_cat, c_decay):
    bsz, s = xr.shape[0], xr.shape[1]
    assert s % LRU_CHUNK == 0
    slab = pl.BlockSpec((1, s, LANES), lambda b, n: (b, 0, n))
    col = lambda r: pl.BlockSpec((r, LANES), lambda b, n: (0, n))
    return pl.pallas_call(
        functools.partial(_lru_kernel, seq=s),
        grid=(bsz, LRU_BLOCKS),
        in_specs=[slab, slab, col(N_META), col(4), col(1),
                  pl.BlockSpec((1, LANES, 4 * LANES), lambda b, n: (n, 0, 0)),
                  pl.BlockSpec((1, 1, 4 * LANES), lambda b, n: (n, 0, 0)),
                  col(2)],
        out_specs=slab,
        out_shape=jax.ShapeDtypeStruct((bsz, s, LRU_WIDTH), _BF16),
        scratch_shapes=[pltpu.VMEM((s + _X0 + SUBLANES, LANES), _F32),
                        pltpu.VMEM((s, LANES), _F32),
                        pltpu.VMEM((s, LANES), _F32),
                        pltpu.VMEM((s, LANES), _F32),
                        pltpu.VMEM((LRU_CHUNK, LANES), _F32),
                        pltpu.VMEM((LRU_CHUNK, LANES), _F32)],
        compiler_params=_params("arbitrary", "arbitrary"),
        name="rg_lru",
    )(xr, yg, xr_meta, conv_w, conv_b, w_cat, b_cat, c_decay)


def _mix_kernel(xp_ref, xs_ref, ap_ref, as_ref, rp_ref, rs_ref, g0_ref, b0_ref, wo_ref, g1_ref, b1_ref,
                rwt_ref, rb_ref, h1_ref, te_ref, gt_ref, rk_ref, cnt_ref, carry, *, n_prompt_tiles, tm):
    i = pl.program_id(0)

    @pl.when(i == 0)
    def _():
        carry[...] = jnp.zeros_like(carry)

    first = i < n_prompt_tiles
    x = jnp.where(first, xp_ref[...], xs_ref[...])
    attn = jnp.where(first, ap_ref[...], as_ref[...])
    rec = jnp.where(first, rp_ref[...], rs_ref[...])
    h0 = _layer_norm(x, g0_ref[...], b0_ref[...])
    mix = _dot(attn, wo_ref[0:Q_COLS, :]) + _dot(rec, wo_ref[Q_COLS:Q_COLS + LRU_WIDTH, :])
    h1 = _layer_norm(DN_ALPHA * h0 + mix, g1_ref[...], b1_ref[...])
    h1_ref[...] = h1

    logits = _dot_nt(rwt_ref[...], h1.astype(_BF16)) + rb_ref[:, 0:1]
    eidx = lax.broadcasted_iota(jnp.int32, (N_EXPERTS, tm), 0)
    work = logits
    vals, idxs, hots = [], [], []
    for _ in range(TOP_K):
        m = jnp.max(work, axis=0, keepdims=True)
        idx = jnp.min(jnp.where(work == m, eidx, N_EXPERTS), axis=0, keepdims=True)
        hot = eidx == idx
        vals.append(m)
        idxs.append(idx)
        hots.append(hot)
        work = jnp.where(hot, -jnp.inf, work)
    exps = [jnp.exp(v - vals[0]) for v in vals]
    den = exps[0] + exps[1] + exps[2] + exps[3]
    hot_all = (hots[0] | hots[1] | hots[2] | hots[3])
    hot_f = jnp.where(hot_all, 1.0, 0.0).astype(_F32)
    upper = (lax.broadcasted_iota(jnp.int32, (tm, tm), 0) < lax.broadcasted_iota(jnp.int32, (tm, tm), 1))
    before = _dot(hot_f.astype(_BF16), jnp.where(upper, 1.0, 0.0).astype(_BF16))
    base = carry[:, 0:1] + before
    zero_rows = jnp.zeros((SUBLANES - TOP_K, tm), _F32)
    ranks = [jnp.sum(jnp.where(h, base, 0.0), axis=0, keepdims=True) for h in hots]
    te_ref[...] = jnp.concatenate(idxs + [zero_rows.astype(jnp.int32)], axis=0)
    gt_ref[...] = jnp.concatenate([e / den for e in exps] + [zero_rows], axis=0)
    rk_ref[...] = jnp.concatenate(ranks + [zero_rows], axis=0).astype(jnp.int32)
    carry[...] = carry[...] + jnp.sum(hot_f, axis=1, keepdims=True)
    cnt_ref[...] = carry[...]


def _mix(x_p, x_s, attn_p, attn_s, rec_p, rec_s, g0, b0, w_out_bf16, g1, b1, rw_t, rb):
    tp, ts = x_p.shape[0], x_s.shape[0]
    t = tp + ts
    tm = min(TOKEN_TILE, tp, ts)
    assert tp % tm == 0 and ts % tm == 0
    npt = tp // tm
    first = lambda n: pl.BlockSpec((tm, n), lambda i: (jnp.minimum(i, npt - 1), 0))
    second = lambda n: pl.BlockSpec((tm, n), lambda i: (jnp.maximum(i - npt, 0), 0))
    lane_row = pl.BlockSpec((SUBLANES, tm), lambda i: (0, i))
    return pl.pallas_call(
        functools.partial(_mix_kernel, n_prompt_tiles=npt, tm=tm),
        grid=(t // tm,),
        in_specs=[first(D_MODEL), second(D_MODEL), first(Q_COLS), second(Q_COLS),
                  first(LRU_WIDTH), second(LRU_WIDTH),
                  _const_spec((1, D_MODEL)), _const_spec((1, D_MODEL)),
                  _const_spec((D_MODEL, D_MODEL)),
                  _const_spec((1, D_MODEL)), _const_spec((1, D_MODEL)),
                  _const_spec((N_EXPERTS, D_MODEL)), _const_spec((N_EXPERTS, LANES))],
        out_specs=[pl.BlockSpec((tm, D_MODEL), lambda i: (i, 0)), lane_row, lane_row, lane_row,
                   pl.BlockSpec((N_EXPERTS, LANES), lambda i: (0, 0))],
        out_shape=[jax.ShapeDtypeStruct((t, D_MODEL), _F32),
                   jax.ShapeDtypeStruct((SUBLANES, t), jnp.int32),
                   jax.ShapeDtypeStruct((SUBLANES, t), _F32),
                   jax.ShapeDtypeStruct((SUBLANES, t), jnp.int32),
                   jax.ShapeDtypeStruct((N_EXPERTS, LANES), _F32)],
        scratch_shapes=[pltpu.VMEM((N_EXPERTS, LANES), _F32)],
        compiler_params=_params("arbitrary"),
        name="mix_router",
    )(x_p, x_s, attn_p, attn_s, rec_p, rec_s, g0, b0, w_out_bf16, g1, b1, rw_t, rb)


def _dispatch_kernel(ps_ref, pn_ref, pos_ref, h1_ref, xs_hbm, zrow, sem, psem):
    i = pl.program_id(0)
    groups = h1_ref.shape[0]
    dt = groups * SUBLANES

    for k in range(TOP_K):
        def body(g, c):
            for u in range(SUBLANES):
                dst = pos_ref[0, 0, k * dt + g * SUBLANES + u]
                pltpu.make_async_copy(h1_ref.at[g, pl.ds(u, 1)], xs_hbm.at[pl.ds(dst, 1)], sem.at[0]).start()
            return c
        lax.fori_loop(0, groups, body, 0)

    @pl.when(i == 0)
    def _():
        zrow[...] = jnp.zeros_like(zrow)

        def per_expert(e, c):
            start = ps_ref[e]

            def pad_copy(j):
                return pltpu.make_async_copy(zrow.at[pl.ds(0, 1)], xs_hbm.at[pl.ds(start + j, 1)], psem.at[0])

            def go(j, c2):
                pad_copy(j).start()
                return c2

            def done(j, c2):
                pad_copy(j).wait()
                return c2

            lax.fori_loop(0, pn_ref[e], go, 0)
            lax.fori_loop(0, pn_ref[e], done, 0)
            return c

        lax.fori_loop(0, N_EXPERTS, per_expert, 0)

    for k in range(TOP_K):
        pltpu.make_async_copy(h1_ref, h1_ref, sem.at[0]).wait()


def _dispatch(pad_start, pad_n, pos3, h1, n_rows):
    t = h1.shape[0]
    dt = pos3.shape[2] // TOP_K
    grid_spec = pltpu.PrefetchScalarGridSpec(
        num_scalar_prefetch=2,
        grid=(t // dt,),
        in_specs=[pl.BlockSpec((1, 1, TOP_K * dt), lambda i, ps, pn: (i, 0, 0), memory_space=pltpu.SMEM),
                  pl.BlockSpec((dt // SUBLANES, SUBLANES, D_MODEL), lambda i, ps, pn: (i, 0, 0))],
        out_specs=pl.BlockSpec(memory_space=pl.ANY),
        scratch_shapes=[pltpu.VMEM((SUBLANES, D_MODEL), _F32),
                        pltpu.SemaphoreType.DMA((1,)), pltpu.SemaphoreType.DMA((1,))],
    )
    return pl.pallas_call(
        _dispatch_kernel,
        grid_spec=grid_spec,
        out_shape=jax.ShapeDtypeStruct((n_rows, D_MODEL), _F32),
        compiler_params=_params("arbitrary"),
        name="dispatch_rows",
    )(pad_start, pad_n, pos3, h1.reshape(t // SUBLANES, SUBLANES, D_MODEL))


def _moe_kernel(te_ref, tns_ref, src_ref, x_ref, w1g_ref, w1l_ref, b1g_ref, b1l_ref, w2_ref, b2_ref,
                o_ref, xb):
    i = pl.program_id(0)
    f = pl.program_id(1)
    nsub = tns_ref[i]

    @pl.when(nsub > 0)
    def _active():
        full = nsub == MOE_SUBS_PER_TILE
        b1g = b1g_ref[0]
        b1l = b1l_ref[0]

        def compute(r0, n, first):
            rows = pl.ds(r0, n)
            if first:
                x = x_ref[rows, :].astype(_BF16)
                xb[rows, :] = x
            else:
                x = xb[rows, :]
            glu = jnp.minimum(_dot(x, w1g_ref[0].astype(_BF16)) + b1g, SWIGLU_LIMIT)
            lin = jnp.clip(_dot(x, w1l_ref[0].astype(_BF16)) + b1l, -SWIGLU_LIMIT, SWIGLU_LIMIT)
            act = glu * jax.nn.sigmoid(SWIGLU_ALPHA * glu) * (lin + 1.0)
            part = _dot(act.astype(_BF16), w2_ref[0].astype(_BF16))
            if first:
                o_ref[rows, :] = part + b2_ref[0]
            else:
                o_ref[rows, :] += part

        @pl.when(full & (f == 0))
        def _():
            compute(0, MOE_TILE, True)

        @pl.when(full & (f > 0))
        def _():
            compute(0, MOE_TILE, False)

        @pl.when(jnp.logical_not(full) & (f == 0))
        def _():
            def sub(s, c):
                rows = pl.ds(pl.multiple_of(s * MOE_SUB, MOE_SUB), MOE_SUB)
                xb[rows, :] = x_ref[rows, :].astype(_BF16)
                return c
            lax.fori_loop(0, nsub, sub, 0)
            o_ref[...] = jnp.broadcast_to(b2_ref[0], o_ref.shape)

        @pl.when((nsub & 2) != 0)
        def _():
            compute(0, 2 * MOE_SUB, False)

        @pl.when((nsub & 5) == 1)
        def _():
            compute(pl.multiple_of((nsub & 2) * MOE_SUB, MOE_SUB), MOE_SUB, False)


def _moe(tile_e, tile_nsub, tile_src, xs, w1, b1, w2, b2):
    n_tiles = tile_e.shape[0]
    nf = D_FF // MOE_FF_CHUNK
    last = nf - 1

    def ff(i, f, tns):
        return jnp.where(tns[i] > 0, f, last)

    grid_spec = pltpu.PrefetchScalarGridSpec(
        num_scalar_prefetch=3,
        grid=(n_tiles, nf),
        in_specs=[
            pl.BlockSpec((MOE_TILE, D_MODEL), lambda i, f, te, tns, src: (src[i], 0)),
            pl.BlockSpec((1, D_MODEL, MOE_FF_CHUNK), lambda i, f, te, tns, src: (te[i], 0, ff(i, f, tns))),
            pl.BlockSpec((1, D_MODEL, MOE_FF_CHUNK), lambda i, f, te, tns, src: (te[i], 0, nf + ff(i, f, tns))),
            pl.BlockSpec((1, 1, MOE_FF_CHUNK), lambda i, f, te, tns, src: (te[i], 0, ff(i, f, tns))),
            pl.BlockSpec((1, 1, MOE_FF_CHUNK), lambda i, f, te, tns, src: (te[i], 0, nf + ff(i, f, tns))),
            pl.BlockSpec((1, MOE_FF_CHUNK, D_MODEL), lambda i, f, te, tns, src: (te[i], ff(i, f, tns), 0)),
            pl.BlockSpec((1, 1, D_MODEL), lambda i, f, te, tns, src: (te[i], 0, 0)),
        ],
        out_specs=pl.BlockSpec((MOE_TILE, D_MODEL), lambda i, f, te, tns, src: (src[i], 0)),
        scratch_shapes=[pltpu.VMEM((MOE_TILE, D_MODEL), _BF16)],
    )
    return pl.pallas_call(
        _moe_kernel,
        grid_spec=grid_spec,
        out_shape=jax.ShapeDtypeStruct(xs.shape, _F32),
        compiler_params=_params("arbitrary", "arbitrary"),
        name="moe_experts",
    )(tile_e, tile_nsub, tile_src, xs, w1, w1, b1, b1, w2, b2)


def _combine_kernel(pos_ref, posn_ref, h1_ref, gt_ref, g_ref, b_ref, ys_hbm, o_ref, rows, sem, *, n_steps):
    i = pl.program_id(0)
    ct = h1_ref.shape[0]
    cur = lax.rem(i, 2)

    def row_copy(p_ref, buf, k, g, u):
        src_row = p_ref[0, 0, k * ct + g * SUBLANES + u]
        return pltpu.make_async_copy(ys_hbm.at[pl.ds(src_row, 1)], rows.at[buf, k, g, pl.ds(u, 1)], sem.at[buf])

    def fetch(p_ref, buf, slots):
        for k in slots:
            def body(g, c):
                for u in range(SUBLANES):
                    row_copy(p_ref, buf, k, g, u).start()
                return c
            lax.fori_loop(0, ct // SUBLANES, body, 0)

    def wait_rows(buf):
        for k in range(TOP_K):
            pltpu.make_async_copy(rows.at[1 - buf, k], rows.at[buf, k], sem.at[buf]).wait()

    @pl.when(i == 0)
    def _():
        fetch(pos_ref, 0, range(TOP_K))

    fetch(posn_ref, 1 - cur, range(TOP_K - 1))
    wait_rows(cur)

    gates = jnp.transpose(gt_ref[...])
    expert_rows = lambda k: rows[cur, k].reshape(ct, D_MODEL)
    ffn = expert_rows(0) * gates[:, 0:1]
    for k in range(1, TOP_K):
        ffn = ffn + expert_rows(k) * gates[:, k:k + 1]
    for g in range(ct // SUBLANES):
        for u in range(SUBLANES):
            row_copy(posn_ref, 1 - cur, TOP_K - 1, g, u).start()
    o_ref[...] = _layer_norm(DN_ALPHA * h1_ref[...] + ffn, g_ref[...], b_ref[...])

    @pl.when(i == n_steps - 1)
    def _():
        wait_rows(1 - cur)


def _combine(h1, ys, pos3, gates_t, g2, b2, tok0, ntok):
    ct = pos3.shape[2] // TOP_K
    assert tok0 % ct == 0 and ntok % ct == 0
    off = tok0 // ct
    n_steps = ntok // ct
    return pl.pallas_call(
        functools.partial(_combine_kernel, n_steps=n_steps),
        grid=(n_steps,),
        in_specs=[pl.BlockSpec((1, 1, TOP_K * ct), lambda i: (off + i, 0, 0), memory_space=pltpu.SMEM),
                  pl.BlockSpec((1, 1, TOP_K * ct), lambda i: (off + jnp.minimum(i + 1, n_steps - 1), 0, 0),
                               memory_space=pltpu.SMEM),
                  pl.BlockSpec((ct, D_MODEL), lambda i: (off + i, 0)),
                  pl.BlockSpec((SUBLANES, ct), lambda i: (0, off + i)),
                  _const_spec((1, D_MODEL)), _const_spec((1, D_MODEL)),
                  pl.BlockSpec(memory_space=pl.ANY)],
        out_specs=pl.BlockSpec((ct, D_MODEL), lambda i: (i, 0)),
        out_shape=jax.ShapeDtypeStruct((ntok, D_MODEL), _F32),
        scratch_shapes=[pltpu.VMEM((2, TOP_K, ct // SUBLANES, SUBLANES, D_MODEL), _F32),
                        pltpu.SemaphoreType.DMA((2,))],
        compiler_params=_params("arbitrary"),
        name="combine_ln2",
    )(pos3, pos3, h1, gates_t, g2, b2, ys)


def _t5_bucket(rel):
    half = N_BUCKETS // 2
    exact = half // 2
    n = jnp.abs(rel)
    large = exact + (jnp.log(jnp.maximum(n, 1).astype(_F32) / exact)
                     / math.log(MAX_DISTANCE / exact) * (half - exact)).astype(jnp.int32)
    large = jnp.minimum(large, half - 1)
    return jnp.where(rel > 0, half, 0) + jnp.where(n < exact, n, large)


def _lookup(table, idx):
    out = jnp.zeros(idx.shape + table.shape[1:], table.dtype)
    expand = (...,) + (None,) * (table.ndim - 1)
    for j in range(table.shape[0]):
        out = out + jnp.where((idx == j)[expand], table[j], jnp.zeros((), table.dtype))
    return out


def _bias_tables(rel_bias, sink, nblk):
    rb = rel_bias.astype(_F32)
    qi = jnp.arange(BLOCK)
    kj = jnp.arange(3 * BLOCK)
    rel_band = kj[None, :] - BLOCK - qi[:, None]
    band = _lookup(rb, _t5_bucket(rel_band)).transpose(2, 0, 1)
    band = jnp.where((jnp.abs(rel_band) <= WINDOW)[None], band, NEG_INF)
    dist = jnp.arange(1, nblk * BLOCK + N_META + 1)
    by_dist = _lookup(rb, _t5_bucket(-dist))
    s = nblk * BLOCK
    meta = jnp.stack([by_dist[N_META - m - 1:N_META - m - 1 + s] for m in range(N_META)], axis=1)
    meta = meta.reshape(nblk, BLOCK, N_META, N_HEADS).transpose(0, 3, 1, 2)
    sink_col = jnp.broadcast_to(sink.astype(_F32)[None, :, None, None], (nblk, N_HEADS, BLOCK, 1))
    pad = jnp.zeros((nblk, N_HEADS, BLOCK, BLOCK - N_META - 1), _F32)
    return band, jnp.concatenate([meta, sink_col, pad], axis=-1)


def _dispatch_tables(top_e, rank, counts, t, n_tiles):
    c = counts.astype(jnp.int32)
    nb = (c + MOE_SUB - 1) // MOE_SUB
    ns = (nb + MOE_SUBS_PER_TILE - 1) // MOE_SUBS_PER_TILE
    ends = jnp.cumsum(ns)
    first_tile = ends - ns

    pos = _lookup(first_tile, top_e[:TOP_K]) * MOE_TILE + rank[:TOP_K]
    blocks = lambda dt: pos.reshape(TOP_K, t // dt, dt).transpose(1, 0, 2).reshape(t // dt, 1, TOP_K * dt)
    pos3 = (blocks(DISPATCH_TOKENS), blocks(COMBINE_TOKENS))
    pad_start = first_tile * MOE_TILE + c
    pad_n = nb * MOE_SUB - c

    tid = jnp.arange(n_tiles, dtype=jnp.int32)
    n_used = ends[-1]
    used = tid < n_used
    te = jnp.minimum(jnp.sum((tid[:, None] >= ends[None, :]).astype(jnp.int32), axis=1), N_EXPERTS - 1)
    last_used = jnp.maximum(n_used - 1, 0)
    te = jnp.where(used, te, te[last_used])
    jt = tid - first_tile[te]
    nsub = jnp.where(used, jnp.clip(nb[te] - jt * MOE_SUBS_PER_TILE, 0, MOE_SUBS_PER_TILE), 0)
    src = jnp.where(used, tid, last_used)
    return te, nsub.astype(jnp.int32), src.astype(jnp.int32), pos3, pad_start, pad_n


def kernel(x_prompt, x_sample, meta_tokens, ln_in_g, ln_in_b, rel_bias, w_in, conv_w, conv_b, lru_wa, lru_ba,
           lru_wi, lru_bi, lru_lam, attn_sink, w_out, ln1_g, ln1_b, router_w, router_b, exp_w1, exp_b1,
           exp_w2, exp_b2, ln2_g, ln2_b):
    assert DEPTH == 1 and w_in.shape[0] == 1
    row = lambda a: a.reshape(1, -1).astype(_F32)
    g0, b0 = row(ln_in_g), row(ln_in_b)
    w_in_b = w_in[0].astype(_BF16)
    w_out_b = w_out[0].astype(_BF16)

    wa, wi = lru_wa[0], lru_wi[0]
    w_cat = jnp.concatenate([wa[0], wa[1], wi[0], wi[1]], axis=-1).astype(_BF16)
    blk = lambda v: v.reshape(LRU_BLOCKS, 1, LRU_BLOCK_W)
    b_cat = jnp.concatenate([blk(lru_ba[0, 0]), blk(lru_ba[0, 1]), blk(lru_bi[0, 0]), blk(lru_bi[0, 1])],
                            axis=-1).astype(_F32)
    c_decay = -LRU_C * jax.nn.softplus(-lru_lam[0].astype(_F32))

    groups = [x_prompt, x_sample]
    nblk_max = max(x.shape[1] for x in groups) // BLOCK
    band_bias, tail_bias = _bias_tables(rel_bias, attn_sink[0], nblk_max)

    _, k_m, v_m, xr_m, _ = _in_proj(meta_tokens.astype(_F32), g0, b0, w_in_b)
    tail = lambda a: jnp.zeros((N_KV_HEADS, BLOCK, HEAD_DIM), _BF16).at[:, :N_META].set(
        a.reshape(N_META, N_KV_HEADS, HEAD_DIM).transpose(1, 0, 2))
    k_tail, v_tail = tail(k_m), tail(v_m)

    xs, attns, recs = [], [], []
    for x in groups:
        bsz, s = x.shape[0], x.shape[1]
        x2 = x.reshape(bsz * s, D_MODEL)
        q, k, v, xr, yg = _in_proj(x2, g0, b0, w_in_b)
        attn = _attention(q.reshape(bsz, s, Q_COLS), k.reshape(bsz, s, KV_COLS), v.reshape(bsz, s, KV_COLS),
                          k_tail, v_tail, band_bias, tail_bias)
        rec = _rg_lru(xr.reshape(bsz, s, LRU_WIDTH), yg.reshape(bsz, s, LRU_WIDTH), xr_m,
                      conv_w[0].astype(_F32), row(conv_b[0]), w_cat, b_cat, c_decay)
        xs.append(x2)
        attns.append(attn.reshape(bsz * s, Q_COLS))
        recs.append(rec.reshape(bsz * s, LRU_WIDTH))

    rw_t = router_w[0].T.astype(_BF16)
    rb = jnp.broadcast_to(router_b[0].astype(_F32)[:, None], (N_EXPERTS, LANES))
    h1, top_e, gates_t, rank, counts = _mix(xs[0], xs[1], attns[0], attns[1], recs[0], recs[1], g0, b0,
                                            w_out_b, row(ln1_g[0]), row(ln1_b[0]), rw_t, rb)

    t = h1.shape[0]
    n_tiles = TOP_K * t // MOE_TILE + N_EXPERTS
    te, nsub, src_tile, pos3, pad_start, pad_n = _dispatch_tables(top_e, rank, counts[:, 0], t, n_tiles)
    xs = _dispatch(pad_start, pad_n, pos3[0], h1, n_tiles * MOE_TILE)
    ys = _moe(te, nsub, src_tile, xs,
              exp_w1[0], exp_b1[0].reshape(N_EXPERTS, 1, 2 * D_FF), exp_w2[0],
              exp_b2[0].reshape(N_EXPERTS, 1, D_MODEL))

    outs = []
    tok0 = 0
    for x in groups:
        bsz, s = x.shape[0], x.shape[1]
        y = _combine(h1, ys, pos3[1], gates_t, row(ln2_g[0]), row(ln2_b[0]), tok0, bsz * s)
        outs.append(y.reshape(bsz, s, D_MODEL))
        tok0 += bsz * s
    return tuple(outs)
```

```python
import functools
import math

import jax
import jax.numpy as jnp
from jax import lax
from jax.experimental import pallas as pl
from jax.experimental.pallas import tpu as pltpu

D_MODEL = 2048
HEAD_DIM = 128
N_HEADS = 8
N_KV_HEADS = 2
GQA_GROUP = N_HEADS // N_KV_HEADS
Q_COLS = N_HEADS * HEAD_DIM
KV_COLS = N_KV_HEADS * HEAD_DIM
LRU_WIDTH = 1024
LRU_BLOCKS = 8
LRU_BLOCK_W = LRU_WIDTH // LRU_BLOCKS
LRU_C = 8.0
IN_COLS = Q_COLS + 2 * KV_COLS + 2 * LRU_WIDTH
WINDOW = 128
BLOCK = 128
N_BUCKETS = 32
MAX_DISTANCE = 128
N_META = 16
N_EXPERTS = 32
TOP_K = 4
D_FF = D_MODEL
SWIGLU_LIMIT = 7.0
SWIGLU_ALPHA = 1.702
DEPTH = 1
DN_ALPHA = (2.0 * DEPTH) ** 0.25
LN_EPS = 1e-5
NEG_INF = -1e30

SUBLANES = 8
LANES = 128
VMEM_LIMIT_BYTES = 56 * 1024 * 1024

TOKEN_TILE = 512
LRU_CHUNK = 512
MOE_SUB = 256
MOE_SUBS_PER_TILE = 4
MOE_TILE = MOE_SUB * MOE_SUBS_PER_TILE
MOE_FF_CHUNK = 256
DISPATCH_TOKENS = 1024
COMBINE_TOKENS = 256

_BF16 = jnp.bfloat16
_F32 = jnp.float32


def _layer_norm(x, g, b):
    mu = jnp.mean(x, axis=-1, keepdims=True)
    xc = x - mu
    var = jnp.mean(xc * xc, axis=-1, keepdims=True)
    return xc * lax.rsqrt(var + LN_EPS) * g + b


def _dot(a, b):
    return jnp.dot(a, b, preferred_element_type=_F32)


def _dot_nt(a, b):
    return lax.dot_general(a, b, (((1,), (1,)), ((), ())), preferred_element_type=_F32)


def _params(*semantics):
    return pltpu.CompilerParams(dimension_semantics=semantics, vmem_limit_bytes=VMEM_LIMIT_BYTES)


def _const_spec(shape):
    nd = len(shape)
    return pl.BlockSpec(shape, lambda *_: (0,) * nd, pipeline_mode=pl.Buffered(1))


def _in_proj_kernel(x_ref, g_ref, b_ref, w_ref, q_ref, k_ref, v_ref, xr_ref, yg_ref):
    h = _layer_norm(x_ref[...], g_ref[...], b_ref[...]).astype(_BF16)
    c0, c1, c2, c3 = Q_COLS, Q_COLS + KV_COLS, Q_COLS + 2 * KV_COLS, Q_COLS + 2 * KV_COLS + LRU_WIDTH
    q_ref[...] = _dot(h, w_ref[:, 0:c0]).astype(_BF16)
    k_ref[...] = _dot(h, w_ref[:, c0:c1]).astype(_BF16)
    v_ref[...] = _dot(h, w_ref[:, c1:c2]).astype(_BF16)
    xr_ref[...] = _dot(h, w_ref[:, c2:c3])
    yg_ref[...] = _dot(h, w_ref[:, c3:IN_COLS])


def _in_proj(x, g, b, w_bf16):
    t = x.shape[0]
    tm = min(TOKEN_TILE, t)
    assert t % tm == 0
    row = lambda n: pl.BlockSpec((tm, n), lambda i: (i, 0))
    return pl.pallas_call(
        _in_proj_kernel,
        grid=(t // tm,),
        in_specs=[row(D_MODEL), _const_spec((1, D_MODEL)), _const_spec((1, D_MODEL)),
                  _const_spec((D_MODEL, IN_COLS))],
        out_specs=[row(Q_COLS), row(KV_COLS), row(KV_COLS), row(LRU_WIDTH), row(LRU_WIDTH)],
        out_shape=[jax.ShapeDtypeStruct((t, Q_COLS), _BF16),
                   jax.ShapeDtypeStruct((t, KV_COLS), _BF16),
                   jax.ShapeDtypeStruct((t, KV_COLS), _BF16),
                   jax.ShapeDtypeStruct((t, LRU_WIDTH), _F32),
                   jax.ShapeDtypeStruct((t, LRU_WIDTH), _F32)],
        compiler_params=_params("arbitrary"),
        name="in_proj",
    )(x, g, b, w_bf16)


def _attn_kernel(q_ref, kp_ref, kc_ref, kn_ref, vp_ref, vc_ref, vn_ref, kt_ref, vt_ref,
                 bb_ref, tb_ref, o_ref, *, nblk):
    i = pl.program_id(1)
    kj = lax.broadcasted_iota(jnp.int32, (1, 3 * BLOCK), 1)
    valid = ((kj >= BLOCK) | (i > 0)) & ((kj < 2 * BLOCK) | (i < nblk - 1))
    tail_valid = lax.broadcasted_iota(jnp.int32, (1, BLOCK), 1) <= N_META
    scale = HEAD_DIM ** -0.5
    gw = GQA_GROUP * HEAD_DIM
    for h in range(N_KV_HEADS):
        cols = slice(h * HEAD_DIM, (h + 1) * HEAD_DIM)
        kb = jnp.concatenate([kp_ref[0, :, cols], kc_ref[0, :, cols], kn_ref[0, :, cols]], axis=0)
        vb = jnp.concatenate([vp_ref[0, :, cols], vc_ref[0, :, cols], vn_ref[0, :, cols]], axis=0)
        kt = kt_ref[h]
        vt = vt_ref[h]
        qs = jnp.concatenate([q_ref[0, :, h * gw + g * HEAD_DIM:h * gw + (g + 1) * HEAD_DIM]
                              for g in range(GQA_GROUP)], axis=0)
        band_bias = bb_ref[h * GQA_GROUP:(h + 1) * GQA_GROUP].reshape(GQA_GROUP * BLOCK, 3 * BLOCK)
        tail_bias = tb_ref[0, h * GQA_GROUP:(h + 1) * GQA_GROUP].reshape(GQA_GROUP * BLOCK, BLOCK)
        s_b = jnp.where(valid, _dot_nt(qs, kb) * scale + band_bias, NEG_INF)
        s_t = jnp.where(tail_valid, _dot_nt(qs, kt) * scale + tail_bias, NEG_INF)
        m = jnp.maximum(jnp.max(s_b, axis=-1, keepdims=True), jnp.max(s_t, axis=-1, keepdims=True))
        e_b = jnp.exp(s_b - m)
        e_t = jnp.exp(s_t - m)
        den = jnp.sum(e_b, axis=-1, keepdims=True) + jnp.sum(e_t, axis=-1, keepdims=True)
        inv = 1.0 / den
        p_b = (e_b * inv).astype(_BF16)
        p_t = (e_t * inv).astype(_BF16)
        o = (_dot(p_b, vb) + _dot(p_t, vt)).astype(_BF16)
        for g in range(GQA_GROUP):
            o_ref[0, :, h * gw + g * HEAD_DIM:h * gw + (g + 1) * HEAD_DIM] = o[g * BLOCK:(g + 1) * BLOCK]


def _attention(q, k, v, k_tail, v_tail, band_bias, tail_bias):
    bsz, s = q.shape[0], q.shape[1]
    nblk = s // BLOCK
    kv_spec = lambda d: pl.BlockSpec(
        (1, BLOCK, KV_COLS), lambda b, i: (b, jnp.clip(i + d, 0, nblk - 1), 0))
    tail_spec = pl.BlockSpec((N_KV_HEADS, BLOCK, HEAD_DIM), lambda b, i: (0, 0, 0))
    return pl.pallas_call(
        functools.partial(_attn_kernel, nblk=nblk),
        grid=(bsz, nblk),
        in_specs=[pl.BlockSpec((1, BLOCK, Q_COLS), lambda b, i: (b, i, 0)),
                  kv_spec(-1), kv_spec(0), kv_spec(1), kv_spec(-1), kv_spec(0), kv_spec(1),
                  tail_spec, tail_spec,
                  pl.BlockSpec((N_HEADS, BLOCK, 3 * BLOCK), lambda b, i: (0, 0, 0)),
                  pl.BlockSpec((1, N_HEADS, BLOCK, BLOCK), lambda b, i: (i, 0, 0, 0))],
        out_specs=pl.BlockSpec((1, BLOCK, Q_COLS), lambda b, i: (b, i, 0)),
        out_shape=jax.ShapeDtypeStruct((bsz, s, Q_COLS), _BF16),
        compiler_params=_params("arbitrary", "arbitrary"),
        name="attention",
    )(q, k, k, k, v, v, v, k_tail, v_tail, band_bias, tail_bias)


_X0 = 24


def _tile_scan(a, u, h_in, reverse):
    n = a.shape[0]
    nt = n // SUBLANES
    a3 = a.reshape(nt, SUBLANES, LANES)
    u3 = u.reshape(nt, SUBLANES, LANES)
    row = lax.broadcasted_iota(jnp.int32, (nt, SUBLANES, LANES), 1)
    for step in (1, 2, 4):
        shift = (SUBLANES - step) if reverse else step
        a_sh = pltpu.roll(a3, shift, 1)
        u_sh = pltpu.roll(u3, shift, 1)
        ok = (row < SUBLANES - step) if reverse else (row >= step)
        u3 = jnp.where(ok, a3 * u_sh + u3, u3)
        a3 = jnp.where(ok, a3 * a_sh, a3)
    out = [None] * nt
    h = h_in
    order = range(nt - 1, -1, -1) if reverse else range(nt)
    edge = 0 if reverse else SUBLANES - 1
    for j in order:
        ht = u3[j] + a3[j] * h
        out[j] = ht
        h = ht[edge:edge + 1]
    return jnp.concatenate(out, axis=0), h


def _chunk_scan(a_ref, u_ref, o_ref, a0, o0, h_in, reverse):
    nt = LRU_CHUNK // SUBLANES
    order = range(SUBLANES - 1, -1, -1) if reverse else range(SUBLANES)
    rows = lambda ref, base, s: ref[pl.ds(base + s, nt, stride=SUBLANES), :]
    hs, ps = {}, {}
    prev = None
    for s in order:
        a_s = rows(a_ref, a0, s)
        u_s = rows(u_ref, a0, s)
        hs[s] = u_s if prev is None else a_s * hs[prev] + u_s
        ps[s] = a_s if prev is None else a_s * ps[prev]
        prev = s
    ends, h_out = _tile_scan(ps[prev], hs[prev], h_in, reverse)
    tile = lax.broadcasted_iota(jnp.int32, (nt, LANES), 0)
    if reverse:
        carry = jnp.where(tile == nt - 1, h_in, pltpu.roll(ends, nt - 1, 0))
    else:
        carry = jnp.where(tile == 0, h_in, pltpu.roll(ends, 1, 0))
    for s in order:
        o_ref[pl.ds(o0 + s, nt, stride=SUBLANES), :] = hs[s] + ps[s] * carry
    return h_out


def _lru_kernel(xr_ref, yg_ref, xm_ref, cw_ref, cb_ref, w_ref, b_ref, c_ref, o_ref,
                xext, hf, ab, ub, sa, su, *, seq):
    nchunk = seq // LRU_CHUNK
    zeros8 = jnp.zeros((SUBLANES, LANES), _F32)
    xext[0:SUBLANES, :] = zeros8
    xext[SUBLANES:_X0, :] = xm_ref[...]
    xext[_X0:_X0 + seq, :] = xr_ref[0]
    xext[_X0 + seq:_X0 + seq + SUBLANES, :] = zeros8
    cw = cw_ref[...]
    cb = cb_ref[...]
    w = w_ref[0]
    bias = b_ref[0]
    c_f = c_ref[0:1, :]
    c_b = c_ref[1:2, :]

    def gates(r0, n):
        win = xext[pl.ds(r0 - SUBLANES, n + 2 * SUBLANES), :]
        span = n + 2 * SUBLANES
        tap = lambda d: pltpu.roll(win, (span - d) % span, 0)[SUBLANES:SUBLANES + n]
        xc = cb + cw[0:1] * tap(-2)
        xc = xc + cw[1:2] * tap(-1)
        xc = xc + cw[2:3] * win[SUBLANES:SUBLANES + n]
        xc = xc + cw[3:4] * tap(1)
        g = _dot(xc.astype(_BF16), w) + bias
        r_f = jax.nn.sigmoid(g[:, 0:LANES])
        r_b = jax.nn.sigmoid(g[:, LANES:2 * LANES])
        i_f = jax.nn.sigmoid(g[:, 2 * LANES:3 * LANES])
        i_b = jax.nn.sigmoid(g[:, 3 * LANES:4 * LANES])
        la_f = c_f * r_f
        la_b = c_b * r_b
        a_f = jnp.exp(la_f)
        a_b = jnp.exp(la_b)
        root = lambda y: jnp.where(y > 0.0, y * lax.rsqrt(y), 0.0)
        u_f = root(1.0 - a_f * a_f) * (i_f * xc)
        u_b = root(1.0 - a_b * a_b) * (i_b * xc)
        return a_f, u_f, a_b, u_b

    a_f, u_f, _, _ = gates(SUBLANES, N_META)
    _, h0 = _tile_scan(a_f, u_f, jnp.zeros((1, LANES), _F32), reverse=False)

    def fwd(c, h):
        t0 = pl.multiple_of(c * LRU_CHUNK, LRU_CHUNK)
        a_f, u_f, a_b, u_b = gates(_X0 + t0, LRU_CHUNK)
        sa[...] = a_f
        su[...] = u_f
        ab[pl.ds(t0, LRU_CHUNK), :] = a_b
        ub[pl.ds(t0, LRU_CHUNK), :] = u_b
        return _chunk_scan(sa, su, hf, 0, t0, h, reverse=False)

    lax.fori_loop(0, nchunk, fwd, h0)

    def bwd(cc, h):
        c = nchunk - 1 - cc
        t0 = pl.multiple_of(c * LRU_CHUNK, LRU_CHUNK)
        h = _chunk_scan(ab, ub, sa, t0, 0, h, reverse=True)
        tot = hf[pl.ds(t0, LRU_CHUNK), :] + sa[...]
        o_ref[0, pl.ds(t0, LRU_CHUNK), :] = (tot * jax.nn.gelu(yg_ref[0, pl.ds(t0, LRU_CHUNK), :])).astype(_BF16)
        return h

    lax.fori_loop(0, nchunk, bwd, jnp.zeros((1, LANES), _F32))


def _rg_lru(xr, yg, xr_meta, conv_w, conv_b, w_cat, b_cat, c_decay):
    bsz, s = xr.shape[0], xr.shape[1]
    assert s % LRU_CHUNK == 0
    slab = pl.BlockSpec((1, s, LANES), lambda b, n: (b, 0, n))
    col = lambda r: pl.BlockSpec((r, LANES), lambda b, n: (0, n))
    return pl.pallas_call(
        functools.partial(_lru_kernel, seq=s),
        grid=(bsz, LRU_BLOCKS),
        in_specs=[slab, slab, col(N_META), col(4), col(1),
                  pl.BlockSpec((1, LANES, 4 * LANES), lambda b, n: (n, 0, 0)),
                  pl.BlockSpec((1, 1, 4 * LANES), lambda b, n: (n, 0, 0)),
                  col(2)],
        out_specs=slab,
        out_shape=jax.ShapeDtypeStruct((bsz, s, LRU_WIDTH), _BF16),
        scratch_shapes=[pltpu.VMEM((s + _X0 + SUBLANES, LANES), _F32),
                        pltpu.VMEM((s, LANES), _F32),
                        pltpu.VMEM((s, LANES), _F32),
                        pltpu.VMEM((s, LANES), _F32),
                        pltpu.VMEM((LRU_CHUNK, LANES), _F32),
                        pltpu.VMEM((LRU_CHUNK, LANES), _F32)],
        compiler_params=_params("arbitrary", "arbitrary"),
        name="rg_lru",
    )(xr, yg, xr_meta, conv_w, conv_b, w_cat, b_cat, c_decay)


def _mix_kernel(xp_ref, xs_ref, ap_ref, as_ref, rp_ref, rs_ref, g0_ref, b0_ref, wo_ref, g1_ref, b1_ref,
                rwt_ref, rb_ref, h1_ref, te_ref, gt_ref, rk_ref, cnt_ref, carry, *, n_prompt_tiles, tm):
    i = pl.program_id(0)

    @pl.when(i == 0)
    def _():
        carry[...] = jnp.zeros_like(carry)

    first = i < n_prompt_tiles
    x = jnp.where(first, xp_ref[...], xs_ref[...])
    attn = jnp.where(first, ap_ref[...], as_ref[...])
    rec = jnp.where(first, rp_ref[...], rs_ref[...])
    h0 = _layer_norm(x, g0_ref[...], b0_ref[...])
    mix = _dot(attn, wo_ref[0:Q_COLS, :]) + _dot(rec, wo_ref[Q_COLS:Q_COLS + LRU_WIDTH, :])
    h1 = _layer_norm(DN_ALPHA * h0 + mix, g1_ref[...], b1_ref[...])
    h1_ref[...] = h1

    logits = _dot_nt(rwt_ref[...], h1.astype(_BF16)) + rb_ref[:, 0:1]
    eidx = lax.broadcasted_iota(jnp.int32, (N_EXPERTS, tm), 0)
    work = logits
    vals, idxs, hots = [], [], []
    for _ in range(TOP_K):
        m = jnp.max(work, axis=0, keepdims=True)
        idx = jnp.min(jnp.where(work == m, eidx, N_EXPERTS), axis=0, keepdims=True)
        hot = eidx == idx
        vals.append(m)
        idxs.append(idx)
        hots.append(hot)
        work = jnp.where(hot, -jnp.inf, work)
    exps = [jnp.exp(v - vals[0]) for v in vals]
    den = exps[0] + exps[1] + exps[2] + exps[3]
    hot_all = (hots[0] | hots[1] | hots[2] | hots[3])
    hot_f = jnp.where(hot_all, 1.0, 0.0).astype(_F32)
    upper = (lax.broadcasted_iota(jnp.int32, (tm, tm), 0) < lax.broadcasted_iota(jnp.int32, (tm, tm), 1))
    before = _dot(hot_f.astype(_BF16), jnp.where(upper, 1.0, 0.0).astype(_BF16))
    base = carry[:, 0:1] + before
    zero_rows = jnp.zeros((SUBLANES - TOP_K, tm), _F32)
    ranks = [jnp.sum(jnp.where(h, base, 0.0), axis=0, keepdims=True) for h in hots]
    te_ref[...] = jnp.concatenate(idxs + [zero_rows.astype(jnp.int32)], axis=0)
    gt_ref[...] = jnp.concatenate([e / den for e in exps] + [zero_rows], axis=0)
    rk_ref[...] = jnp.concatenate(ranks + [zero_rows], axis=0).astype(jnp.int32)
    carry[...] = carry[...] + jnp.sum(hot_f, axis=1, keepdims=True)
    cnt_ref[...] = carry[...]


def _mix(x_p, x_s, attn_p, attn_s, rec_p, rec_s, g0, b0, w_out_bf16, g1, b1, rw_t, rb):
    tp, ts = x_p.shape[0], x_s.shape[0]
    t = tp + ts
    tm = min(TOKEN_TILE, tp, ts)
    assert tp % tm == 0 and ts % tm == 0
    npt = tp // tm
    first = lambda n: pl.BlockSpec((tm, n), lambda i: (jnp.minimum(i, npt - 1), 0))
    second = lambda n: pl.BlockSpec((tm, n), lambda i: (jnp.maximum(i - npt, 0), 0))
    lane_row = pl.BlockSpec((SUBLANES, tm), lambda i: (0, i))
    return pl.pallas_call(
        functools.partial(_mix_kernel, n_prompt_tiles=npt, tm=tm),
        grid=(t // tm,),
        in_specs=[first(D_MODEL), second(D_MODEL), first(Q_COLS), second(Q_COLS),
                  first(LRU_WIDTH), second(LRU_WIDTH),
                  _const_spec((1, D_MODEL)), _const_spec((1, D_MODEL)),
                  _const_spec((D_MODEL, D_MODEL)),
                  _const_spec((1, D_MODEL)), _const_spec((1, D_MODEL)),
                  _const_spec((N_EXPERTS, D_MODEL)), _const_spec((N_EXPERTS, LANES))],
        out_specs=[pl.BlockSpec((tm, D_MODEL), lambda i: (i, 0)), lane_row, lane_row, lane_row,
                   pl.BlockSpec((N_EXPERTS, LANES), lambda i: (0, 0))],
        out_shape=[jax.ShapeDtypeStruct((t, D_MODEL), _F32),
                   jax.ShapeDtypeStruct((SUBLANES, t), jnp.int32),
                   jax.ShapeDtypeStruct((SUBLANES, t), _F32),
                   jax.ShapeDtypeStruct((SUBLANES, t), jnp.int32),
                   jax.ShapeDtypeStruct((N_EXPERTS, LANES), _F32)],
        scratch_shapes=[pltpu.VMEM((N_EXPERTS, LANES), _F32)],
        compiler_params=_params("arbitrary"),
        name="mix_router",
    )(x_p, x_s, attn_p, attn_s, rec_p, rec_s, g0, b0, w_out_bf16, g1, b1, rw_t, rb)


def _dispatch_kernel(ps_ref, pn_ref, pos_ref, h1_ref, xs_hbm, zrow, sem, psem):
    i = pl.program_id(0)
    groups = h1_ref.shape[0]
    dt = groups * SUBLANES

    for k in range(TOP_K):
        def body(g, c):
            for u in range(SUBLANES):
                dst = pos_ref[0, 0, k * dt + g * SUBLANES + u]
                pltpu.make_async_copy(h1_ref.at[g, pl.ds(u, 1)], xs_hbm.at[pl.ds(dst, 1)], sem.at[0]).start()
            return c
        lax.fori_loop(0, groups, body, 0)

    @pl.when(i == 0)
    def _():
        zrow[...] = jnp.zeros_like(zrow)

        def per_expert(e, c):
            start = ps_ref[e]

            def pad_copy(j):
                return pltpu.make_async_copy(zrow.at[pl.ds(0, 1)], xs_hbm.at[pl.ds(start + j, 1)], psem.at[0])

            def go(j, c2):
                pad_copy(j).start()
                return c2

            def done(j, c2):
                pad_copy(j).wait()
                return c2

            lax.fori_loop(0, pn_ref[e], go, 0)
            lax.fori_loop(0, pn_ref[e], done, 0)
            return c

        lax.fori_loop(0, N_EXPERTS, per_expert, 0)

    for k in range(TOP_K):
        pltpu.make_async_copy(h1_ref, h1_ref, sem.at[0]).wait()


def _dispatch(pad_start, pad_n, pos3, h1, n_rows):
    t = h1.shape[0]
    dt = pos3.shape[2] // TOP_K
    grid_spec = pltpu.PrefetchScalarGridSpec(
        num_scalar_prefetch=2,
        grid=(t // dt,),
        in_specs=[pl.BlockSpec((1, 1, TOP_K * dt), lambda i, ps, pn: (i, 0, 0), memory_space=pltpu.SMEM),
                  pl.BlockSpec((dt // SUBLANES, SUBLANES, D_MODEL), lambda i, ps, pn: (i, 0, 0))],
        out_specs=pl.BlockSpec(memory_space=pl.ANY),
        scratch_shapes=[pltpu.VMEM((SUBLANES, D_MODEL), _F32),
                        pltpu.SemaphoreType.DMA((1,)), pltpu.SemaphoreType.DMA((1,))],
    )
    return pl.pallas_call(
        _dispatch_kernel,
        grid_spec=grid_spec,
        out_shape=jax.ShapeDtypeStruct((n_rows, D_MODEL), _F32),
        compiler_params=_params("arbitrary"),
        name="dispatch_rows",
    )(pad_start, pad_n, pos3, h1.reshape(t // SUBLANES, SUBLANES, D_MODEL))


def _moe_kernel(te_ref, tns_ref, src_ref, x_ref, w1g_ref, w1l_ref, b1g_ref, b1l_ref, w2_ref, b2_ref,
                o_ref, xb):
    i = pl.program_id(0)
    f = pl.program_id(1)
    nsub = tns_ref[i]

    @pl.when(nsub > 0)
    def _active():
        full = nsub == MOE_SUBS_PER_TILE
        b1g = b1g_ref[0]
        b1l = b1l_ref[0]

        def compute(r0, n, first):
            rows = pl.ds(r0, n)
            if first:
                x = x_ref[rows, :].astype(_BF16)
                xb[rows, :] = x
            else:
                x = xb[rows, :]
            glu = jnp.minimum(_dot(x, w1g_ref[0].astype(_BF16)) + b1g, SWIGLU_LIMIT)
            lin = jnp.clip(_dot(x, w1l_ref[0].astype(_BF16)) + b1l, -SWIGLU_LIMIT, SWIGLU_LIMIT)
            act = glu * jax.nn.sigmoid(SWIGLU_ALPHA * glu) * (lin + 1.0)
            part = _dot(act.astype(_BF16), w2_ref[0].astype(_BF16))
            if first:
                o_ref[rows, :] = part + b2_ref[0]
            else:
                o_ref[rows, :] += part

        @pl.when(full & (f == 0))
        def _():
            compute(0, MOE_TILE, True)

        @pl.when(full & (f > 0))
        def _():
            compute(0, MOE_TILE, False)

        @pl.when(jnp.logical_not(full) & (f == 0))
        def _():
            def sub(s, c):
                rows = pl.ds(pl.multiple_of(s * MOE_SUB, MOE_SUB), MOE_SUB)
                xb[rows, :] = x_ref[rows, :].astype(_BF16)
                return c
            lax.fori_loop(0, nsub, sub, 0)
            o_ref[...] = jnp.broadcast_to(b2_ref[0], o_ref.shape)

        @pl.when((nsub & 2) != 0)
        def _():
            compute(0, 2 * MOE_SUB, False)

        @pl.when((nsub & 5) == 1)
        def _():
            compute(pl.multiple_of((nsub & 2) * MOE_SUB, MOE_SUB), MOE_SUB, False)


def _moe(tile_e, tile_nsub, tile_src, xs, w1, b1, w2, b2):
    n_tiles = tile_e.shape[0]
    nf = D_FF // MOE_FF_CHUNK
    last = nf - 1

    def ff(i, f, tns):
        return jnp.where(tns[i] > 0, f, last)

    grid_spec = pltpu.PrefetchScalarGridSpec(
        num_scalar_prefetch=3,
        grid=(n_tiles, nf),
        in_specs=[
            pl.BlockSpec((MOE_TILE, D_MODEL), lambda i, f, te, tns, src: (src[i], 0)),
            pl.BlockSpec((1, D_MODEL, MOE_FF_CHUNK), lambda i, f, te, tns, src: (te[i], 0, ff(i, f, tns))),
            pl.BlockSpec((1, D_MODEL, MOE_FF_CHUNK), lambda i, f, te, tns, src: (te[i], 0, nf + ff(i, f, tns))),
            pl.BlockSpec((1, 1, MOE_FF_CHUNK), lambda i, f, te, tns, src: (te[i], 0, ff(i, f, tns))),
            pl.BlockSpec((1, 1, MOE_FF_CHUNK), lambda i, f, te, tns, src: (te[i], 0, nf + ff(i, f, tns))),
            pl.BlockSpec((1, MOE_FF_CHUNK, D_MODEL), lambda i, f, te, tns, src: (te[i], ff(i, f, tns), 0)),
            pl.BlockSpec((1, 1, D_MODEL), lambda i, f, te, tns, src: (te[i], 0, 0)),
        ],
        out_specs=pl.BlockSpec((MOE_TILE, D_MODEL), lambda i, f, te, tns, src: (src[i], 0)),
        scratch_shapes=[pltpu.VMEM((MOE_TILE, D_MODEL), _BF16)],
    )
    return pl.pallas_call(
        _moe_kernel,
        grid_spec=grid_spec,
        out_shape=jax.ShapeDtypeStruct(xs.shape, _F32),
        compiler_params=_params("arbitrary", "arbitrary"),
        name="moe_experts",
    )(tile_e, tile_nsub, tile_src, xs, w1, w1, b1, b1, w2, b2)


def _combine_kernel(pos_ref, posn_ref, h1_ref, gt_ref, g_ref, b_ref, ys_hbm, o_ref, rows, sem, *, n_steps):
    i = pl.program_id(0)
    ct = h1_ref.shape[0]
    cur = lax.rem(i, 2)

    def row_copy(p_ref, buf, k, g, u):
        src_row = p_ref[0, 0, k * ct + g * SUBLANES + u]
        return pltpu.make_async_copy(ys_hbm.at[pl.ds(src_row, 1)], rows.at[buf, k, g, pl.ds(u, 1)], sem.at[buf])

    def fetch(p_ref, buf, slots):
        for k in slots:
            def body(g, c):
                for u in range(SUBLANES):
                    row_copy(p_ref, buf, k, g, u).start()
                return c
            lax.fori_loop(0, ct // SUBLANES, body, 0)

    def wait_rows(buf):
        for k in range(TOP_K):
            pltpu.make_async_copy(rows.at[1 - buf, k], rows.at[buf, k], sem.at[buf]).wait()

    @pl.when(i == 0)
    def _():
        fetch(pos_ref, 0, range(TOP_K))

    fetch(posn_ref, 1 - cur, range(TOP_K - 1))
    wait_rows(cur)

    gates = jnp.transpose(gt_ref[...])
    expert_rows = lambda k: rows[cur, k].reshape(ct, D_MODEL)
    ffn = expert_rows(0) * gates[:, 0:1]
    for k in range(1, TOP_K):
        ffn = ffn + expert_rows(k) * gates[:, k:k + 1]
    for g in range(ct // SUBLANES):
        for u in range(SUBLANES):
            row_copy(posn_ref, 1 - cur, TOP_K - 1, g, u).start()
    o_ref[...] = _layer_norm(DN_ALPHA * h1_ref[...] + ffn, g_ref[...], b_ref[...])

    @pl.when(i == n_steps - 1)
    def _():
        wait_rows(1 - cur)


def _combine(h1, ys, pos3, gates_t, g2, b2, tok0, ntok):
    ct = pos3.shape[2] // TOP_K
    assert tok0 % ct == 0 and ntok % ct == 0
    off = tok0 // ct
    n_steps = ntok // ct
    return pl.pallas_call(
        functools.partial(_combine_kernel, n_steps=n_steps),
        grid=(n_steps,),
        in_specs=[pl.BlockSpec((1, 1, TOP_K * ct), lambda i: (off + i, 0, 0), memory_space=pltpu.SMEM),
                  pl.BlockSpec((1, 1, TOP_K * ct), lambda i: (off + jnp.minimum(i + 1, n_steps - 1), 0, 0),
                               memory_space=pltpu.SMEM),
                  pl.BlockSpec((ct, D_MODEL), lambda i: (off + i, 0)),
                  pl.BlockSpec((SUBLANES, ct), lambda i: (0, off + i)),
                  _const_spec((1, D_MODEL)), _const_spec((1, D_MODEL)),
                  pl.BlockSpec(memory_space=pl.ANY)],
        out_specs=pl.BlockSpec((ct, D_MODEL), lambda i: (i, 0)),
        out_shape=jax.ShapeDtypeStruct((ntok, D_MODEL), _F32),
        scratch_shapes=[pltpu.VMEM((2, TOP_K, ct // SUBLANES, SUBLANES, D_MODEL), _F32),
                        pltpu.SemaphoreType.DMA((2,))],
        compiler_params=_params("arbitrary"),
        name="combine_ln2",
    )(pos3, pos3, h1, gates_t, g2, b2, ys)


def _t5_bucket(rel):
    half = N_BUCKETS // 2
    exact = half // 2
    n = jnp.abs(rel)
    large = exact + (jnp.log(jnp.maximum(n, 1).astype(_F32) / exact)
                     / math.log(MAX_DISTANCE / exact) * (half - exact)).astype(jnp.int32)
    large = jnp.minimum(large, half - 1)
    return jnp.where(rel > 0, half, 0) + jnp.where(n < exact, n, large)


def _lookup(table, idx):
    out = jnp.zeros(idx.shape + table.shape[1:], table.dtype)
    expand = (...,) + (None,) * (table.ndim - 1)
    for j in range(table.shape[0]):
        out = out + jnp.where((idx == j)[expand], table[j], jnp.zeros((), table.dtype))
    return out


def _bias_tables(rel_bias, sink, nblk):
    rb = rel_bias.astype(_F32)
    qi = jnp.arange(BLOCK)
    kj = jnp.arange(3 * BLOCK)
    rel_band = kj[None, :] - BLOCK - qi[:, None]
    band = _lookup(rb, _t5_bucket(rel_band)).transpose(2, 0, 1)
    band = jnp.where((jnp.abs(rel_band) <= WINDOW)[None], band, NEG_INF)
    dist = jnp.arange(1, nblk * BLOCK + N_META + 1)
    by_dist = _lookup(rb, _t5_bucket(-dist))
    s = nblk * BLOCK
    meta = jnp.stack([by_dist[N_META - m - 1:N_META - m - 1 + s] for m in range(N_META)], axis=1)
    meta = meta.reshape(nblk, BLOCK, N_META, N_HEADS).transpose(0, 3, 1, 2)
    sink_col = jnp.broadcast_to(sink.astype(_F32)[None, :, None, None], (nblk, N_HEADS, BLOCK, 1))
    pad = jnp.zeros((nblk, N_HEADS, BLOCK, BLOCK - N_META - 1), _F32)
    return band, jnp.concatenate([meta, sink_col, pad], axis=-1)


def _dispatch_tables(top_e, rank, counts, t, n_tiles):
    c = counts.astype(jnp.int32)
    nb = (c + MOE_SUB - 1) // MOE_SUB
    ns = (nb + MOE_SUBS_PER_TILE - 1) // MOE_SUBS_PER_TILE
    ends = jnp.cumsum(ns)
    first_tile = ends - ns

    pos = _lookup(first_tile, top_e[:TOP_K]) * MOE_TILE + rank[:TOP_K]
    blocks = lambda dt: pos.reshape(TOP_K, t // dt, dt).transpose(1, 0, 2).reshape(t // dt, 1, TOP_K * dt)
    pos3 = (blocks(DISPATCH_TOKENS), blocks(COMBINE_TOKENS))
    pad_start = first_tile * MOE_TILE + c
    pad_n = nb * MOE_SUB - c

    tid = jnp.arange(n_tiles, dtype=jnp.int32)
    n_used = ends[-1]
    used = tid < n_used
    te = jnp.minimum(jnp.sum((tid[:, None] >= ends[None, :]).astype(jnp.int32), axis=1), N_EXPERTS - 1)
    last_used = jnp.maximum(n_used - 1, 0)
    te = jnp.where(used, te, te[last_used])
    jt = tid - first_tile[te]
    nsub = jnp.where(used, jnp.clip(nb[te] - jt * MOE_SUBS_PER_TILE, 0, MOE_SUBS_PER_TILE), 0)
    src = jnp.where(used, tid, last_used)
    return te, nsub.astype(jnp.int32), src.astype(jnp.int32), pos3, pad_start, pad_n


def kernel(x_prompt, x_sample, meta_tokens, ln_in_g, ln_in_b, rel_bias, w_in, conv_w, conv_b, lru_wa, lru_ba,
           lru_wi, lru_bi, lru_lam, attn_sink, w_out, ln1_g, ln1_b, router_w, router_b, exp_w1, exp_b1,
           exp_w2, exp_b2, ln2_g, ln2_b):
    assert DEPTH == 1 and w_in.shape[0] == 1
    row = lambda a: a.reshape(1, -1).astype(_F32)
    g0, b0 = row(ln_in_g), row(ln_in_b)
    w_in_b = w_in[0].astype(_BF16)
    w_out_b = w_out[0].astype(_BF16)

    wa, wi = lru_wa[0], lru_wi[0]
    w_cat = jnp.concatenate([wa[0], wa[1], wi[0], wi[1]], axis=-1).astype(_BF16)
    blk = lambda v: v.reshape(LRU_BLOCKS, 1, LRU_BLOCK_W)
    b_cat = jnp.concatenate([blk(lru_ba[0, 0]), blk(lru_ba[0, 1]), blk(lru_bi[0, 0]), blk(lru_bi[0, 1])],
                            axis=-1).astype(_F32)
    c_decay = -LRU_C * jax.nn.softplus(-lru_lam[0].astype(_F32))

    groups = [x_prompt, x_sample]
    nblk_max = max(x.shape[1] for x in groups) // BLOCK
    band_bias, tail_bias = _bias_tables(rel_bias, attn_sink[0], nblk_max)

    _, k_m, v_m, xr_m, _ = _in_proj(meta_tokens.astype(_F32), g0, b0, w_in_b)
    tail = lambda a: jnp.zeros((N_KV_HEADS, BLOCK, HEAD_DIM), _BF16).at[:, :N_META].set(
        a.reshape(N_META, N_KV_HEADS, HEAD_DIM).transpose(1, 0, 2))
    k_tail, v_tail = tail(k_m), tail(v_m)

    xs, attns, recs = [], [], []
    for x in groups:
        bsz, s = x.shape[0], x.shape[1]
        x2 = x.reshape(bsz * s, D_MODEL)
        q, k, v, xr, yg = _in_proj(x2, g0, b0, w_in_b)
        attn = _attention(q.reshape(bsz, s, Q_COLS), k.reshape(bsz, s, KV_COLS), v.reshape(bsz, s, KV_COLS),
                          k_tail, v_tail, band_bias, tail_bias)
        rec = _rg_lru(xr.reshape(bsz, s, LRU_WIDTH), yg.reshape(bsz, s, LRU_WIDTH), xr_m,
                      conv_w[0].astype(_F32), row(conv_b[0]), w_cat, b_cat, c_decay)
        xs.append(x2)
        attns.append(attn.reshape(bsz * s, Q_COLS))
        recs.append(rec.reshape(bsz * s, LRU_WIDTH))

    rw_t = router_w[0].T.astype(_BF16)
    rb = jnp.broadcast_to(router_b[0].astype(_F32)[:, None], (N_EXPERTS, LANES))
    h1, top_e, gates_t, rank, counts = _mix(xs[0], xs[1], attns[0], attns[1], recs[0], recs[1], g0, b0,
                                            w_out_b, row(ln1_g[0]), row(ln1_b[0]), rw_t, rb)

    t = h1.shape[0]
    n_tiles = TOP_K * t // MOE_TILE + N_EXPERTS
    te, nsub, src_tile, pos3, pad_start, pad_n = _dispatch_tables(top_e, rank, counts[:, 0], t, n_tiles)
    xs = _dispatch(pad_start, pad_n, pos3[0], h1, n_tiles * MOE_TILE)
    ys = _moe(te, nsub, src_tile, xs,
              exp_w1[0], exp_b1[0].reshape(N_EXPERTS, 1, 2 * D_FF), exp_w2[0],
              exp_b2[0].reshape(N_EXPERTS, 1, D_MODEL))

    outs = []
    tok0 = 0
    for x in groups:
        bsz, s = x.shape[0], x.shape[1]
        y = _combine(h1, ys, pos3[1], gates_t, row(ln2_g[0]), row(ln2_b[0]), tok0, bsz * s)
        outs.append(y.reshape(bsz, s, D_MODEL))
        tok0 += bsz * s
    return tuple(outs)
```

```python
import functools
import math

import jax
import jax.numpy as jnp
from jax import lax
from jax.experimental import pallas as pl
from jax.experimental.pallas import tpu as pltpu

D_MODEL = 2048
HEAD_DIM = 128
N_HEADS = 8
N_KV_HEADS = 2
GQA_GROUP = N_HEADS // N_KV_HEADS
Q_COLS = N_HEADS * HEAD_DIM
KV_COLS = N_KV_HEADS * HEAD_DIM
LRU_WIDTH = 1024
LRU_BLOCKS = 8
LRU_BLOCK_W = LRU_WIDTH // LRU_BLOCKS
LRU_C = 8.0
IN_COLS = Q_COLS + 2 * KV_COLS + 2 * LRU_WIDTH
WINDOW = 128
BLOCK = 128
N_BUCKETS = 32
MAX_DISTANCE = 128
N_META = 16
N_EXPERTS = 32
TOP_K = 4
D_FF = D_MODEL
SWIGLU_LIMIT = 7.0
SWIGLU_ALPHA = 1.702
DEPTH = 1
DN_ALPHA = (2.0 * DEPTH) ** 0.25
LN_EPS = 1e-5
NEG_INF = -1e30

SUBLANES = 8
LANES = 128
VMEM_LIMIT_BYTES = 56 * 1024 * 1024

TOKEN_TILE = 512
LRU_CHUNK = 512
MOE_SUB = 256
MOE_SUBS_PER_TILE = 4
MOE_TILE = MOE_SUB * MOE_SUBS_PER_TILE
MOE_FF_CHUNK = 256
DISPATCH_TOKENS = 1024
COMBINE_TOKENS = 256

_BF16 = jnp.bfloat16
_F32 = jnp.float32


def _layer_norm(x, g, b):
    mu = jnp.mean(x, axis=-1, keepdims=True)
    xc = x - mu
    var = jnp.mean(xc * xc, axis=-1, keepdims=True)
    return xc * lax.rsqrt(var + LN_EPS) * g + b


def _dot(a, b):
    return jnp.dot(a, b, preferred_element_type=_F32)


def _dot_nt(a, b):
    return lax.dot_general(a, b, (((1,), (1,)), ((), ())), preferred_element_type=_F32)


def _params(*semantics):
    return pltpu.CompilerParams(dimension_semantics=semantics, vmem_limit_bytes=VMEM_LIMIT_BYTES)


def _const_spec(shape):
    nd = len(shape)
    return pl.BlockSpec(shape, lambda *_: (0,) * nd, pipeline_mode=pl.Buffered(1))


def _in_proj_kernel(x_ref, g_ref, b_ref, w_ref, q_ref, k_ref, v_ref, xr_ref, yg_ref):
    h = _layer_norm(x_ref[...], g_ref[...], b_ref[...]).astype(_BF16)
    c0, c1, c2, c3 = Q_COLS, Q_COLS + KV_COLS, Q_COLS + 2 * KV_COLS, Q_COLS + 2 * KV_COLS + LRU_WIDTH
    q_ref[...] = _dot(h, w_ref[:, 0:c0]).astype(_BF16)
    k_ref[...] = _dot(h, w_ref[:, c0:c1]).astype(_BF16)
    v_ref[...] = _dot(h, w_ref[:, c1:c2]).astype(_BF16)
    xr_ref[...] = _dot(h, w_ref[:, c2:c3])
    yg_ref[...] = _dot(h, w_ref[:, c3:IN_COLS])


def _in_proj(x, g, b, w_bf16):
    t = x.shape[0]
    tm = min(TOKEN_TILE, t)
    assert t % tm == 0
    row = lambda n: pl.BlockSpec((tm, n), lambda i: (i, 0))
    return pl.pallas_call(
        _in_proj_kernel,
        grid=(t // tm,),
        in_specs=[row(D_MODEL), _const_spec((1, D_MODEL)), _const_spec((1, D_MODEL)),
                  _const_spec((D_MODEL, IN_COLS))],
        out_specs=[row(Q_COLS), row(KV_COLS), row(KV_COLS), row(LRU_WIDTH), row(LRU_WIDTH)],
        out_shape=[jax.ShapeDtypeStruct((t, Q_COLS), _BF16),
                   jax.ShapeDtypeStruct((t, KV_COLS), _BF16),
                   jax.ShapeDtypeStruct((t, KV_COLS), _BF16),
                   jax.ShapeDtypeStruct((t, LRU_WIDTH), _F32),
                   jax.ShapeDtypeStruct((t, LRU_WIDTH), _F32)],
        compiler_params=_params("arbitrary"),
        name="in_proj",
    )(x, g, b, w_bf16)


def _attn_kernel(q_ref, kp_ref, kc_ref, kn_ref, vp_ref, vc_ref, vn_ref, kt_ref, vt_ref,
                 bb_ref, tb_ref, o_ref, *, nblk):
    i = pl.program_id(1)
    kj = lax.broadcasted_iota(jnp.int32, (1, 3 * BLOCK), 1)
    valid = ((kj >= BLOCK) | (i > 0)) & ((kj < 2 * BLOCK) | (i < nblk - 1))
    tail_valid = lax.broadcasted_iota(jnp.int32, (1, BLOCK), 1) <= N_META
    scale = HEAD_DIM ** -0.5
    gw = GQA_GROUP * HEAD_DIM
    for h in range(N_KV_HEADS):
        cols = slice(h * HEAD_DIM, (h + 1) * HEAD_DIM)
        kb = jnp.concatenate([kp_ref[0, :, cols], kc_ref[0, :, cols], kn_ref[0, :, cols]], axis=0)
        vb = jnp.concatenate([vp_ref[0, :, cols], vc_ref[0, :, cols], vn_ref[0, :, cols]], axis=0)
        kt = kt_ref[h]
        vt = vt_ref[h]
        qs = jnp.concatenate([q_ref[0, :, h * gw + g * HEAD_DIM:h * gw + (g + 1) * HEAD_DIM]
                              for g in range(GQA_GROUP)], axis=0)
        band_bias = bb_ref[h * GQA_GROUP:(h + 1) * GQA_GROUP].reshape(GQA_GROUP * BLOCK, 3 * BLOCK)
        tail_bias = tb_ref[0, h * GQA_GROUP:(h + 1) * GQA_GROUP].reshape(GQA_GROUP * BLOCK, BLOCK)
        s_b = jnp.where(valid, _dot_nt(qs, kb) * scale + band_bias, NEG_INF)
        s_t = jnp.where(tail_valid, _dot_nt(qs, kt) * scale + tail_bias, NEG_INF)
        m = jnp.maximum(jnp.max(s_b, axis=-1, keepdims=True), jnp.max(s_t, axis=-1, keepdims=True))
        e_b = jnp.exp(s_b - m)
        e_t = jnp.exp(s_t - m)
        den = jnp.sum(e_b, axis=-1, keepdims=True) + jnp.sum(e_t, axis=-1, keepdims=True)
        inv = 1.0 / den
        p_b = (e_b * inv).astype(_BF16)
        p_t = (e_t * inv).astype(_BF16)
        o = (_dot(p_b, vb) + _dot(p_t, vt)).astype(_BF16)
        for g in range(GQA_GROUP):
            o_ref[0, :, h * gw + g * HEAD_DIM:h * gw + (g + 1) * HEAD_DIM] = o[g * BLOCK:(g + 1) * BLOCK]


def _attention(q, k, v, k_tail, v_tail, band_bias, tail_bias):
    bsz, s = q.shape[0], q.shape[1]
    nblk = s // BLOCK
    kv_spec = lambda d: pl.BlockSpec(
        (1, BLOCK, KV_COLS), lambda b, i: (b, jnp.clip(i + d, 0, nblk - 1), 0))
    tail_spec = pl.BlockSpec((N_KV_HEADS, BLOCK, HEAD_DIM), lambda b, i: (0, 0, 0))
    return pl.pallas_call(
        functools.partial(_attn_kernel, nblk=nblk),
        grid=(bsz, nblk),
        in_specs=[pl.BlockSpec((1, BLOCK, Q_COLS), lambda b, i: (b, i, 0)),
                  kv_spec(-1), kv_spec(0), kv_spec(1), kv_spec(-1), kv_spec(0), kv_spec(1),
                  tail_spec, tail_spec,
                  pl.BlockSpec((N_HEADS, BLOCK, 3 * BLOCK), lambda b, i: (0, 0, 0)),
                  pl.BlockSpec((1, N_HEADS, BLOCK, BLOCK), lambda b, i: (i, 0, 0, 0))],
        out_specs=pl.BlockSpec((1, BLOCK, Q_COLS), lambda b, i: (b, i, 0)),
        out_shape=jax.ShapeDtypeStruct((bsz, s, Q_COLS), _BF16),
        compiler_params=_params("arbitrary", "arbitrary"),
        name="attention",
    )(q, k, k, k, v, v, v, k_tail, v_tail, band_bias, tail_bias)


_X0 = 24


def _tile_scan(a, u, h_in, reverse):
    n = a.shape[0]
    nt = n // SUBLANES
    a3 = a.reshape(nt, SUBLANES, LANES)
    u3 = u.reshape(nt, SUBLANES, LANES)
    row = lax.broadcasted_iota(jnp.int32, (nt, SUBLANES, LANES), 1)
    for step in (1, 2, 4):
        shift = (SUBLANES - step) if reverse else step
        a_sh = pltpu.roll(a3, shift, 1)
        u_sh = pltpu.roll(u3, shift, 1)
        ok = (row < SUBLANES - step) if reverse else (row >= step)
        u3 = jnp.where(ok, a3 * u_sh + u3, u3)
        a3 = jnp.where(ok, a3 * a_sh, a3)
    out = [None] * nt
    h = h_in
    order = range(nt - 1, -1, -1) if reverse else range(nt)
    edge = 0 if reverse else SUBLANES - 1
    for j in order:
        ht = u3[j] + a3[j] * h
        out[j] = ht
        h = ht[edge:edge + 1]
    return jnp.concatenate(out, axis=0), h


def _chunk_scan(a_ref, u_ref, o_ref, a0, o0, h_in, reverse):
    nt = LRU_CHUNK // SUBLANES
    order = range(SUBLANES - 1, -1, -1) if reverse else range(SUBLANES)
    rows = lambda ref, base, s: ref[pl.ds(base + s, nt, stride=SUBLANES), :]
    hs, ps = {}, {}
    prev = None
    for s in order:
        a_s = rows(a_ref, a0, s)
        u_s = rows(u_ref, a0, s)
        hs[s] = u_s if prev is None else a_s * hs[prev] + u_s
        ps[s] = a_s if prev is None else a_s * ps[prev]
        prev = s
    ends, h_out = _tile_scan(ps[prev], hs[prev], h_in, reverse)
    tile = lax.broadcasted_iota(jnp.int32, (nt, LANES), 0)
    if reverse:
        carry = jnp.where(tile == nt - 1, h_in, pltpu.roll(ends, nt - 1, 0))
    else:
        carry = jnp.where(tile == 0, h_in, pltpu.roll(ends, 1, 0))
    for s in order:
        o_ref[pl.ds(o0 + s, nt, stride=SUBLANES), :] = hs[s] + ps[s] * carry
    return h_out


def _lru_kernel(xr_ref, yg_ref, xm_ref, cw_ref, cb_ref, w_ref, b_ref, c_ref, o_ref,
                xext, hf, ab, ub, sa, su, *, seq):
    nchunk = seq // LRU_CHUNK
    zeros8 = jnp.zeros((SUBLANES, LANES), _F32)
    xext[0:SUBLANES, :] = zeros8
    xext[SUBLANES:_X0, :] = xm_ref[...]
    xext[_X0:_X0 + seq, :] = xr_ref[0]
    xext[_X0 + seq:_X0 + seq + SUBLANES, :] = zeros8
    cw = cw_ref[...]
    cb = cb_ref[...]
    w = w_ref[0]
    bias = b_ref[0]
    c_f = c_ref[0:1, :]
    c_b = c_ref[1:2, :]

    def gates(r0, n):
        win = xext[pl.ds(r0 - SUBLANES, n + 2 * SUBLANES), :]
        span = n + 2 * SUBLANES
        tap = lambda d: pltpu.roll(win, (span - d) % span, 0)[SUBLANES:SUBLANES + n]
        xc = cb + cw[0:1] * tap(-2)
        xc = xc + cw[1:2] * tap(-1)
        xc = xc + cw[2:3] * win[SUBLANES:SUBLANES + n]
        xc = xc + cw[3:4] * tap(1)
        g = _dot(xc.astype(_BF16), w) + bias
        r_f = jax.nn.sigmoid(g[:, 0:LANES])
        r_b = jax.nn.sigmoid(g[:, LANES:2 * LANES])
        i_f = jax.nn.sigmoid(g[:, 2 * LANES:3 * LANES])
        i_b = jax.nn.sigmoid(g[:, 3 * LANES:4 * LANES])
        la_f = c_f * r_f
        la_b = c_b * r_b
        a_f = jnp.exp(la_f)
        a_b = jnp.exp(la_b)
        root = lambda y: jnp.where(y > 0.0, y * lax.rsqrt(y), 0.0)
        u_f = root(1.0 - a_f * a_f) * (i_f * xc)
        u_b = root(1.0 - a_b * a_b) * (i_b * xc)
        return a_f, u_f, a_b, u_b

    a_f, u_f, _, _ = gates(SUBLANES, N_META)
    _, h0 = _tile_scan(a_f, u_f, jnp.zeros((1, LANES), _F32), reverse=False)

    def fwd(c, h):
        t0 = pl.multiple_of(c * LRU_CHUNK, LRU_CHUNK)
        a_f, u_f, a_b, u_b = gates(_X0 + t0, LRU_CHUNK)
        sa[...] = a_f
        su[...] = u_f
        ab[pl.ds(t0, LRU_CHUNK), :] = a_b
        ub[pl.ds(t0, LRU_CHUNK), :] = u_b
        return _chunk_scan(sa, su, hf, 0, t0, h, reverse=False)

    lax.fori_loop(0, nchunk, fwd, h0)

    def bwd(cc, h):
        c = nchunk - 1 - cc
        t0 = pl.multiple_of(c * LRU_CHUNK, LRU_CHUNK)
        h = _chunk_scan(ab, ub, sa, t0, 0, h, reverse=True)
        tot = hf[pl.ds(t0, LRU_CHUNK), :] + sa[...]
        o_ref[0, pl.ds(t0, LRU_CHUNK), :] = (tot * jax.nn.gelu(yg_ref[0, pl.ds(t0, LRU_CHUNK), :])).astype(_BF16)
        return h

    lax.fori_loop(0, nchunk, bwd, jnp.zeros((1, LANES), _F32))


def _rg_lru(xr, yg, xr_meta, conv_w, conv_b, w_cat, b_cat, c_decay):
    bsz, s = xr.shape[0], xr.shape[1]
    assert s % LRU_CHUNK == 0
    slab = pl.BlockSpec((1, s, LANES), lambda b, n: (b, 0, n))
    col = lambda r: pl.BlockSpec((r, LANES), lambda b, n: (0, n))
    return pl.pallas_call(
        functools.partial(_lru_kernel, seq=s),
        grid=(bsz, LRU_BLOCKS),
        in_specs=[slab, slab, col(N_META), col(4), col(1),
                  pl.BlockSpec((1, LANES, 4 * LANES), lambda b, n: (n, 0, 0)),
                  pl.BlockSpec((1, 1, 4 * LANES), lambda b, n: (n, 0, 0)),
                  col(2)],
        out_specs=slab,
        out_shape=jax.ShapeDtypeStruct((bsz, s, LRU_WIDTH), _BF16),
        scratch_shapes=[pltpu.VMEM((s + _X0 + SUBLANES, LANES), _F32),
                        pltpu.VMEM((s, LANES), _F32),
                        pltpu.VMEM((s, LANES), _F32),
                        pltpu.VMEM((s, LANES), _F32),
                        pltpu.VMEM((LRU_CHUNK, LANES), _F32),
                        pltpu.VMEM((LRU_CHUNK, LANES), _F32)],
        compiler_params=_params("arbitrary", "arbitrary"),
        name="rg_lru",
    )(xr, yg, xr_meta, conv_w, conv_b, w_cat, b_cat, c_decay)


def _mix_kernel(xp_ref, xs_ref, ap_ref, as_ref, rp_ref, rs_ref, g0_ref, b0_ref, wo_ref, g1_ref, b1_ref,
                rwt_ref, rb_ref, h1_ref, te_ref, gt_ref, rk_ref, cnt_ref, carry, *, n_prompt_tiles, tm):
    i = pl.program_id(0)

    @pl.when(i == 0)
    def _():
        carry[...] = jnp.zeros_like(carry)

    first = i < n_prompt_tiles
    x = jnp.where(first, xp_ref[...], xs_ref[...])
    attn = jnp.where(first, ap_ref[...], as_ref[...])
    rec = jnp.where(first, rp_ref[...], rs_ref[...])
    h0 = _layer_norm(x, g0_ref[...], b0_ref[...])
    mix = _dot(attn, wo_ref[0:Q_COLS, :]) + _dot(rec, wo_ref[Q_COLS:Q_COLS + LRU_WIDTH, :])
    h1 = _layer_norm(DN_ALPHA * h0 + mix, g1_ref[...], b1_ref[...])
    h1_ref[...] = h1

    logits = _dot_nt(rwt_ref[...], h1.astype(_BF16)) + rb_ref[:, 0:1]
    eidx = lax.broadcasted_iota(jnp.int32, (N_EXPERTS, tm), 0)
    work = logits
    vals, idxs, hots = [], [], []
    for _ in range(TOP_K):
        m = jnp.max(work, axis=0, keepdims=True)
        idx = jnp.min(jnp.where(work == m, eidx, N_EXPERTS), axis=0, keepdims=True)
        hot = eidx == idx
        vals.append(m)
        idxs.append(idx)
        hots.append(hot)
        work = jnp.where(hot, -jnp.inf, work)
    exps = [jnp.exp(v - vals[0]) for v in vals]
    den = exps[0] + exps[1] + exps[2] + exps[3]
    hot_all = (hots[0] | hots[1] | hots[2] | hots[3])
    hot_f = jnp.where(hot_all, 1.0, 0.0).astype(_F32)
    upper = (lax.broadcasted_iota(jnp.int32, (tm, tm), 0) < lax.broadcasted_iota(jnp.int32, (tm, tm), 1))
    before = _dot(hot_f.astype(_BF16), jnp.where(upper, 1.0, 0.0).astype(_BF16))
    base = carry[:, 0:1] + before
    zero_rows = jnp.zeros((SUBLANES - TOP_K, tm), _F32)
    ranks = [jnp.sum(jnp.where(h, base, 0.0), axis=0, keepdims=True) for h in hots]
    te_ref[...] = jnp.concatenate(idxs + [zero_rows.astype(jnp.int32)], axis=0)
    gt_ref[...] = jnp.concatenate([e / den for e in exps] + [zero_rows], axis=0)
    rk_ref[...] = jnp.concatenate(ranks + [zero_rows], axis=0).astype(jnp.int32)
    carry[...] = carry[...] + jnp.sum(hot_f, axis=1, keepdims=True)
    cnt_ref[...] = carry[...]


def _mix(x_p, x_s, attn_p, attn_s, rec_p, rec_s, g0, b0, w_out_bf16, g1, b1, rw_t, rb):
    tp, ts = x_p.shape[0], x_s.shape[0]
    t = tp + ts
    tm = min(TOKEN_TILE, tp, ts)
    assert tp % tm == 0 and ts % tm == 0
    npt = tp // tm
    first = lambda n: pl.BlockSpec((tm, n), lambda i: (jnp.minimum(i, npt - 1), 0))
    second = lambda n: pl.BlockSpec((tm, n), lambda i: (jnp.maximum(i - npt, 0), 0))
    lane_row = pl.BlockSpec((SUBLANES, tm), lambda i: (0, i))
    return pl.pallas_call(
        functools.partial(_mix_kernel, n_prompt_tiles=npt, tm=tm),
        grid=(t // tm,),
        in_specs=[first(D_MODEL), second(D_MODEL), first(Q_COLS), second(Q_COLS),
                  first(LRU_WIDTH), second(LRU_WIDTH),
                  _const_spec((1, D_MODEL)), _const_spec((1, D_MODEL)),
                  _const_spec((D_MODEL, D_MODEL)),
                  _const_spec((1, D_MODEL)), _const_spec((1, D_MODEL)),
                  _const_spec((N_EXPERTS, D_MODEL)), _const_spec((N_EXPERTS, LANES))],
        out_specs=[pl.BlockSpec((tm, D_MODEL), lambda i: (i, 0)), lane_row, lane_row, lane_row,
                   pl.BlockSpec((N_EXPERTS, LANES), lambda i: (0, 0))],
        out_shape=[jax.ShapeDtypeStruct((t, D_MODEL), _F32),
                   jax.ShapeDtypeStruct((SUBLANES, t), jnp.int32),
                   jax.ShapeDtypeStruct((SUBLANES, t), _F32),
                   jax.ShapeDtypeStruct((SUBLANES, t), jnp.int32),
                   jax.ShapeDtypeStruct((N_EXPERTS, LANES), _F32)],
        scratch_shapes=[pltpu.VMEM((N_EXPERTS, LANES), _F32)],
        compiler_params=_params("arbitrary"),
        name="mix_router",
    )(x_p, x_s, attn_p, attn_s, rec_p, rec_s, g0, b0, w_out_bf16, g1, b1, rw_t, rb)


def _dispatch_kernel(ps_ref, pn_ref, pos_ref, h1_ref, xs_hbm, zrow, sem, psem):
    i = pl.program_id(0)
    groups = h1_ref.shape[0]
    dt = groups * SUBLANES

    for k in range(TOP_K):
        def body(g, c):
            for u in range(SUBLANES):
                dst = pos_ref[0, 0, k * dt + g * SUBLANES + u]
                pltpu.make_async_copy(h1_ref.at[g, pl.ds(u, 1)], xs_hbm.at[pl.ds(dst, 1)],
                                      sem.at[0]).start(priority=u % 2)
            return c
        lax.fori_loop(0, groups, body, 0)

    @pl.when(i == 0)
    def _():
        zrow[...] = jnp.zeros_like(zrow)

        def per_expert(e, c):
            start = ps_ref[e]

            def pad_copy(j):
                return pltpu.make_async_copy(zrow.at[pl.ds(0, 1)], xs_hbm.at[pl.ds(start + j, 1)], psem.at[0])

            def go(j, c2):
                pad_copy(j).start()
                return c2

            def done(j, c2):
                pad_copy(j).wait()
                return c2

            lax.fori_loop(0, pn_ref[e], go, 0)
            lax.fori_loop(0, pn_ref[e], done, 0)
            return c

        lax.fori_loop(0, N_EXPERTS, per_expert, 0)

    for k in range(TOP_K):
        pltpu.make_async_copy(h1_ref, h1_ref, sem.at[0]).wait()


def _dispatch(pad_start, pad_n, pos3, h1, n_rows):
    t = h1.shape[0]
    dt = pos3.shape[2] // TOP_K
    grid_spec = pltpu.PrefetchScalarGridSpec(
        num_scalar_prefetch=2,
        grid=(t // dt,),
        in_specs=[pl.BlockSpec((1, 1, TOP_K * dt), lambda i, ps, pn: (i, 0, 0), memory_space=pltpu.SMEM),
                  pl.BlockSpec((dt // SUBLANES, SUBLANES, D_MODEL), lambda i, ps, pn: (i, 0, 0))],
        out_specs=pl.BlockSpec(memory_space=pl.ANY),
        scratch_shapes=[pltpu.VMEM((SUBLANES, D_MODEL), _F32),
                        pltpu.SemaphoreType.DMA((1,)), pltpu.SemaphoreType.DMA((1,))],
    )
    return pl.pallas_call(
        _dispatch_kernel,
        grid_spec=grid_spec,
        out_shape=jax.ShapeDtypeStruct((n_rows, D_MODEL), _F32),
        compiler_params=_params("arbitrary"),
        name="dispatch_rows",
    )(pad_start, pad_n, pos3, h1.reshape(t // SUBLANES, SUBLANES, D_MODEL))


def _moe_kernel(te_ref, tns_ref, src_ref, x_ref, w1g_ref, w1l_ref, b1g_ref, b1l_ref, w2_ref, b2_ref,
                o_ref, xb):
    i = pl.program_id(0)
    f = pl.program_id(1)
    nsub = tns_ref[i]

    @pl.when(nsub > 0)
    def _active():
        full = nsub == MOE_SUBS_PER_TILE
        b1g = b1g_ref[0]
        b1l = b1l_ref[0]

        def compute(r0, n, first):
            rows = pl.ds(r0, n)
            if first:
                x = x_ref[rows, :].astype(_BF16)
                xb[rows, :] = x
            else:
                x = xb[rows, :]
            glu = jnp.minimum(_dot(x, w1g_ref[0].astype(_BF16)) + b1g, SWIGLU_LIMIT)
            lin = jnp.clip(_dot(x, w1l_ref[0].astype(_BF16)) + b1l, -SWIGLU_LIMIT, SWIGLU_LIMIT)
            act = glu * jax.nn.sigmoid(SWIGLU_ALPHA * glu) * (lin + 1.0)
            part = _dot(act.astype(_BF16), w2_ref[0].astype(_BF16))
            if first:
                o_ref[rows, :] = part + b2_ref[0]
            else:
                o_ref[rows, :] += part

        @pl.when(full & (f == 0))
        def _():
            compute(0, MOE_TILE, True)

        @pl.when(full & (f > 0))
        def _():
            compute(0, MOE_TILE, False)

        @pl.when(jnp.logical_not(full) & (f == 0))
        def _():
            def sub(s, c):
                rows = pl.ds(pl.multiple_of(s * MOE_SUB, MOE_SUB), MOE_SUB)
                xb[rows, :] = x_ref[rows, :].astype(_BF16)
                return c
            lax.fori_loop(0, nsub, sub, 0)
            o_ref[...] = jnp.broadcast_to(b2_ref[0], o_ref.shape)

        @pl.when((nsub & 2) != 0)
        def _():
            compute(0, 2 * MOE_SUB, False)

        @pl.when((nsub & 5) == 1)
        def _():
            compute(pl.multiple_of((nsub & 2) * MOE_SUB, MOE_SUB), MOE_SUB, False)


def _moe(tile_e, tile_nsub, tile_src, xs, w1, b1, w2, b2):
    n_tiles = tile_e.shape[0]
    nf = D_FF // MOE_FF_CHUNK
    last = nf - 1

    def ff(i, f, tns):
        return jnp.where(tns[i] > 0, f, last)

    grid_spec = pltpu.PrefetchScalarGridSpec(
        num_scalar_prefetch=3,
        grid=(n_tiles, nf),
        in_specs=[
            pl.BlockSpec((MOE_TILE, D_MODEL), lambda i, f, te, tns, src: (src[i], 0)),
            pl.BlockSpec((1, D_MODEL, MOE_FF_CHUNK), lambda i, f, te, tns, src: (te[i], 0, ff(i, f, tns))),
            pl.BlockSpec((1, D_MODEL, MOE_FF_CHUNK), lambda i, f, te, tns, src: (te[i], 0, nf + ff(i, f, tns))),
            pl.BlockSpec((1, 1, MOE_FF_CHUNK), lambda i, f, te, tns, src: (te[i], 0, ff(i, f, tns))),
            pl.BlockSpec((1, 1, MOE_FF_CHUNK), lambda i, f, te, tns, src: (te[i], 0, nf + ff(i, f, tns))),
            pl.BlockSpec((1, MOE_FF_CHUNK, D_MODEL), lambda i, f, te, tns, src: (te[i], ff(i, f, tns), 0)),
            pl.BlockSpec((1, 1, D_MODEL), lambda i, f, te, tns, src: (te[i], 0, 0)),
        ],
        out_specs=pl.BlockSpec((MOE_TILE, D_MODEL), lambda i, f, te, tns, src: (src[i], 0)),
        scratch_shapes=[pltpu.VMEM((MOE_TILE, D_MODEL), _BF16)],
    )
    return pl.pallas_call(
        _moe_kernel,
        grid_spec=grid_spec,
        out_shape=jax.ShapeDtypeStruct(xs.shape, _F32),
        compiler_params=_params("arbitrary", "arbitrary"),
        name="moe_experts",
    )(tile_e, tile_nsub, tile_src, xs, w1, w1, b1, b1, w2, b2)


def _combine_kernel(pos_ref, posn_ref, h1_ref, gt_ref, g_ref, b_ref, ys_hbm, o_ref, rows, sem, *, n_steps):
    i = pl.program_id(0)
    ct = h1_ref.shape[0]
    cur = lax.rem(i, 2)

    def row_copy(p_ref, buf, k, g, u):
        src_row = p_ref[0, 0, k * ct + g * SUBLANES + u]
        return pltpu.make_async_copy(ys_hbm.at[pl.ds(src_row, 1)], rows.at[buf, k, g, pl.ds(u, 1)], sem.at[buf])

    def fetch(p_ref, buf, slots):
        for k in slots:
            def body(g, c):
                for u in range(SUBLANES):
                    row_copy(p_ref, buf, k, g, u).start(priority=u % 2)
                return c
            lax.fori_loop(0, ct // SUBLANES, body, 0)

    def wait_rows(buf):
        for k in range(TOP_K):
            pltpu.make_async_copy(rows.at[1 - buf, k], rows.at[buf, k], sem.at[buf]).wait()

    @pl.when(i == 0)
    def _():
        fetch(pos_ref, 0, range(TOP_K))

    fetch(posn_ref, 1 - cur, range(TOP_K - 1))
    wait_rows(cur)

    gates = jnp.transpose(gt_ref[...])
    expert_rows = lambda k: rows[cur, k].reshape(ct, D_MODEL)
    ffn = expert_rows(0) * gates[:, 0:1]
    for k in range(1, TOP_K):
        ffn = ffn + expert_rows(k) * gates[:, k:k + 1]
    for g in range(ct // SUBLANES):
        for u in range(SUBLANES):
            row_copy(posn_ref, 1 - cur, TOP_K - 1, g, u).start(priority=u % 2)
    o_ref[...] = _layer_norm(DN_ALPHA * h1_ref[...] + ffn, g_ref[...], b_ref[...])

    @pl.when(i == n_steps - 1)
    def _():
        wait_rows(1 - cur)


def _combine(h1, ys, pos3, gates_t, g2, b2, tok0, ntok):
    ct = pos3.shape[2] // TOP_K
    assert tok0 % ct == 0 and ntok % ct == 0
    off = tok0 // ct
    n_steps = ntok // ct
    return pl.pallas_call(
        functools.partial(_combine_kernel, n_steps=n_steps),
        grid=(n_steps,),
        in_specs=[pl.BlockSpec((1, 1, TOP_K * ct), lambda i: (off + i, 0, 0), memory_space=pltpu.SMEM),
                  pl.BlockSpec((1, 1, TOP_K * ct), lambda i: (off + jnp.minimum(i + 1, n_steps - 1), 0, 0),
                               memory_space=pltpu.SMEM),
                  pl.BlockSpec((ct, D_MODEL), lambda i: (off + i, 0)),
                  pl.BlockSpec((SUBLANES, ct), lambda i: (0, off + i)),
                  _const_spec((1, D_MODEL)), _const_spec((1, D_MODEL)),
                  pl.BlockSpec(memory_space=pl.ANY)],
        out_specs=pl.BlockSpec((ct, D_MODEL), lambda i: (i, 0)),
        out_shape=jax.ShapeDtypeStruct((ntok, D_MODEL), _F32),
        scratch_shapes=[pltpu.VMEM((2, TOP_K, ct // SUBLANES, SUBLANES, D_MODEL), _F32),
                        pltpu.SemaphoreType.DMA((2,))],
        compiler_params=_params("arbitrary"),
        name="combine_ln2",
    )(pos3, pos3, h1, gates_t, g2, b2, ys)


def _t5_bucket(rel):
    half = N_BUCKETS // 2
    exact = half // 2
    n = jnp.abs(rel)
    large = exact + (jnp.log(jnp.maximum(n, 1).astype(_F32) / exact)
                     / math.log(MAX_DISTANCE / exact) * (half - exact)).astype(jnp.int32)
    large = jnp.minimum(large, half - 1)
    return jnp.where(rel > 0, half, 0) + jnp.where(n < exact, n, large)


def _lookup(table, idx):
    out = jnp.zeros(idx.shape + table.shape[1:], table.dtype)
    expand = (...,) + (None,) * (table.ndim - 1)
    for j in range(table.shape[0]):
        out = out + jnp.where((idx == j)[expand], table[j], jnp.zeros((), table.dtype))
    return out


def _bias_tables(rel_bias, sink, nblk):
    rb = rel_bias.astype(_F32)
    qi = jnp.arange(BLOCK)
    kj = jnp.arange(3 * BLOCK)
    rel_band = kj[None, :] - BLOCK - qi[:, None]
    band = _lookup(rb, _t5_bucket(rel_band)).transpose(2, 0, 1)
    band = jnp.where((jnp.abs(rel_band) <= WINDOW)[None], band, NEG_INF)
    dist = jnp.arange(1, nblk * BLOCK + N_META + 1)
    by_dist = _lookup(rb, _t5_bucket(-dist))
    s = nblk * BLOCK
    meta = jnp.stack([by_dist[N_META - m - 1:N_META - m - 1 + s] for m in range(N_META)], axis=1)
    meta = meta.reshape(nblk, BLOCK, N_META, N_HEADS).transpose(0, 3, 1, 2)
    sink_col = jnp.broadcast_to(sink.astype(_F32)[None, :, None, None], (nblk, N_HEADS, BLOCK, 1))
    pad = jnp.zeros((nblk, N_HEADS, BLOCK, BLOCK - N_META - 1), _F32)
    return band, jnp.concatenate([meta, sink_col, pad], axis=-1)


def _dispatch_tables(top_e, rank, counts, t, n_tiles):
    c = counts.astype(jnp.int32)
    nb = (c + MOE_SUB - 1) // MOE_SUB
    ns = (nb + MOE_SUBS_PER_TILE - 1) // MOE_SUBS_PER_TILE
    ends = jnp.cumsum(ns)
    first_tile = ends - ns

    pos = _lookup(first_tile, top_e[:TOP_K]) * MOE_TILE + rank[:TOP_K]
    blocks = lambda dt: pos.reshape(TOP_K, t // dt, dt).transpose(1, 0, 2).reshape(t // dt, 1, TOP_K * dt)
    pos3 = (blocks(DISPATCH_TOKENS), blocks(COMBINE_TOKENS))
    pad_start = first_tile * MOE_TILE + c
    pad_n = nb * MOE_SUB - c

    tid = jnp.arange(n_tiles, dtype=jnp.int32)
    n_used = ends[-1]
    used = tid < n_used
    te = jnp.minimum(jnp.sum((tid[:, None] >= ends[None, :]).astype(jnp.int32), axis=1), N_EXPERTS - 1)
    last_used = jnp.maximum(n_used - 1, 0)
    te = jnp.where(used, te, te[last_used])
    jt = tid - first_tile[te]
    nsub = jnp.where(used, jnp.clip(nb[te] - jt * MOE_SUBS_PER_TILE, 0, MOE_SUBS_PER_TILE), 0)
    src = jnp.where(used, tid, last_used)
    return te, nsub.astype(jnp.int32), src.astype(jnp.int32), pos3, pad_start, pad_n


def kernel(x_prompt, x_sample, meta_tokens, ln_in_g, ln_in_b, rel_bias, w_in, conv_w, conv_b, lru_wa, lru_ba,
           lru_wi, lru_bi, lru_lam, attn_sink, w_out, ln1_g, ln1_b, router_w, router_b, exp_w1, exp_b1,
           exp_w2, exp_b2, ln2_g, ln2_b):
    assert DEPTH == 1 and w_in.shape[0] == 1
    row = lambda a: a.reshape(1, -1).astype(_F32)
    g0, b0 = row(ln_in_g), row(ln_in_b)
    w_in_b = w_in[0].astype(_BF16)
    w_out_b = w_out[0].astype(_BF16)

    wa, wi = lru_wa[0], lru_wi[0]
    w_cat = jnp.concatenate([wa[0], wa[1], wi[0], wi[1]], axis=-1).astype(_BF16)
    blk = lambda v: v.reshape(LRU_BLOCKS, 1, LRU_BLOCK_W)
    b_cat = jnp.concatenate([blk(lru_ba[0, 0]), blk(lru_ba[0, 1]), blk(lru_bi[0, 0]), blk(lru_bi[0, 1])],
                            axis=-1).astype(_F32)
    c_decay = -LRU_C * jax.nn.softplus(-lru_lam[0].astype(_F32))

    groups = [x_prompt, x_sample]
    nblk_max = max(x.shape[1] for x in groups) // BLOCK
    band_bias, tail_bias = _bias_tables(rel_bias, attn_sink[0], nblk_max)

    _, k_m, v_m, xr_m, _ = _in_proj(meta_tokens.astype(_F32), g0, b0, w_in_b)
    tail = lambda a: jnp.zeros((N_KV_HEADS, BLOCK, HEAD_DIM), _BF16).at[:, :N_META].set(
        a.reshape(N_META, N_KV_HEADS, HEAD_DIM).transpose(1, 0, 2))
    k_tail, v_tail = tail(k_m), tail(v_m)

    xs, attns, recs = [], [], []
    for x in groups:
        bsz, s = x.shape[0], x.shape[1]
        x2 = x.reshape(bsz * s, D_MODEL)
        q, k, v, xr, yg = _in_proj(x2, g0, b0, w_in_b)
        attn = _attention(q.reshape(bsz, s, Q_COLS), k.reshape(bsz, s, KV_COLS), v.reshape(bsz, s, KV_COLS),
                          k_tail, v_tail, band_bias, tail_bias)
        rec = _rg_lru(xr.reshape(bsz, s, LRU_WIDTH), yg.reshape(bsz, s, LRU_WIDTH), xr_m,
                      conv_w[0].astype(_F32), row(conv_b[0]), w_cat, b_cat, c_decay)
        xs.append(x2)
        attns.append(attn.reshape(bsz * s, Q_COLS))
        recs.append(rec.reshape(bsz * s, LRU_WIDTH))

    rw_t = router_w[0].T.astype(_BF16)
    rb = jnp.broadcast_to(router_b[0].astype(_F32)[:, None], (N_EXPERTS, LANES))
    h1, top_e, gates_t, rank, counts = _mix(xs[0], xs[1], attns[0], attns[1], recs[0], recs[1], g0, b0,
                                            w_out_b, row(ln1_g[0]), row(ln1_b[0]), rw_t, rb)

    t = h1.shape[0]
    n_tiles = TOP_K * t // MOE_TILE + N_EXPERTS
    te, nsub, src_tile, pos3, pad_start, pad_n = _dispatch_tables(top_e, rank, counts[:, 0], t, n_tiles)
    xs = _dispatch(pad_start, pad_n, pos3[0], h1, n_tiles * MOE_TILE)
    ys = _moe(te, nsub, src_tile, xs,
              exp_w1[0], exp_b1[0].reshape(N_EXPERTS, 1, 2 * D_FF), exp_w2[0],
              exp_b2[0].reshape(N_EXPERTS, 1, D_MODEL))

    outs = []
    tok0 = 0
    for x in groups:
        bsz, s = x.shape[0], x.shape[1]
        y = _combine(h1, ys, pos3[1], gates_t, row(ln2_g[0]), row(ln2_b[0]), tok0, bsz * s)
        outs.append(y.reshape(bsz, s, D_MODEL))
        tok0 += bsz * s
    return tuple(outs)
```
